```python
import math
import jax
import jax.numpy as jnp
from jax import lax
import numpy as np

D_MODEL = 4096
BATCH = 16
SEQ = 256
DEPTH = 2
DEC_BATCH = 2
DEC_SEQ = 4096
PAST_LEN = 512

GRID_W = 64
N_BRANCH = 3
D_BRANCH = D_MODEL // 4
HY_WIDTH = D_BRANCH
HY_ORDER = 2
HY_BANDS = 16
HY_FEAT = 1 + 2 * HY_BANDS
HY_HID = 64
HY_TARGET = 1e-2
HY_FAST_DECAY_PCT = 0.3
HY_SLOW_DECAY_PCT = 1.5
HEAD_DIM = 128
N_Q_HEADS = D_BRANCH // HEAD_DIM
N_KV_HEADS = 2
GQA_GROUP = N_Q_HEADS // N_KV_HEADS
WINDOW = 128
BLOCK = 128
ROPE_BASE = 10000.0
S5_WIDTH = D_BRANCH
S5_CH = 16
S5_GROUPS = S5_WIDTH // S5_CH
S5_STATE = 64
D_FF = 2 * D_MODEL
EPS = 1e-6
NEG_INF = -1e30

HY_COLS = 3 * HY_WIDTH
Q_COLS = N_Q_HEADS * HEAD_DIM
KV_COLS = N_KV_HEADS * HEAD_DIM
S5_COLS = S5_WIDTH
GATE_COLS = N_BRANCH * D_MODEL
SPLITS = (HY_COLS, HY_COLS + Q_COLS, HY_COLS + Q_COLS + KV_COLS,
          HY_COLS + Q_COLS + 2 * KV_COLS, HY_COLS + Q_COLS + 2 * KV_COLS + S5_COLS)
N_IN = SPLITS[-1] + GATE_COLS

kernel_name = "hybrid_dit_hyena_swa_s5_step"


def rmsnorm(x, g):
    xf = x.astype(jnp.float32)
    y = xf * lax.rsqrt(jnp.mean(xf * xf, axis=-1, keepdims=True) + EPS)
    return (y * g.astype(jnp.float32)).astype(x.dtype)


def dwconv3(x, w, b):
    L = x.shape[1]
    xp = jnp.pad(x, ((0, 0), (1, 1), (0, 0)))
    return xp[:, :L] * w[0] + xp[:, 1:L + 1] * w[1] + xp[:, 2:] * w[2] + b


def axial_rope(x):
    L = x.shape[1]
    rows = L // GRID_W
    row = jnp.repeat(jnp.arange(rows, dtype=jnp.float32), GRID_W)
    col = jnp.tile(jnp.arange(GRID_W, dtype=jnp.float32), rows)
    n_freq = HEAD_DIM // 4
    inv = ROPE_BASE ** (-jnp.arange(n_freq, dtype=jnp.float32) / n_freq)
    ang = jnp.concatenate([row[:, None] * inv, col[:, None] * inv], axis=-1)
    cos = jnp.cos(ang)[None, :, None, :]
    sin = jnp.sin(ang)[None, :, None, :]
    xf = x.astype(jnp.float32)
    x1, x2 = xf[..., :HEAD_DIM // 2], xf[..., HEAD_DIM // 2:]
    return jnp.concatenate([x1 * cos - x2 * sin, x1 * sin + x2 * cos], axis=-1).astype(x.dtype)


def attn_softmax(q, k, v, valid, sink):
    s = jnp.einsum("bqkgd,bskd->bkgqs", q.astype(jnp.float32), k.astype(jnp.float32)) * (HEAD_DIM ** -0.5)
    if valid is not None:
        s = jnp.where(valid, s, NEG_INF)
    sink_col = jnp.broadcast_to(
        sink.reshape(N_KV_HEADS, GQA_GROUP)[None, :, :, None, None].astype(jnp.float32),
        s.shape[:-1] + (1,))
    p = jax.nn.softmax(jnp.concatenate([sink_col, s], axis=-1), axis=-1)[..., 1:]
    o = jnp.einsum("bkgqs,bskd->bqkgd", p, v.astype(jnp.float32))
    return o.astype(q.dtype)


def context_attention(q, k, v, sink):
    B, C = q.shape[:2]
    qg = q.reshape(B, C, N_KV_HEADS, GQA_GROUP, HEAD_DIM)

    def block(i):
        qb = lax.dynamic_slice_in_dim(qg, i * BLOCK, BLOCK, axis=1)
        return attn_softmax(qb, k, v, None, sink)

    o = lax.map(block, jnp.arange(C // BLOCK))
    return jnp.moveaxis(o, 0, 1).reshape(B, C, N_Q_HEADS * HEAD_DIM)


def latent_attention(q, k, v, k_ctx, v_ctx, sink):
    B, L = q.shape[:2]
    qg = q.reshape(B, L, N_KV_HEADS, GQA_GROUP, HEAD_DIM)
    pad = ((0, 0), (BLOCK, BLOCK), (0, 0), (0, 0))
    kp, vp = jnp.pad(k, pad), jnp.pad(v, pad)
    rel = jnp.arange(3 * BLOCK)[None, :] - BLOCK - jnp.arange(BLOCK)[:, None]
    band_ok = jnp.abs(rel) <= WINDOW
    ctx_ok = jnp.ones((BLOCK, k_ctx.shape[1]), dtype=bool)

    def block(i):
        qb = lax.dynamic_slice_in_dim(qg, i * BLOCK, BLOCK, axis=1)
        kb = lax.dynamic_slice_in_dim(kp, i * BLOCK, 3 * BLOCK, axis=1)
        vb = lax.dynamic_slice_in_dim(vp, i * BLOCK, 3 * BLOCK, axis=1)
        kpos = (i - 1) * BLOCK + jnp.arange(3 * BLOCK)
        valid = band_ok & ((kpos >= 0) & (kpos < L))[None, :]
        keys = jnp.concatenate([k_ctx.astype(kb.dtype), kb], axis=1)
        vals = jnp.concatenate([v_ctx.astype(vb.dtype), vb], axis=1)
        return attn_softmax(qb, keys, vals, jnp.concatenate([ctx_ok, valid], axis=1), sink)

    o = lax.map(block, jnp.arange(L // BLOCK))
    return jnp.moveaxis(o, 0, 1).reshape(B, L, N_Q_HEADS * HEAD_DIM)


def hyena_filters(L, w1, b1, w2, b2, w3, freq, decay):
    f32 = jnp.float32
    pos = jnp.arange(L, dtype=f32)
    t = (pos / L)[:, None]
    bands = jnp.linspace(1e-4, HY_BANDS - 1, HY_BANDS, dtype=f32)
    wpos = 2.0 * math.pi * t * bands
    feats = jnp.concatenate([t, jnp.cos(wpos), -jnp.sin(wpos)], axis=-1)
    fr = freq.astype(f32)
    h = jnp.sin(fr * (feats @ w1.astype(f32) + b1.astype(f32)))
    h = jnp.sin(fr * (h @ w2.astype(f32) + b2.astype(f32)))
    h = (h @ w3.astype(f32)) * jnp.exp(-t * jnp.abs(decay.astype(f32)))
    return h.reshape(L, 2, HY_ORDER, HY_WIDTH)


def bidir_fftconv(u, h_fwd, h_bwd, bias):
    L = u.shape[1]
    k = jnp.concatenate([h_fwd, jnp.zeros((1, h_fwd.shape[1]), jnp.float32), h_bwd[:0:-1]], axis=0)
    K = jnp.fft.rfft(k, axis=0)
    U = jnp.fft.rfft(u.astype(jnp.float32), n=2 * L, axis=1)
    y = jnp.fft.irfft(U * K[None], n=2 * L, axis=1)[:, :L]
    return (y + u.astype(jnp.float32) * bias.astype(jnp.float32)).astype(u.dtype)


def hyena_mixer(z, lp):
    L = z.shape[1]
    z = dwconv3(z, lp["hy_conv_w"], lp["hy_conv_b"])
    v, x1, x2 = jnp.split(z, 3, axis=-1)
    h = hyena_filters(L, lp["hy_w1"], lp["hy_b1"], lp["hy_w2"], lp["hy_b2"], lp["hy_w3"],
                      lp["hy_freq"], lp["hy_decay"])
    y = x1 * bidir_fftconv(v, h[:, 0, 0], h[:, 1, 0], lp["hy_bias"][0])
    y = x2 * bidir_fftconv(y, h[:, 0, 1], h[:, 1, 1], lp["hy_bias"][1])
    return y


def _complex_affine_combine(e1, e2):
    a1r, a1i, b1r, b1i = e1
    a2r, a2i, b2r, b2i = e2
    return (a2r * a1r - a2i * a1i, a2r * a1i + a2i * a1r,
            a2r * b1r - a2i * b1i + b2r, a2r * b1i + a2i * b1r + b2i)


def s5_direction(u, lam_re, lam_im, log_dt, b_re, b_im, c_re, c_im, reverse, h0):
    f32 = jnp.float32
    L = u.shape[1]
    lr, li = lam_re.astype(f32), lam_im.astype(f32)
    dt = jnp.exp(log_dt.astype(f32))[:, None]
    zr, zi = lr * dt, li * dt
    mag = jnp.exp(zr)
    ar, ai = mag * jnp.cos(zi), mag * jnp.sin(zi)
    den = lr * lr + li * li
    cr = ((ar - 1.0) * lr + ai * li) / den
    ci = (ai * lr - (ar - 1.0) * li) / den
    bur = jnp.einsum("blgc,gpc->blgp", u, b_re.astype(f32))
    bui = jnp.einsum("blgc,gpc->blgp", u, b_im.astype(f32))
    br, bi = cr * bur - ci * bui, cr * bui + ci * bur
    shape = br.shape
    _, _, hr, hi = lax.associative_scan(
        _complex_affine_combine,
        (jnp.broadcast_to(ar, shape), jnp.broadcast_to(ai, shape), br, bi),
        reverse=reverse, axis=1)
    if h0 is not None:
        n = (L - jnp.arange(L) if reverse else jnp.arange(L) + 1).astype(f32)[:, None, None]
        pm = jnp.exp(n * zr)
        pr, pi = pm * jnp.cos(n * zi), pm * jnp.sin(n * zi)
        h0r = h0[0].astype(f32)[:, None]
        h0i = h0[1].astype(f32)[:, None]
        hr, hi = hr + pr * h0r - pi * h0i, hi + pr * h0i + pi * h0r
    y = jnp.einsum("blgp,gcp->blgc", hr, c_re.astype(f32)) - jnp.einsum("blgp,gcp->blgc", hi, c_im.astype(f32))
    return y, hr, hi


def s5_mixer(u, lp, h0):
    f32 = jnp.float32
    B, L, _ = u.shape
    ug = u.astype(f32).reshape(B, L, S5_GROUPS, S5_CH)
    ys, fin_re, fin_im = [], [], []
    for d, rev in ((0, False), (1, True)):
        init = None if h0 is None else (h0[0][:, d], h0[1][:, d])
        y, hr, hi = s5_direction(ug, lp["s5_lam_re"][d], lp["s5_lam_im"][d], lp["s5_log_dt"][d],
                                 lp["s5_b_re"][d], lp["s5_b_im"][d], lp["s5_c_re"][d], lp["s5_c_im"][d],
                                 rev, init)
        ys.append(y)
        last = 0 if rev else L - 1
        fin_re.append(hr[:, last])
        fin_im.append(hi[:, last])
    y = (ys[0] + ys[1]).reshape(B, L, S5_WIDTH) + u.astype(f32) * lp["s5_d"].astype(f32)
    y = jax.nn.gelu(y)
    y = y * jax.nn.sigmoid(y @ lp["s5_glu_w"].astype(f32) + lp["s5_glu_b"].astype(f32))
    return y.astype(u.dtype), jnp.stack(fin_re, axis=1), jnp.stack(fin_im, axis=1)


def token_mixer(h, lp, ctx):
    B, L, _ = h.shape
    z = h @ lp["w_in"]
    z_hy, z_q, z_k, z_v, z_s5, z_g = jnp.split(z, SPLITS, axis=-1)
    q = z_q.reshape(B, L, N_Q_HEADS, HEAD_DIM)
    k = z_k.reshape(B, L, N_KV_HEADS, HEAD_DIM)
    v = z_v.reshape(B, L, N_KV_HEADS, HEAD_DIM)
    y_hy = hyena_mixer(z_hy, lp)
    if ctx is None:
        y_at = context_attention(q, k, v, lp["attn_sink"])
        y_s5, s_re, s_im = s5_mixer(z_s5, lp, None)
        new = (k, v, s_re, s_im)
    else:
        k_ctx, v_ctx, s_re0, s_im0 = ctx
        y_at = latent_attention(axial_rope(q), axial_rope(k), v, k_ctx, v_ctx, lp["attn_sink"])
        y_s5, _, _ = s5_mixer(z_s5, lp, (s_re0, s_im0))
        new = None
    g = jax.nn.sigmoid(z_g.astype(jnp.float32)).astype(h.dtype).reshape(B, L, N_BRANCH, D_MODEL)
    merged = (g[:, :, 0] * (y_hy @ lp["w_branch_hy"])
              + g[:, :, 1] * (y_at @ lp["w_branch_attn"])
              + g[:, :, 2] * (y_s5 @ lp["w_branch_s5"]))
    return merged @ lp["w_out"], new


def conv_ffn(h, lp):
    u = dwconv3(h @ lp["ffn_w_up"], lp["ffn_conv_w"], lp["ffn_conv_b"])
    a, b = jnp.split(u, 2, axis=-1)
    return (jax.nn.gelu(a) * b) @ lp["ffn_w_down"]


def layer(x, mod, lp, ctx):
    sh1, sc1, g1, sh2, sc2, g2 = [mod[:, None, i] for i in range(6)]
    h = rmsnorm(x, lp["norm_g"][0]) * (1.0 + sc1) + sh1
    y, new = token_mixer(h, lp, ctx)
    x = x + g1 * rmsnorm(y, lp["norm_g"][1])
    h = rmsnorm(x, lp["norm_g"][2]) * (1.0 + sc2) + sh2
    x = x + g2 * rmsnorm(conv_ffn(h, lp), lp["norm_g"][3])
    return x, new


def setup_inputs(seed: int = 0) -> dict:
    key = jax.random.key(seed)
    ks = iter(jax.random.split(key, 48))
    f32 = jnp.float32

    def nrm(shape, scale):
        return scale * jax.random.normal(next(ks), shape, f32)

    max_decay = math.log(HY_TARGET) / HY_FAST_DECAY_PCT
    min_decay = math.log(HY_TARGET) / HY_SLOW_DECAY_PCT
    decay0 = jnp.tile(jnp.linspace(min_decay, max_decay, HY_WIDTH, dtype=f32), 2 * HY_ORDER)
    lam_im0 = math.pi * jnp.arange(S5_STATE, dtype=f32)
    cplx_b = (0.5 / S5_CH) ** 0.5
    cplx_c = (0.5 / S5_STATE) ** 0.5
    return {
        "x_prompt": nrm((BATCH, SEQ, D_MODEL), 1.0),
        "x_sample": nrm((DEC_BATCH, DEC_SEQ, D_MODEL), 1.0),
        "c": nrm((DEC_BATCH, D_MODEL), 1.0),
        "cache_k": nrm((DEC_BATCH, DEPTH, PAST_LEN, N_KV_HEADS, HEAD_DIM), 1.0),
        "cache_v": nrm((DEC_BATCH, DEPTH, PAST_LEN, N_KV_HEADS, HEAD_DIM), 1.0),
        "state_ssm_re": nrm((DEC_BATCH, DEPTH, 2, S5_GROUPS, S5_STATE), 0.1),
        "state_ssm_im": nrm((DEC_BATCH, DEPTH, 2, S5_GROUPS, S5_STATE), 0.1),
        "c_ctx": nrm((D_MODEL,), 1.0),
        "w_mod": nrm((DEPTH, D_MODEL, 6 * D_MODEL), 0.1 * D_MODEL ** -0.5),
        "b_mod": nrm((DEPTH, 6 * D_MODEL), 0.01),
        "norm_g": 1.0 + nrm((DEPTH, 4, D_MODEL), 0.02),
        "w_in": nrm((DEPTH, D_MODEL, N_IN), D_MODEL ** -0.5),
        "hy_conv_w": nrm((DEPTH, 3, HY_COLS), 3 ** -0.5),
        "hy_conv_b": nrm((DEPTH, HY_COLS), 0.02),
        "hy_w1": nrm((DEPTH, HY_FEAT, HY_HID), HY_FEAT ** -0.5),
        "hy_b1": nrm((DEPTH, HY_HID), 0.1),
        "hy_w2": nrm((DEPTH, HY_HID, HY_HID), HY_HID ** -0.5),
        "hy_b2": nrm((DEPTH, HY_HID), 0.1),
        "hy_w3": nrm((DEPTH, HY_HID, 2 * HY_ORDER * HY_WIDTH), 0.1 * HY_HID ** -0.5),
        "hy_freq": 1.0 + nrm((DEPTH, HY_HID), 0.1),
        "hy_decay": decay0[None] + nrm((DEPTH, 2 * HY_ORDER * HY_WIDTH), 0.05),
        "hy_bias": nrm((DEPTH, HY_ORDER, HY_WIDTH), 1.0),
        "attn_sink": nrm((DEPTH, N_Q_HEADS), 1.0),
        "s5_lam_re": -0.5 + nrm((DEPTH, 2, S5_GROUPS, S5_STATE), 0.01),
        "s5_lam_im": lam_im0 + nrm((DEPTH, 2, S5_GROUPS, S5_STATE), 0.01),
        "s5_log_dt": jax.random.uniform(next(ks), (DEPTH, 2, S5_GROUPS), f32,
                                        minval=math.log(1e-3), maxval=math.log(1e-1)),
        "s5_b_re": nrm((DEPTH, 2, S5_GROUPS, S5_STATE, S5_CH), cplx_b),
        "s5_b_im": nrm((DEPTH, 2, S5_GROUPS, S5_STATE, S5_CH), cplx_b),
        "s5_c_re": nrm((DEPTH, 2, S5_GROUPS, S5_CH, S5_STATE), cplx_c),
        "s5_c_im": nrm((DEPTH, 2, S5_GROUPS, S5_CH, S5_STATE), cplx_c),
        "s5_d": nrm((DEPTH, S5_WIDTH), 1.0),
        "s5_glu_w": nrm((DEPTH, S5_WIDTH, S5_WIDTH), S5_WIDTH ** -0.5),
        "s5_glu_b": nrm((DEPTH, S5_WIDTH), 0.01),
        "w_branch_hy": nrm((DEPTH, HY_WIDTH, D_MODEL), HY_WIDTH ** -0.5),
        "w_branch_attn": nrm((DEPTH, Q_COLS, D_MODEL), Q_COLS ** -0.5),
        "w_branch_s5": nrm((DEPTH, S5_WIDTH, D_MODEL), S5_WIDTH ** -0.5),
        "w_out": nrm((DEPTH, D_MODEL, D_MODEL), D_MODEL ** -0.5),
        "ffn_w_up": nrm((DEPTH, D_MODEL, 2 * D_FF), D_MODEL ** -0.5),
        "ffn_conv_w": nrm((DEPTH, 3, 2 * D_FF), 3 ** -0.5),
        "ffn_conv_b": nrm((DEPTH, 2 * D_FF), 0.02),
        "ffn_w_down": nrm((DEPTH, D_FF, D_MODEL), D_FF ** -0.5),
    }


def reference(x_prompt, x_sample, c, cache_k, cache_v, state_ssm_re, state_ssm_im, c_ctx,
              w_mod, b_mod, norm_g, w_in, hy_conv_w, hy_conv_b, hy_w1, hy_b1, hy_w2, hy_b2, hy_w3,
              hy_freq, hy_decay, hy_bias, attn_sink, s5_lam_re, s5_lam_im, s5_log_dt, s5_b_re, s5_b_im,
              s5_c_re, s5_c_im, s5_d, s5_glu_w, s5_glu_b, w_branch_hy, w_branch_attn, w_branch_s5, w_out,
              ffn_w_up, ffn_conv_w, ffn_conv_b, ffn_w_down):
    def layer_params(l):
        return {
            "norm_g": norm_g[l], "w_in": w_in[l],
            "hy_conv_w": hy_conv_w[l], "hy_conv_b": hy_conv_b[l],
            "hy_w1": hy_w1[l], "hy_b1": hy_b1[l], "hy_w2": hy_w2[l], "hy_b2": hy_b2[l],
            "hy_w3": hy_w3[l], "hy_freq": hy_freq[l], "hy_decay": hy_decay[l], "hy_bias": hy_bias[l],
            "attn_sink": attn_sink[l],
            "s5_lam_re": s5_lam_re[l], "s5_lam_im": s5_lam_im[l], "s5_log_dt": s5_log_dt[l],
            "s5_b_re": s5_b_re[l], "s5_b_im": s5_b_im[l], "s5_c_re": s5_c_re[l], "s5_c_im": s5_c_im[l],
            "s5_d": s5_d[l], "s5_glu_w": s5_glu_w[l], "s5_glu_b": s5_glu_b[l],
            "w_branch_hy": w_branch_hy[l], "w_branch_attn": w_branch_attn[l],
            "w_branch_s5": w_branch_s5[l], "w_out": w_out[l],
            "ffn_w_up": ffn_w_up[l], "ffn_conv_w": ffn_conv_w[l], "ffn_conv_b": ffn_conv_b[l],
            "ffn_w_down": ffn_w_down[l],
        }

    x = x_prompt
    ks_, vs_, srs_, sis_ = [], [], [], []
    for l in range(DEPTH):
        mod_ctx = (jax.nn.silu(c_ctx)[None] @ w_mod[l] + b_mod[l]).reshape(1, 6, D_MODEL)
        x, (k_l, v_l, sr_l, si_l) = layer(x, mod_ctx, layer_params(l), None)
        ks_.append(k_l)
        vs_.append(v_l)
        srs_.append(sr_l)
        sis_.append(si_l)
    y_prompt = x
    new_cache_k = jnp.stack(ks_, axis=1)
    new_cache_v = jnp.stack(vs_, axis=1)
    new_state_ssm_re = jnp.stack(srs_, axis=1)
    new_state_ssm_im = jnp.stack(sis_, axis=1)

    xs = x_sample
    for l in range(DEPTH):
        mod_lat = (jax.nn.silu(c) @ w_mod[l] + b_mod[l]).reshape(-1, 6, D_MODEL)
        ctx = (cache_k[:, l], cache_v[:, l], state_ssm_re[:, l], state_ssm_im[:, l])
        xs, _ = layer(xs, mod_lat, layer_params(l), ctx)
    y_sample = xs
    return (y_prompt, y_sample, new_cache_k, new_cache_v, new_state_ssm_re, new_state_ssm_im)
```

```python
import functools
import math

import numpy as np
import jax
import jax.numpy as jnp
from jax import lax
from jax.experimental import pallas as pl
from jax.experimental.pallas import tpu as pltpu

F32 = jnp.float32
BF16 = jnp.bfloat16

D_MODEL = 4096
DEPTH = 2
GRID_W = 64
D_BRANCH = D_MODEL // 4
HY_WIDTH = D_BRANCH
HY_BANDS = 16
HY_FEAT = 1 + 2 * HY_BANDS
HY_HID = 64
HEAD_DIM = 128
N_Q_HEADS = D_BRANCH // HEAD_DIM
N_KV_HEADS = 2
GQA_GROUP = N_Q_HEADS // N_KV_HEADS
WINDOW = 128
BLOCK = 128
ROPE_BASE = 10000.0
S5_CH = 16
S5_GROUPS = D_BRANCH // S5_CH
S5_STATE = 64
S5_CHUNK = 16
S5_K = S5_CHUNK * S5_CH
D_FF = 2 * D_MODEL
EPS = 1e-6
NEG_INF = -1e30

HY_COLS = 3 * HY_WIDTH
Q_COL0 = HY_COLS
K_COL0 = Q_COL0 + N_Q_HEADS * HEAD_DIM
KV_COLS = N_KV_HEADS * HEAD_DIM
V_COL0 = K_COL0 + KV_COLS
S5_COL0 = V_COL0 + KV_COLS
GATE_COL0 = S5_COL0 + D_BRANCH
N_IN = GATE_COL0 + 3 * D_MODEL

V7X_VMEM_REQUEST_MAX = 60 * 1024 * 1024
LANES = 128
MOD_ROWS = 8


def _nbytes(shape, dtype):
    return math.prod(shape) * jnp.dtype(dtype).itemsize


def _cparams(n_grid, *block_bytes, extra=0):
    est = 2 * sum(block_bytes) + extra + (4 << 20)
    return pltpu.CompilerParams(
        dimension_semantics=("arbitrary",) * n_grid,
        vmem_limit_bytes=int(min(max(est, 16 << 20), V7X_VMEM_REQUEST_MAX)))


def _dot(a, b):
    return jnp.dot(a, b, preferred_element_type=F32)


def _dot_nt(a, b):
    return lax.dot_general(a, b, (((1,), (1,)), ((), ())), preferred_element_type=F32)


def _split(a):
    hi = a.astype(BF16)
    return hi, (a - hi.astype(F32)).astype(BF16)


def _dot3(a, b):
    ah, al = _split(a)
    bh, bl = _split(b)
    return _dot(ah, bh) + (_dot(ah, bl) + _dot(al, bh))


def _sigmoid(x):
    return 1.0 / (1.0 + jnp.exp(-x))


def _gelu(x):
    return 0.5 * x * (1.0 + jnp.tanh(math.sqrt(2.0 / math.pi) * (x + 0.044715 * (x * x * x))))


def _rms(x, g):
    return x * lax.rsqrt(jnp.mean(x * x, axis=-1, keepdims=True) + EPS) * g


class _Cfg:
    def __init__(self, nc, lc, nl, ll):
        self.nc, self.lc, self.nl, self.ll = nc, lc, nl, ll
        self.rc = nc * lc
        self.rl = nl * ll
        self.r = self.rc + self.rl
        self.nseg = 1 + nl
        assert self.rc % ll == 0 and ll % lc == 0 and lc % BLOCK == 0
        self.tm = min(1024, ll)
        self.rb = ll

    def seg_of_row(self, row0):
        return jnp.where(row0 >= self.rc, (row0 - self.rc) // self.ll + 1, 0)


def _mod_kernel(c_ref, w_ref, b_ref, o_ref):
    c = c_ref[...]
    s = (c * _sigmoid(c)).astype(BF16)
    o_ref[...] = _dot(s, w_ref[...].astype(BF16)) + b_ref[...]


def _mod(cvec, w_mod, b_mod):
    n = w_mod.shape[-1]
    tn = 512
    return pl.pallas_call(
        _mod_kernel,
        grid=(DEPTH, n // tn),
        in_specs=[pl.BlockSpec((MOD_ROWS, D_MODEL), lambda l, j: (0, 0)),
                  pl.BlockSpec((None, D_MODEL, tn), lambda l, j: (l, 0, j)),
                  pl.BlockSpec((None, 1, tn), lambda l, j: (l, 0, j))],
        out_specs=pl.BlockSpec((None, MOD_ROWS, tn), lambda l, j: (l, 0, j)),
        out_shape=jax.ShapeDtypeStruct((DEPTH, MOD_ROWS, n), F32),
        compiler_params=_cparams(2, _nbytes((D_MODEL, tn), F32), extra=_nbytes((D_MODEL, tn), BF16)),
        name="mod",
    )(cvec, w_mod, b_mod.reshape(DEPTH, 1, n))


def _resid_norm_kernel(*refs, has_y, has_h):
    it = iter(refs)
    x_ref = next(it)
    if has_y:
        y_ref, gate_ref, gpost_ref = next(it), next(it), next(it)
    if has_h:
        gpre_ref, sc_ref, sh_ref = next(it), next(it), next(it)
    x = x_ref[...]
    if has_y:
        x = x + gate_ref[...] * _rms(y_ref[...], gpost_ref[...])
        xo_ref = next(it)
        xo_ref[...] = x
    if has_h:
        ho_ref = next(it)
        ho_ref[...] = (_rms(x, gpre_ref[...]) * (1.0 + sc_ref[...]) + sh_ref[...]).astype(BF16)


def _resid_norm(cfg, x, y=None, gate=None, gpost=None, gpre=None, sc=None, sh=None):
    has_y, has_h = y is not None, gpre is not None
    tm = 256
    row = pl.BlockSpec((tm, D_MODEL), lambda i: (i, 0))
    vec = pl.BlockSpec((1, D_MODEL), lambda i: (0, 0))
    seg = pl.BlockSpec((None, 1, D_MODEL), lambda i: (cfg.seg_of_row(i * tm), 0, 0))
    args, specs, outs, ospecs = [x], [row], [], []
    if has_y:
        args += [y, gate, gpost.reshape(1, D_MODEL)]
        specs += [row, seg, vec]
        outs.append(jax.ShapeDtypeStruct((cfg.r, D_MODEL), F32))
        ospecs.append(row)
    if has_h:
        args += [gpre.reshape(1, D_MODEL), sc, sh]
        specs += [vec, seg, seg]
        outs.append(jax.ShapeDtypeStruct((cfg.r, D_MODEL), BF16))
        ospecs.append(row)
    res = pl.pallas_call(
        functools.partial(_resid_norm_kernel, has_y=has_y, has_h=has_h),
        grid=(cfg.r // tm,),
        in_specs=specs, out_specs=ospecs, out_shape=outs,
        compiler_params=_cparams(1, 4 * _nbytes((tm, D_MODEL), F32)),
        name="resid_norm",
    )(*args)
    return res if len(res) > 1 else res[0]


def _mm_kernel(x_ref, w_ref, o_ref, wb_ref):
    @pl.when(pl.program_id(1) == 0)
    def _():
        wb_ref[...] = w_ref[...].astype(BF16)

    o_ref[...] = _dot(x_ref[...], wb_ref[...]).astype(o_ref.dtype)


def _mm(x, w, l, *, tm, tn, out_dtype=F32):
    m, k = x.shape
    n = w.shape[-1]
    return pl.pallas_call(
        _mm_kernel,
        grid=(n // tn, m // tm),
        in_specs=[pl.BlockSpec((tm, k), lambda j, i: (i, 0)),
                  pl.BlockSpec((None, k, tn), lambda j, i: (l, 0, j))],
        out_specs=pl.BlockSpec((tm, tn), lambda j, i: (i, j)),
        out_shape=jax.ShapeDtypeStruct((m, n), out_dtype),
        scratch_shapes=[pltpu.VMEM((k, tn), BF16)],
        compiler_params=_cparams(2, _nbytes((tm, k), BF16), _nbytes((k, tn), F32), _nbytes((tm, tn), out_dtype),
                                 extra=_nbytes((k, tn), BF16) + _nbytes((tm, tn), F32)),
        name="mm",
    )(x, w)


def _merge_kernel(g0, g1, g2, y0, y1, y2, w0, w1, w2, o_ref, wb_ref):
    @pl.when(pl.program_id(1) == 0)
    def _():
        wb_ref[0] = w0[...].astype(BF16)
        wb_ref[1] = w1[...].astype(BF16)
        wb_ref[2] = w2[...].astype(BF16)

    acc = _sigmoid(g0[...]) * _dot(y0[...], wb_ref[0])
    acc += _sigmoid(g1[...]) * _dot(y1[...], wb_ref[1])
    acc += _sigmoid(g2[...]) * _dot(y2[...], wb_ref[2])
    o_ref[...] = acc.astype(o_ref.dtype)


def _merge(cfg, z, ys, ws, l):
    tm, tn = cfg.tm, 512
    gate_specs = [pl.BlockSpec((tm, tn), functools.partial(
        lambda j, i, b: (i, (GATE_COL0 + b * D_MODEL) // tn + j), b=b)) for b in range(3)]
    y_spec = pl.BlockSpec((tm, D_BRANCH), lambda j, i: (i, 0))
    w_spec = pl.BlockSpec((None, D_BRANCH, tn), lambda j, i: (l, 0, j))
    return pl.pallas_call(
        _merge_kernel,
        grid=(D_MODEL // tn, cfg.r // tm),
        in_specs=gate_specs + [y_spec] * 3 + [w_spec] * 3,
        out_specs=pl.BlockSpec((tm, tn), lambda j, i: (i, j)),
        out_shape=jax.ShapeDtypeStruct((cfg.r, D_MODEL), BF16),
        scratch_shapes=[pltpu.VMEM((3, D_BRANCH, tn), BF16)],
        compiler_params=_cparams(2, 3 * _nbytes((tm, tn), F32), 3 * _nbytes((tm, D_BRANCH), BF16),
                                 3 * _nbytes((D_BRANCH, tn), F32), _nbytes((tm, tn), BF16),
                                 extra=3 * _nbytes((D_BRANCH, tn), BF16) + 2 * _nbytes((tm, tn), F32)),
        name="merge",
    )(z, z, z, *ys, *ws)


def _seq_pos(cfg, shape):
    sl = jnp.where(pl.program_id(0) < cfg.rc // cfg.rb, cfg.lc, cfg.ll)
    return lax.broadcasted_iota(jnp.int32, shape, 0) & (sl - 1), sl


def _dwconv3(x, w_ref, b_ref, t, sl):
    rows = x.shape[0]
    xm = jnp.where(t == 0, 0.0, pltpu.roll(x, 1, 0))
    xp = jnp.where(t == sl - 1, 0.0, pltpu.roll(x, rows - 1, 0))
    return xm * w_ref[0:1, :] + x * w_ref[1:2, :] + xp * w_ref[2:3, :] + b_ref[...]


def _hy_dwconv_kernel(x_ref, w_ref, b_ref, o_ref, *, cfg):
    x = x_ref[...]
    t, sl = _seq_pos(cfg, x.shape)
    o_ref[...] = _dwconv3(x, w_ref, b_ref, t, sl)


def _hy_dwconv(cfg, z, w, b, l):
    tn = 256
    blk = _nbytes((cfg.rb, tn), F32)
    return pl.pallas_call(
        functools.partial(_hy_dwconv_kernel, cfg=cfg),
        grid=(cfg.r // cfg.rb, HY_COLS // tn),
        in_specs=[pl.BlockSpec((cfg.rb, tn), lambda r, j: (r, j)),
                  pl.BlockSpec((None, 3, tn), lambda r, j: (l, 0, j)),
                  pl.BlockSpec((None, 1, tn), lambda r, j: (l, 0, j))],
        out_specs=pl.BlockSpec((cfg.rb, tn), lambda r, j: (r, j)),
        out_shape=jax.ShapeDtypeStruct((cfg.r, HY_COLS), F32),
        compiler_params=_cparams(2, 2 * blk, extra=4 * blk),
        name="hy_dwconv",
    )(z, w, b.reshape(DEPTH, 1, HY_COLS))


def _ffn_act_kernel(a_ref, b_ref, wa_ref, wb_ref, ba_ref, bb_ref, o_ref, *, cfg):
    a = a_ref[...]
    t, sl = _seq_pos(cfg, a.shape)
    a = _dwconv3(a, wa_ref, ba_ref, t, sl)
    b = _dwconv3(b_ref[...], wb_ref, bb_ref, t, sl)
    o_ref[...] = (_gelu(a) * b).astype(BF16)


def _ffn_act(cfg, u, w, b, l):
    tn = 256
    nb = D_FF // tn
    blk = _nbytes((cfg.rb, tn), F32)
    b3 = b.reshape(DEPTH, 1, 2 * D_FF)
    return pl.pallas_call(
        functools.partial(_ffn_act_kernel, cfg=cfg),
        grid=(cfg.r // cfg.rb, nb),
        in_specs=[pl.BlockSpec((cfg.rb, tn), lambda r, j: (r, j)),
                  pl.BlockSpec((cfg.rb, tn), lambda r, j: (r, j + nb)),
                  pl.BlockSpec((None, 3, tn), lambda r, j: (l, 0, j)),
                  pl.BlockSpec((None, 3, tn), lambda r, j: (l, 0, j + nb)),
                  pl.BlockSpec((None, 1, tn), lambda r, j: (l, 0, j)),
                  pl.BlockSpec((None, 1, tn), lambda r, j: (l, 0, j + nb))],
        out_specs=pl.BlockSpec((cfg.rb, tn), lambda r, j: (r, j)),
        out_shape=jax.ShapeDtypeStruct((cfg.r, D_FF), BF16),
        compiler_params=_cparams(2, 3 * blk, extra=6 * blk),
        name="ffn_act",
    )(u, u, w, w, b3, b3)


def _dft_tables(length):
    period = 4 * length
    r = 1 << ((length.bit_length()) // 2)
    idx = np.arange(length, dtype=np.int64)

    def cs(m):
        ang = 2.0 * np.pi * (m % period).astype(np.float64) / period
        return jnp.asarray(np.cos(ang), F32), jnp.asarray(np.sin(ang), F32)

    def combine(ca, sa, cb, sb):
        c = ca[:, :, None] * cb[:, None, :] - sa[:, :, None] * sb[:, None, :]
        s = sa[:, :, None] * cb[:, None, :] + ca[:, :, None] * sb[:, None, :]
        return c.reshape(length, length), s.reshape(length, length)

    odd = 2 * idx + 1
    ca, sa = cs(odd[:, None] * (np.arange(length // r) * r)[None, :])
    cb, sb = cs(odd[:, None] * np.arange(r)[None, :])
    fc, fs = combine(ca, sa, cb, sb)
    ca, sa = cs(idx[:, None] * (2 * r * np.arange(length // r))[None, :])
    cb, sb = cs(idx[:, None] * (2 * np.arange(r) + 1)[None, :])
    fct, fst = combine(ca, sa, cb, sb)
    return fc, fs, fct, fst


def _hy_feats(length):
    pos = jnp.arange(length, dtype=F32)
    t = (pos / length)[:, None]
    bands = jnp.linspace(1e-4, HY_BANDS - 1, HY_BANDS, dtype=F32)
    wpos = 2.0 * math.pi * t * bands
    feats = jnp.concatenate([t, jnp.cos(wpos), -jnp.sin(wpos)], axis=-1)
    return jnp.pad(feats, ((0, 0), (0, LANES - HY_FEAT)))


def _hy_filter_kernel(f_ref, w1_ref, b1_ref, w2_ref, b2_ref, w3_ref, fr_ref, dec_ref, hs_ref, hd_ref, hb0_ref):
    feats = f_ref[...]
    fr = fr_ref[...]
    h = jnp.sin(fr * (_dot3(feats, w1_ref[...]) + b1_ref[...]))
    h = jnp.sin(fr * (_dot3(h, w2_ref[...]) + b2_ref[...]))
    h = _dot3(h, w3_ref[...]) * jnp.exp(-feats[:, 0:1] * jnp.abs(dec_ref[...]))
    half = 2 * HY_WIDTH
    hf, hb = h[:, :half], h[:, half:]
    hs_ref[...] = hf + hb
    hd_ref[...] = hb - hf

    @pl.when(pl.program_id(0) == 0)
    def _():
        hb0_ref[...] = jnp.broadcast_to(hb[0:1, :], hb0_ref.shape)


def _hy_filter(length, feats, w1p, b1, w2, b2, w3, freq, decay, l):
    tl = 256
    half = 2 * HY_WIDTH
    lsel = lambda i: (l, 0, 0)
    return pl.pallas_call(
        _hy_filter_kernel,
        grid=(length // tl,),
        in_specs=[pl.BlockSpec((tl, LANES), lambda i: (i, 0)),
                  pl.BlockSpec((None, LANES, HY_HID), lsel),
                  pl.BlockSpec((None, 1, HY_HID), lsel),
                  pl.BlockSpec((None, HY_HID, HY_HID), lsel),
                  pl.BlockSpec((None, 1, HY_HID), lsel),
                  pl.BlockSpec((None, HY_HID, 2 * half), lsel),
                  pl.BlockSpec((None, 1, HY_HID), lsel),
                  pl.BlockSpec((None, 1, 2 * half), lsel)],
        out_specs=[pl.BlockSpec((tl, half), lambda i: (i, 0)),
                   pl.BlockSpec((tl, half), lambda i: (i, 0)),
                   pl.BlockSpec((8, half), lambda i: (0, 0))],
        out_shape=[jax.ShapeDtypeStruct((length, half), F32),
                   jax.ShapeDtypeStruct((length, half), F32),
                   jax.ShapeDtypeStruct((8, half), F32)],
        compiler_params=_cparams(1, 2 * _nbytes((tl, half), F32), extra=6 * _nbytes((tl, 2 * half), F32)),
        name="hy_filter",
    )(feats, w1p, b1, w2, b2, w3, freq, decay)


def _hy_spec_kernel(fch, fcl, fsh, fsl, hs_ref, hd_ref, hb0_ref, kre_ref, kim_ref, sb_ref):
    @pl.when(pl.program_id(1) == 0)
    def _():
        sb_ref[0], sb_ref[1] = _split(hs_ref[...])
        sb_ref[2], sb_ref[3] = _split(hd_ref[...])

    kre_ref[...] = (_dot(fch[...], sb_ref[0]) + (_dot(fch[...], sb_ref[1]) + _dot(fcl[...], sb_ref[0]))
                    - hb0_ref[0:1, :])
    kim_ref[...] = _dot(fsh[...], sb_ref[2]) + (_dot(fsh[...], sb_ref[3]) + _dot(fsl[...], sb_ref[2]))


def _hy_spec(length, mats, hs, hd, hb0):
    tk = min(256, length)
    tn = 256
    half = 2 * HY_WIDTH
    fspec = pl.BlockSpec((tk, length), lambda j, i: (i, 0))
    hspec = pl.BlockSpec((length, tn), lambda j, i: (0, j))
    kspec = pl.BlockSpec((tk, tn), lambda j, i: (i, j))
    return pl.pallas_call(
        _hy_spec_kernel,
        grid=(half // tn, length // tk),
        in_specs=[fspec] * 4 + [hspec, hspec, pl.BlockSpec((8, tn), lambda j, i: (0, j))],
        out_specs=[kspec, kspec],
        out_shape=[jax.ShapeDtypeStruct((length, half), F32)] * 2,
        scratch_shapes=[pltpu.VMEM((4, length, tn), BF16)],
        compiler_params=_cparams(2, 4 * _nbytes((tk, length), BF16), 2 * _nbytes((length, tn), F32),
                                 extra=4 * _nbytes((length, tn), BF16) + 2 * _nbytes((length, tn), F32)),
        name="hy_spec",
    )(mats["fc_hi"], mats["fc_lo"], mats["fs_hi"], mats["fs_lo"], hs, hd, hb0)


def _hy_fwd_kernel(fc, fs, u_ref, kre, kim, yre_ref, yim_ref, ub_ref):
    @pl.when(pl.program_id(2) == 0)
    def _():
        ub_ref[...] = u_ref[...].astype(BF16)

    ure = _dot(fc[...], ub_ref[...])
    uim = -_dot(fs[...], ub_ref[...])
    yre_ref[...] = (ure * kre[...] - uim * kim[...]).astype(BF16)
    yim_ref[...] = (ure * kim[...] + uim * kre[...]).astype(BF16)


def _hy_inv_kernel(fct, fst, yre, yim, u_ref, gate_ref, bias_ref, o_ref, *, inv_len):
    acc = _dot(fct[...], yre[...]) - _dot(fst[...], yim[...])
    o_ref[...] = (gate_ref[...] * (acc * inv_len + u_ref[...] * bias_ref[...])).astype(o_ref.dtype)


def _hy_conv(length, nseq, mats, kre, kim, order, data, data_rb0, data_cb0, zc, gate_rb0, gate_col0, bias, l,
             out_dtype):
    tn = min(512, HY_WIDTH)
    tk = min(512, length)
    ncb = HY_WIDTH // tn
    ntk = length // tk
    fspec = pl.BlockSpec((tk, length), lambda s, j, i: (i, 0))
    kspec = pl.BlockSpec((tk, tn), lambda s, j, i: (i, order * ncb + j))
    yspec = pl.BlockSpec((tk, tn), lambda s, j, i: (s * ntk + i, j))
    yre, yim = pl.pallas_call(
        _hy_fwd_kernel,
        grid=(nseq, ncb, ntk),
        in_specs=[fspec, fspec,
                  pl.BlockSpec((length, tn), lambda s, j, i: (data_rb0 + s, data_cb0 + j)),
                  kspec, kspec],
        out_specs=[yspec, yspec],
        out_shape=[jax.ShapeDtypeStruct((nseq * length, HY_WIDTH), BF16)] * 2,
        scratch_shapes=[pltpu.VMEM((length, tn), BF16)],
        compiler_params=_cparams(3, 2 * _nbytes((tk, length), BF16), _nbytes((length, tn), F32),
                                 2 * _nbytes((tk, tn), F32), 2 * _nbytes((tk, tn), BF16),
                                 extra=_nbytes((length, tn), BF16) + 6 * _nbytes((tk, tn), F32)),
        name="hy_fwd",
    )(mats["fc_hi"], mats["fs_hi"], data, kre, kim)

    yfull = pl.BlockSpec((length, tn), lambda s, j, i: (s, j))
    return pl.pallas_call(
        functools.partial(_hy_inv_kernel, inv_len=1.0 / length),
        grid=(nseq, ncb, ntk),
        in_specs=[fspec, fspec, yfull, yfull,
                  pl.BlockSpec((tk, tn), lambda s, j, i: ((data_rb0 + s) * ntk + i, data_cb0 + j)),
                  pl.BlockSpec((tk, tn), lambda s, j, i: ((gate_rb0 + s) * ntk + i, gate_col0 // tn + j)),
                  pl.BlockSpec((None, None, 1, tn), lambda s, j, i: (l, order, 0, j))],
        out_specs=pl.BlockSpec((tk, tn), lambda s, j, i: (s * ntk + i, j)),
        out_shape=jax.ShapeDtypeStruct((nseq * length, HY_WIDTH), out_dtype),
        compiler_params=_cparams(3, 2 * _nbytes((tk, length), BF16), 2 * _nbytes((length, tn), BF16),
                                 3 * _nbytes((tk, tn), F32), extra=4 * _nbytes((tk, tn), F32)),
        name="hy_inv",
    )(mats["fct_hi"], mats["fst_hi"], yre, yim, data, zc, bias)


def _hyena(cfg, consts, z, p, l):
    zc = _hy_dwconv(cfg, z, p["hy_conv_w"], p["hy_conv_b"], l)
    bias = p["hy_bias"].reshape(DEPTH, 2, 1, HY_WIDTH)
    w1p = jnp.pad(p["hy_w1"], ((0, 0), (0, LANES - HY_FEAT), (0, 0)))
    outs = []
    for length, nseq, row0 in ((cfg.lc, cfg.nc, 0), (cfg.ll, cfg.nl, cfg.rc)):
        mats = consts["dft"][length]
        hs, hd, hb0 = _hy_filter(length, consts["feats"][length], w1p, p["hy_b1"].reshape(DEPTH, 1, HY_HID),
                                 p["hy_w2"], p["hy_b2"].reshape(DEPTH, 1, HY_HID), p["hy_w3"],
                                 p["hy_freq"].reshape(DEPTH, 1, HY_HID),
                                 p["hy_decay"].reshape(DEPTH, 1, 4 * HY_WIDTH), l)
        kre, kim = _hy_spec(length, mats, hs, hd, hb0)
        rb0 = row0 // length
        y1 = _hy_conv(length, nseq, mats, kre, kim, 0, zc, rb0, 0, zc, rb0, HY_WIDTH, bias, l, F32)
        y2 = _hy_conv(length, nseq, mats, kre, kim, 1, y1, 0, 0, zc, rb0, 2 * HY_WIDTH, bias, l, BF16)
        outs.append(y2)
    return jnp.concatenate(outs, axis=0)


def _softmax_sink_pv(parts, sink):
    m = sink
    for s, _ in parts:
        m = jnp.maximum(m, jnp.max(s, axis=-1, keepdims=True))
    den = jnp.exp(sink - m)
    acc = None
    for s, v in parts:
        e = jnp.exp(s - m)
        den = den + jnp.sum(e, axis=-1, keepdims=True)
        pv = _dot(e.astype(BF16), v)
        acc = pv if acc is None else acc + pv
    return acc / den


def _attn_ctx_kernel(q_ref, k_ref, v_ref, sink_ref, o_ref):
    scale = HEAD_DIM ** -0.5
    for h in range(N_KV_HEADS):
        hs = slice(h * HEAD_DIM, (h + 1) * HEAD_DIM)
        k = k_ref[:, hs].astype(BF16)
        v = v_ref[:, hs].astype(BF16)
        for g in range(GQA_GROUP):
            hq = h * GQA_GROUP + g
            qs = slice(hq * HEAD_DIM, (hq + 1) * HEAD_DIM)
            s = _dot_nt(q_ref[:, qs].astype(BF16), k) * scale
            o_ref[:, qs] = _softmax_sink_pv([(s, v)], sink_ref[hq:hq + 1, 0:1]).astype(BF16)


def _attn_ctx(cfg, z, sink):
    nb = cfg.lc // BLOCK
    qw = N_Q_HEADS * HEAD_DIM
    return pl.pallas_call(
        _attn_ctx_kernel,
        grid=(cfg.nc, nb),
        in_specs=[pl.BlockSpec((BLOCK, qw), lambda s, i: (s * nb + i, Q_COL0 // qw)),
                  pl.BlockSpec((cfg.lc, KV_COLS), lambda s, i: (s, K_COL0 // KV_COLS)),
                  pl.BlockSpec((cfg.lc, KV_COLS), lambda s, i: (s, V_COL0 // KV_COLS)),
                  pl.BlockSpec((N_Q_HEADS, LANES), lambda s, i: (0, 0))],
        out_specs=pl.BlockSpec((BLOCK, qw), lambda s, i: (s * nb + i, 0)),
        out_shape=jax.ShapeDtypeStruct((cfg.rc, qw), BF16),
        compiler_params=_cparams(2, _nbytes((BLOCK, qw), F32), 2 * _nbytes((cfg.lc, KV_COLS), F32)),
        name="attn_ctx",
    )(z, z, z, sink)


def _rope(x, c, s):
    return x * c + pltpu.roll(x, HEAD_DIM // 2, 1) * s


def _attn_lat_kernel(q_ref, kp_ref, kc_ref, kn_ref, vp_ref, vc_ref, vn_ref, ck_ref, cv_ref,
                     cq, sq, cp, sp, cn, sn, sink_ref, o_ref, *, nb):
    i = pl.program_id(1)
    scale = HEAD_DIM ** -0.5
    shape = (BLOCK, 3 * BLOCK)
    qi = lax.broadcasted_iota(jnp.int32, shape, 0)
    kj = lax.broadcasted_iota(jnp.int32, shape, 1)
    ok = jnp.abs(kj - BLOCK - qi) <= WINDOW
    ok = ok & ((kj >= BLOCK) | (i > 0)) & ((kj < 2 * BLOCK) | (i < nb - 1))
    for h in range(N_KV_HEADS):
        hs = slice(h * HEAD_DIM, (h + 1) * HEAD_DIM)
        kw = jnp.concatenate([_rope(kp_ref[:, hs], cp[...], sp[...]),
                              _rope(kc_ref[:, hs], cq[...], sq[...]),
                              _rope(kn_ref[:, hs], cn[...], sn[...])], axis=0).astype(BF16)
        vw = jnp.concatenate([vp_ref[:, hs], vc_ref[:, hs], vn_ref[:, hs]], axis=0).astype(BF16)
        kctx = ck_ref[:, hs].astype(BF16)
        vctx = cv_ref[:, hs].astype(BF16)
        for g in range(GQA_GROUP):
            hq = h * GQA_GROUP + g
            qs = slice(hq * HEAD_DIM, (hq + 1) * HEAD_DIM)
            q = _rope(q_ref[:, qs], cq[...], sq[...]).astype(BF16)
            s1 = _dot_nt(q, kctx) * scale
            s2 = jnp.where(ok, _dot_nt(q, kw) * scale, NEG_INF)
            o_ref[:, qs] = _softmax_sink_pv([(s1, vctx), (s2, vw)], sink_ref[hq:hq + 1, 0:1]).astype(BF16)


def _attn_lat(cfg, consts, z, ck, cv, sink):
    nb = cfg.ll // BLOCK
    rb0 = cfg.rc // BLOCK
    qw = N_Q_HEADS * HEAD_DIM
    past = ck.shape[1]
    cos2, sin2 = consts["rope"]

    def blk(delta):
        return lambda b, i: rb0 + b * nb + jnp.clip(i + delta, 0, nb - 1)

    def zspec(width, col0, delta):
        rowf = blk(delta)
        return pl.BlockSpec((BLOCK, width), lambda b, i: (rowf(b, i), col0 // width))

    def tspec(delta):
        return pl.BlockSpec((BLOCK, HEAD_DIM), lambda b, i: (jnp.clip(i + delta, 0, nb - 1), 0))

    cspec = pl.BlockSpec((None, past, KV_COLS), lambda b, i: (b, 0, 0))
    return pl.pallas_call(
        functools.partial(_attn_lat_kernel, nb=nb),
        grid=(cfg.nl, nb),
        in_specs=[zspec(qw, Q_COL0, 0),
                  zspec(KV_COLS, K_COL0, -1), zspec(KV_COLS, K_COL0, 0), zspec(KV_COLS, K_COL0, 1),
                  zspec(KV_COLS, V_COL0, -1), zspec(KV_COLS, V_COL0, 0), zspec(KV_COLS, V_COL0, 1),
                  cspec, cspec,
                  tspec(0), tspec(0), tspec(-1), tspec(-1), tspec(1), tspec(1),
                  pl.BlockSpec((N_Q_HEADS, LANES), lambda b, i: (0, 0))],
        out_specs=pl.BlockSpec((BLOCK, qw), lambda b, i: (b * nb + i, 0)),
        out_shape=jax.ShapeDtypeStruct((cfg.rl, qw), BF16),
        compiler_params=_cparams(2, _nbytes((BLOCK, qw), F32), 6 * _nbytes((BLOCK, KV_COLS), F32),
                                 2 * _nbytes((past, KV_COLS), F32)),
        name="attn_lat",
    )(z, z, z, z, z, z, z, ck, cv, cos2, sin2, cos2, sin2, cos2, sin2, sink)


def _rope_tables(length):
    rows = length // GRID_W
    row = jnp.repeat(jnp.arange(rows, dtype=F32), GRID_W)
    col = jnp.tile(jnp.arange(GRID_W, dtype=F32), rows)
    n_freq = HEAD_DIM // 4
    inv = ROPE_BASE ** (-jnp.arange(n_freq, dtype=F32) / n_freq)
    ang = jnp.concatenate([row[:, None] * inv, col[:, None] * inv], axis=-1)
    cos, sin = jnp.cos(ang), jnp.sin(ang)
    return jnp.concatenate([cos, cos], axis=-1), jnp.concatenate([-sin, sin], axis=-1)


def _s5_matrices(p):
    hp = lax.Precision.HIGHEST
    t_len = S5_CHUNK
    lr, li = p["s5_lam_re"].astype(F32), p["s5_lam_im"].astype(F32)
    dt = jnp.exp(p["s5_log_dt"].astype(F32))[..., None]
    zr, zi = lr * dt, li * dt
    mag = jnp.exp(zr)
    ar, ai = mag * jnp.cos(zi), mag * jnp.sin(zi)
    den = lr * lr + li * li
    cr = ((ar - 1.0) * lr + ai * li) / den
    ci = (ai * lr - (ar - 1.0) * li) / den
    b_re, b_im = p["s5_b_re"].astype(F32), p["s5_b_im"].astype(F32)
    bbr = cr[..., None] * b_re - ci[..., None] * b_im
    bbi = cr[..., None] * b_im + ci[..., None] * b_re
    c_re, c_im = p["s5_c_re"].astype(F32), p["s5_c_im"].astype(F32)

    k = jnp.arange(t_len + 1, dtype=F32)[:, None, None, None, None]
    pm = jnp.exp(k * zr)
    pr, pi = pm * jnp.cos(k * zi), pm * jnp.sin(k * zi)
    abr = pr[..., None] * bbr - pi[..., None] * bbi
    abi = pr[..., None] * bbi + pi[..., None] * bbr
    w = (jnp.einsum("ldgcp,kldgpe->kldgce", c_re, abr[:t_len], precision=hp)
         - jnp.einsum("ldgcp,kldgpe->kldgce", c_im, abi[:t_len], precision=hp))
    s_idx = jnp.arange(t_len)[:, None]
    t_idx = jnp.arange(t_len)[None, :]
    lag_f = t_idx - s_idx
    wf = jnp.where((lag_f >= 0)[:, :, None, None, None, None], w[jnp.clip(lag_f, 0), :, 0], 0.0)
    wb = jnp.where((lag_f <= 0)[:, :, None, None, None, None], w[jnp.clip(-lag_f, 0), :, 1], 0.0)
    m = (wf + wb).transpose(2, 3, 0, 5, 1, 4)
    n_layers = m.shape[0]
    m_intra = m.reshape(n_layers, S5_GROUPS, S5_K, S5_K)

    def p_mat(pow_idx, d):
        re = abr[pow_idx, :, d].transpose(1, 2, 0, 4, 3)
        im = abi[pow_idx, :, d].transpose(1, 2, 0, 4, 3)
        return jnp.concatenate([re, im], axis=-1).reshape(n_layers, S5_GROUPS, S5_K, 2 * S5_STATE)

    p_in = jnp.stack([p_mat(t_len - 1 - jnp.arange(t_len), 0), p_mat(jnp.arange(t_len), 1)], axis=1)

    def q_mat(pow_idx, d):
        qr = c_re[:, d][None] * pr[pow_idx, :, d][:, :, :, None, :] - c_im[:, d][None] * pi[pow_idx, :, d][:, :, :, None, :]
        qi = c_re[:, d][None] * pi[pow_idx, :, d][:, :, :, None, :] + c_im[:, d][None] * pr[pow_idx, :, d][:, :, :, None, :]
        q = jnp.concatenate([qr, -qi], axis=-1)
        return q.transpose(1, 2, 4, 0, 3).reshape(n_layers, S5_GROUPS, 2 * S5_STATE, S5_K)

    q_out = jnp.stack([q_mat(1 + jnp.arange(t_len), 0), q_mat(t_len - jnp.arange(t_len), 1)], axis=1)
    a_chunk = jnp.stack([jnp.concatenate([pr[t_len], pr[t_len]], axis=-1),
                         jnp.concatenate([-pi[t_len], pi[t_len]], axis=-1)], axis=2)
    return m_intra, p_in, q_out, a_chunk


def _s5_in_kernel(u_ref, p_ref, x_ref):
    u = u_ref[...].astype(BF16)
    x_ref[0] = _dot(u, p_ref[0].astype(BF16))
    x_ref[1] = _dot(u, p_ref[1].astype(BF16))


def _s5_in(ug, p_in, l):
    g, r16, _ = ug.shape
    return pl.pallas_call(
        _s5_in_kernel,
        grid=(g,),
        in_specs=[pl.BlockSpec((None, r16, S5_K), lambda gi: (gi, 0, 0)),
                  pl.BlockSpec((None, 2, None, S5_K, 2 * S5_STATE), lambda gi: (l, 0, gi, 0, 0))],
        out_specs=pl.BlockSpec((2, None, r16, 2 * S5_STATE), lambda gi: (0, gi, 0, 0)),
        out_shape=jax.ShapeDtypeStruct((2, g, r16, 2 * S5_STATE), F32),
        compiler_params=_cparams(1, _nbytes((r16, S5_K), F32), 2 * _nbytes((r16, 2 * S5_STATE), F32)),
        name="s5_in",
    )(ug, p_in)


def _s5_scan_kernel(*refs, n_in, nsb, jb):
    a_ref, h0_ref = refs[0], refs[1]
    xf = refs[2:2 + n_in]
    xb = refs[2 + n_in:2 + 2 * n_in]
    sf = refs[2 + 2 * n_in:2 + 3 * n_in]
    sb = refs[2 + 3 * n_in:2 + 4 * n_in]
    fin_ref, st_ref = refs[2 + 4 * n_in], refs[3 + 4 * n_in]
    t = pl.program_id(0)

    @pl.when(t == 0)
    def _():
        st_ref[...] = h0_ref[...]

    half = S5_STATE

    def body(s, carry):
        jf = s
        jr = jb - 1 - s
        for r in range(n_in):
            for q in range(nsb):
                idx = r * nsb + q
                cur = st_ref[0, idx]
                sf[r][q * jb + jf] = cur
                st_ref[0, idx] = cur * a_ref[0, 0] + pltpu.roll(cur, half, 1) * a_ref[0, 1] + xf[r][q * jb + jf]
                cur = st_ref[1, idx]
                sb[r][q * jb + jr] = cur
                st_ref[1, idx] = cur * a_ref[1, 0] + pltpu.roll(cur, half, 1) * a_ref[1, 1] + xb[r][q * jb + jr]
        return carry

    lax.fori_loop(0, jb, body, 0)

    @pl.when(t == pl.num_programs(0) - 1)
    def _():
        fin_ref[...] = st_ref[...]


def _s5_scan(x, a_chunk, h0, l, row0, n_in, nsb, nj, jb):
    g = x.shape[2]
    w = x.shape[3]
    nblk = nj // jb
    nseq = n_in * nsb
    rows = nsb * jb
    base = row0 // rows

    def xspec(d, r):
        if d == 0:
            return pl.BlockSpec((None, rows, g, w), lambda t: (0, base + r * nblk + t, 0, 0))
        return pl.BlockSpec((None, rows, g, w), lambda t: (1, base + r * nblk + nblk - 1 - t, 0, 0))

    def sspec(d):
        if d == 0:
            return pl.BlockSpec((rows, g, w), lambda t: (t, 0, 0))
        return pl.BlockSpec((rows, g, w), lambda t: (nblk - 1 - t, 0, 0))

    st_spec = pl.BlockSpec((2, nseq, g, w), lambda t: (0, 0, 0, 0))
    res = pl.pallas_call(
        functools.partial(_s5_scan_kernel, n_in=n_in, nsb=nsb, jb=jb),
        grid=(nblk,),
        in_specs=[pl.BlockSpec((None, 2, 2, g, w), lambda t: (l, 0, 0, 0, 0)), st_spec]
                 + [xspec(0, r) for r in range(n_in)] + [xspec(1, r) for r in range(n_in)],
        out_specs=[sspec(0)] * n_in + [sspec(1)] * n_in + [st_spec],
        out_shape=[jax.ShapeDtypeStruct((nsb * nj, g, w), F32)] * (2 * n_in)
                  + [jax.ShapeDtypeStruct((2, nseq, g, w), F32)],
        scratch_shapes=[pltpu.VMEM((2, nseq, g, w), F32)],
        compiler_params=_cparams(1, 4 * n_in * _nbytes((rows, g, w), F32), 3 * _nbytes((2, nseq, g, w), F32)),
        name="s5_scan",
    )(a_chunk, h0, *([x] * (2 * n_in)))
    return res[:n_in], res[n_in:2 * n_in], res[2 * n_in]


def _s5_out_kernel(*refs, n_pieces):
    u_ref, m_ref, q_ref, d_ref = refs[:4]
    sf = refs[4:4 + n_pieces]
    sb = refs[4 + n_pieces:4 + 2 * n_pieces]
    y_ref = refs[4 + 2 * n_pieces]
    u = u_ref[...]
    s_f = jnp.concatenate([r[...] for r in sf], axis=0).astype(BF16)
    s_b = jnp.concatenate([r[...] for r in sb], axis=0).astype(BF16)
    y = _dot(u.astype(BF16), m_ref[...].astype(BF16))
    y += _dot(s_f, q_ref[0].astype(BF16))
    y += _dot(s_b, q_ref[1].astype(BF16))
    y_ref[...] = y + u * d_ref[...]


def _s5_out(ug, m_intra, q_out, d_tiled, sin_f, sin_b, l):
    g, r16, _ = ug.shape
    w = 2 * S5_STATE
    n_pieces = len(sin_f)
    piece_specs = [pl.BlockSpec((None, s.shape[1], w), lambda gi: (gi, 0, 0)) for s in sin_f]
    return pl.pallas_call(
        functools.partial(_s5_out_kernel, n_pieces=n_pieces),
        grid=(g,),
        in_specs=[pl.BlockSpec((None, r16, S5_K), lambda gi: (gi, 0, 0)),
                  pl.BlockSpec((None, None, S5_K, S5_K), lambda gi: (l, gi, 0, 0)),
                  pl.BlockSpec((None, 2, None, w, S5_K), lambda gi: (l, 0, gi, 0, 0)),
                  pl.BlockSpec((None, None, 1, S5_K), lambda gi: (l, gi, 0, 0))] + piece_specs * 2,
        out_specs=pl.BlockSpec((None, r16, S5_K), lambda gi: (gi, 0, 0)),
        out_shape=jax.ShapeDtypeStruct((g, r16, S5_K), F32),
        compiler_params=_cparams(1, 2 * _nbytes((r16, S5_K), F32), 2 * _nbytes((r16, w), F32),
                                 extra=4 * _nbytes((r16, S5_K), F32)),
        name="s5_out",
    )(ug, m_intra, q_out, d_tiled, *sin_f, *sin_b)


def _s5_glu_kernel(y_ref, w_ref, b_ref, o_ref, wb_ref):
    @pl.when(pl.program_id(0) == 0)
    def _():
        wb_ref[...] = w_ref[...].astype(BF16)

    y = _gelu(y_ref[...])
    o_ref[...] = (y * _sigmoid(_dot(y.astype(BF16), wb_ref[...]) + b_ref[...])).astype(BF16)


def _s5_glu(cfg, y, w, b, l):
    tm = min(512, cfg.ll)
    return pl.pallas_call(
        _s5_glu_kernel,
        grid=(cfg.r // tm,),
        in_specs=[pl.BlockSpec((tm, D_BRANCH), lambda i: (i, 0)),
                  pl.BlockSpec((None, D_BRANCH, D_BRANCH), lambda i: (l, 0, 0)),
                  pl.BlockSpec((None, 1, D_BRANCH), lambda i: (l, 0, 0))],
        out_specs=pl.BlockSpec((tm, D_BRANCH), lambda i: (i, 0)),
        out_shape=jax.ShapeDtypeStruct((cfg.r, D_BRANCH), BF16),
        scratch_shapes=[pltpu.VMEM((D_BRANCH, D_BRANCH), BF16)],
        compiler_params=_cparams(1, 2 * _nbytes((tm, D_BRANCH), F32), _nbytes((D_BRANCH, D_BRANCH), F32),
                                 extra=_nbytes((D_BRANCH, D_BRANCH), BF16) + 4 * _nbytes((tm, D_BRANCH), F32)),
        name="s5_glu",
    )(y, w, b.reshape(DEPTH, 1, D_BRANCH))


def _s5(cfg, s5m, z, p, h0_lat, l):
    m_intra, p_in, q_out, a_chunk = s5m
    r16 = cfg.r // S5_CHUNK
    ug = (z[:, S5_COL0:S5_COL0 + D_BRANCH].reshape(r16, S5_CHUNK, S5_GROUPS, S5_CH)
          .transpose(2, 0, 1, 3).reshape(S5_GROUPS, r16, S5_K))
    x = _s5_in(ug, p_in, l).transpose(0, 2, 1, 3)
    njc, njl = cfg.lc // S5_CHUNK, cfg.ll // S5_CHUNK
    rc16 = cfg.rc // S5_CHUNK
    zero_h0 = jnp.zeros((2, cfg.nc, S5_GROUPS, 2 * S5_STATE), F32)
    sf_c, sb_c, fin_ctx = _s5_scan(x, a_chunk, zero_h0, l, 0, 1, cfg.nc, njc, njc)
    sf_l, sb_l, _ = _s5_scan(x, a_chunk, h0_lat, l, rc16, cfg.nl, 1, njl, min(32, njl))
    d_tiled = jnp.tile(p["s5_d"].reshape(DEPTH, S5_GROUPS, 1, S5_CH), (1, 1, 1, S5_CHUNK))
    sin_f = [s.transpose(1, 0, 2) for s in list(sf_c) + list(sf_l)]
    sin_b = [s.transpose(1, 0, 2) for s in list(sb_c) + list(sb_l)]
    y = _s5_out(ug, m_intra, q_out, d_tiled, sin_f, sin_b, l)
    y = (y.reshape(S5_GROUPS, r16, S5_CHUNK, S5_CH).transpose(1, 2, 0, 3).reshape(cfg.r, D_BRANCH))
    return _s5_glu(cfg, y, p["s5_glu_w"], p["s5_glu_b"], l), fin_ctx


def _forward(cfg, x_prompt, x_sample, c, cache_k, cache_v, state_ssm_re, state_ssm_im, c_ctx, p):
    d = D_MODEL
    x = jnp.concatenate([x_prompt.reshape(cfg.rc, d), x_sample.reshape(cfg.rl, d)], axis=0)
    cvec = jnp.concatenate([c_ctx[None], c, jnp.zeros((MOD_ROWS - cfg.nseg, d), F32)], axis=0)
    mod = _mod(cvec, p["w_mod"], p["b_mod"])[:, :cfg.nseg].reshape(DEPTH, cfg.nseg, 6, 1, d)
    mod = [[mod[l, :, i] for i in range(6)] for l in range(DEPTH)]
    norm_g = p["norm_g"]

    consts = {
        "dft": {},
        "feats": {n: _hy_feats(n) for n in {cfg.lc, cfg.ll}},
        "rope": _rope_tables(cfg.ll),
    }
    for n in {cfg.lc, cfg.ll}:
        fc, fs, fct, fst = _dft_tables(n)
        fc_hi, fc_lo = _split(fc)
        fs_hi, fs_lo = _split(fs)
        consts["dft"][n] = {"fc_hi": fc_hi, "fc_lo": fc_lo, "fs_hi": fs_hi, "fs_lo": fs_lo,
                            "fct_hi": fct.astype(BF16), "fst_hi": fst.astype(BF16)}
    s5m = _s5_matrices(p)

    kv_shape = (cfg.nc, cfg.lc, N_KV_HEADS, HEAD_DIM)
    ks, vs, srs, sis = [], [], [], []
    h = _resid_norm(cfg, x, gpre=norm_g[0, 0], sc=mod[0][1], sh=mod[0][0])
    for l in range(DEPTH):
        z = _mm(h, p["w_in"], l, tm=cfg.tm, tn=512)
        ks.append(z[:cfg.rc, K_COL0:K_COL0 + KV_COLS].reshape(kv_shape))
        vs.append(z[:cfg.rc, V_COL0:V_COL0 + KV_COLS].reshape(kv_shape))

        y_hy = _hyena(cfg, consts, z, p, l)
        sink = jnp.broadcast_to(p["attn_sink"][l][:, None], (N_Q_HEADS, LANES))
        past = cache_k.shape[2]
        y_at = jnp.concatenate([
            _attn_ctx(cfg, z, sink),
            _attn_lat(cfg, consts, z, cache_k[:, l].reshape(cfg.nl, past, KV_COLS),
                      cache_v[:, l].reshape(cfg.nl, past, KV_COLS), sink)], axis=0)
        h0 = jnp.concatenate([state_ssm_re[:, l], state_ssm_im[:, l]], axis=-1).transpose(1, 0, 2, 3)
        y_s5, fin = _s5(cfg, s5m, z, p, h0, l)
        srs.append(fin[..., :S5_STATE].transpose(1, 0, 2, 3))
        sis.append(fin[..., S5_STATE:].transpose(1, 0, 2, 3))

        merged = _merge(cfg, z, (y_hy, y_at, y_s5),
                        (p["w_branch_hy"], p["w_branch_attn"], p["w_branch_s5"]), l)
        y = _mm(merged, p["w_out"], l, tm=cfg.tm, tn=512)
        x, h = _resid_norm(cfg, x, y, gate=mod[l][2], gpost=norm_g[l, 1],
                           gpre=norm_g[l, 2], sc=mod[l][4], sh=mod[l][3])
        u = _mm(h, p["ffn_w_up"], l, tm=cfg.tm, tn=512)
        act = _ffn_act(cfg, u, p["ffn_conv_w"], p["ffn_conv_b"], l)
        f = _mm(act, p["ffn_w_down"], l, tm=min(512, cfg.tm), tn=256)
        if l + 1 < DEPTH:
            x, h = _resid_norm(cfg, x, f, gate=mod[l][5], gpost=norm_g[l, 3],
                               gpre=norm_g[l + 1, 0], sc=mod[l + 1][1], sh=mod[l + 1][0])
        else:
            x = _resid_norm(cfg, x, f, gate=mod[l][5], gpost=norm_g[l, 3])

    return (x[:cfg.rc].reshape(cfg.nc, cfg.lc, d), x[cfg.rc:].reshape(cfg.nl, cfg.ll, d),
            jnp.stack(ks, axis=1), jnp.stack(vs, axis=1), jnp.stack(srs, axis=1), jnp.stack(sis, axis=1))


def kernel(x_prompt, x_sample, c, cache_k, cache_v, state_ssm_re, state_ssm_im, c_ctx, w_mod, b_mod, norm_g, w_in, hy_conv_w, hy_conv_b, hy_w1, hy_b1, hy_w2, hy_b2, hy_w3, hy_freq, hy_decay, hy_bias, attn_sink, s5_lam_re, s5_lam_im, s5_log_dt, s5_b_re, s5_b_im, s5_c_re, s5_c_im, s5_d, s5_glu_w, s5_glu_b, w_branch_hy, w_branch_attn, w_branch_s5, w_out, ffn_w_up, ffn_conv_w, ffn_conv_b, ffn_w_down):
    p = dict(w_mod=w_mod, b_mod=b_mod, norm_g=norm_g, w_in=w_in, hy_conv_w=hy_conv_w, hy_conv_b=hy_conv_b,
             hy_w1=hy_w1, hy_b1=hy_b1, hy_w2=hy_w2, hy_b2=hy_b2, hy_w3=hy_w3, hy_freq=hy_freq,
             hy_decay=hy_decay, hy_bias=hy_bias, attn_sink=attn_sink, s5_lam_re=s5_lam_re,
             s5_lam_im=s5_lam_im, s5_log_dt=s5_log_dt, s5_b_re=s5_b_re, s5_b_im=s5_b_im, s5_c_re=s5_c_re,
             s5_c_im=s5_c_im, s5_d=s5_d, s5_glu_w=s5_glu_w, s5_glu_b=s5_glu_b, w_branch_hy=w_branch_hy,
             w_branch_attn=w_branch_attn, w_branch_s5=w_branch_s5, w_out=w_out, ffn_w_up=ffn_w_up,
             ffn_conv_w=ffn_conv_w, ffn_conv_b=ffn_conv_b, ffn_w_down=ffn_w_down)
    cfg = _Cfg(x_prompt.shape[0], x_prompt.shape[1], x_sample.shape[0], x_sample.shape[1])
    return _forward(cfg, x_prompt, x_sample, c, cache_k, cache_v, state_ssm_re, state_ssm_im, c_ctx, p)
```

```python
import functools
import math

import numpy as np
import jax
import jax.numpy as jnp
from jax import lax
from jax.experimental import pallas as pl
from jax.experimental.pallas import tpu as pltpu

F32 = jnp.float32
BF16 = jnp.bfloat16

D_MODEL = 4096
DEPTH = 2
GRID_W = 64
D_BRANCH = D_MODEL // 4
HY_WIDTH = D_BRANCH
HY_BANDS = 16
HY_FEAT = 1 + 2 * HY_BANDS
HY_HID = 64
HEAD_DIM = 128
N_Q_HEADS = D_BRANCH // HEAD_DIM
N_KV_HEADS = 2
GQA_GROUP = N_Q_HEADS // N_KV_HEADS
WINDOW = 128
BLOCK = 128
ROPE_BASE = 10000.0
S5_CH = 16
S5_GROUPS = D_BRANCH // S5_CH
S5_STATE = 64
S5_CHUNK = 16
S5_OCTETS = 8
S5_OCT_K = S5_CHUNK * 128
D_FF = 2 * D_MODEL
EPS = 1e-6
NEG_INF = -1e30

HY_COLS = 3 * HY_WIDTH
Q_COL0 = HY_COLS
K_COL0 = Q_COL0 + N_Q_HEADS * HEAD_DIM
KV_COLS = N_KV_HEADS * HEAD_DIM
V_COL0 = K_COL0 + KV_COLS
S5_COL0 = V_COL0 + KV_COLS
GATE_COL0 = S5_COL0 + D_BRANCH
N_IN = GATE_COL0 + 3 * D_MODEL

V7X_VMEM_REQUEST_MAX = 60 * 1024 * 1024
LANES = 128
MOD_ROWS = 8


def _nbytes(shape, dtype):
    return math.prod(shape) * jnp.dtype(dtype).itemsize


def _cparams(n_grid, *block_bytes, extra=0):
    est = 2 * sum(block_bytes) + extra + (4 << 20)
    return pltpu.CompilerParams(
        dimension_semantics=("arbitrary",) * n_grid,
        vmem_limit_bytes=int(min(max(est, 16 << 20), V7X_VMEM_REQUEST_MAX)))


def _dot(a, b):
    return jnp.dot(a, b, preferred_element_type=F32)


def _dot_nt(a, b):
    return lax.dot_general(a, b, (((1,), (1,)), ((), ())), preferred_element_type=F32)


def _split(a):
    hi = a.astype(BF16)
    return hi, (a - hi.astype(F32)).astype(BF16)


def _dot3(a, b):
    ah, al = _split(a)
    bh, bl = _split(b)
    return _dot(ah, bh) + (_dot(ah, bl) + _dot(al, bh))


def _sigmoid(x):
    return 1.0 / (1.0 + jnp.exp(-x))


def _gelu(x):
    return 0.5 * x * (1.0 + jnp.tanh(math.sqrt(2.0 / math.pi) * (x + 0.044715 * (x * x * x))))


def _rms(x, g):
    return x * lax.rsqrt(jnp.mean(x * x, axis=-1, keepdims=True) + EPS) * g


class _Cfg:
    def __init__(self, nc, lc, nl, ll):
        self.nc, self.lc, self.nl, self.ll = nc, lc, nl, ll
        self.rc = nc * lc
        self.rl = nl * ll
        self.r = self.rc + self.rl
        self.nseg = 1 + nl
        assert self.rc % ll == 0 and ll % lc == 0 and lc % BLOCK == 0
        self.tm = min(1024, ll)
        self.rb = ll

    def seg_of_row(self, row0):
        return jnp.where(row0 >= self.rc, (row0 - self.rc) // self.ll + 1, 0)


def _mod_kernel(c_ref, w_ref, b_ref, o_ref):
    c = c_ref[...]
    s = (c * _sigmoid(c)).astype(BF16)
    o_ref[...] = _dot(s, w_ref[...].astype(BF16)) + b_ref[...]


def _mod(cvec, w_mod, b_mod):
    n = w_mod.shape[-1]
    tn = 512
    return pl.pallas_call(
        _mod_kernel,
        grid=(DEPTH, n // tn),
        in_specs=[pl.BlockSpec((MOD_ROWS, D_MODEL), lambda l, j: (0, 0)),
                  pl.BlockSpec((None, D_MODEL, tn), lambda l, j: (l, 0, j)),
                  pl.BlockSpec((None, 1, tn), lambda l, j: (l, 0, j))],
        out_specs=pl.BlockSpec((None, MOD_ROWS, tn), lambda l, j: (l, 0, j)),
        out_shape=jax.ShapeDtypeStruct((DEPTH, MOD_ROWS, n), F32),
        compiler_params=_cparams(2, _nbytes((D_MODEL, tn), F32), extra=_nbytes((D_MODEL, tn), BF16)),
        name="mod",
    )(cvec, w_mod, b_mod.reshape(DEPTH, 1, n))


def _resid_norm_kernel(*refs, has_y, has_h):
    it = iter(refs)
    x_ref = next(it)
    if has_y:
        y_ref, gate_ref, gpost_ref = next(it), next(it), next(it)
    if has_h:
        gpre_ref, sc_ref, sh_ref = next(it), next(it), next(it)
    x = x_ref[...]
    if has_y:
        x = x + gate_ref[...] * _rms(y_ref[...], gpost_ref[...])
        xo_ref = next(it)
        xo_ref[...] = x
    if has_h:
        ho_ref = next(it)
        ho_ref[...] = (_rms(x, gpre_ref[...]) * (1.0 + sc_ref[...]) + sh_ref[...]).astype(BF16)


def _resid_norm(cfg, x, y=None, gate=None, gpost=None, gpre=None, sc=None, sh=None):
    has_y, has_h = y is not None, gpre is not None
    tm = 256
    row = pl.BlockSpec((tm, D_MODEL), lambda i: (i, 0))
    vec = pl.BlockSpec((1, D_MODEL), lambda i: (0, 0))
    seg = pl.BlockSpec((None, 1, D_MODEL), lambda i: (cfg.seg_of_row(i * tm), 0, 0))
    args, specs, outs, ospecs = [x], [row], [], []
    if has_y:
        args += [y, gate, gpost.reshape(1, D_MODEL)]
        specs += [row, seg, vec]
        outs.append(jax.ShapeDtypeStruct((cfg.r, D_MODEL), F32))
        ospecs.append(row)
    if has_h:
        args += [gpre.reshape(1, D_MODEL), sc, sh]
        specs += [vec, seg, seg]
        outs.append(jax.ShapeDtypeStruct((cfg.r, D_MODEL), BF16))
        ospecs.append(row)
    res = pl.pallas_call(
        functools.partial(_resid_norm_kernel, has_y=has_y, has_h=has_h),
        grid=(cfg.r // tm,),
        in_specs=specs, out_specs=ospecs, out_shape=outs,
        compiler_params=_cparams(1, 4 * _nbytes((tm, D_MODEL), F32)),
        name="resid_norm",
    )(*args)
    return res if len(res) > 1 else res[0]


def _mm_kernel(x_ref, w_ref, o_ref, wb_ref):
    @pl.when(pl.program_id(1) == 0)
    def _():
        wb_ref[...] = w_ref[...].astype(BF16)

    o_ref[...] = _dot(x_ref[...], wb_ref[...]).astype(o_ref.dtype)


def _mm(x, w, l, *, tm, tn, out_dtype=F32, w_buffers=2):
    m, k = x.shape
    n = w.shape[-1]
    w_mode = {} if w_buffers == 2 else {"pipeline_mode": pl.Buffered(w_buffers)}
    return pl.pallas_call(
        _mm_kernel,
        grid=(n // tn, m // tm),
        in_specs=[pl.BlockSpec((tm, k), lambda j, i: (i, 0)),
                  pl.BlockSpec((None, k, tn), lambda j, i: (l, 0, j), **w_mode)],
        out_specs=pl.BlockSpec((tm, tn), lambda j, i: (i, j)),
        out_shape=jax.ShapeDtypeStruct((m, n), out_dtype),
        scratch_shapes=[pltpu.VMEM((k, tn), BF16)],
        compiler_params=_cparams(2, _nbytes((tm, k), BF16), _nbytes((tm, tn), out_dtype),
                                 extra=w_buffers * _nbytes((k, tn), F32) + _nbytes((k, tn), BF16)
                                 + _nbytes((tm, tn), F32)),
        name="mm",
    )(x, w)


def _merge_kernel(g0, g1, g2, y0, y1, y2, w0, w1, w2, o_ref, wb_ref):
    @pl.when(pl.program_id(1) == 0)
    def _():
        wb_ref[0] = w0[...].astype(BF16)
        wb_ref[1] = w1[...].astype(BF16)
        wb_ref[2] = w2[...].astype(BF16)

    acc = _sigmoid(g0[...]) * _dot(y0[...], wb_ref[0])
    acc += _sigmoid(g1[...]) * _dot(y1[...], wb_ref[1])
    acc += _sigmoid(g2[...]) * _dot(y2[...], wb_ref[2])
    o_ref[...] = acc.astype(o_ref.dtype)


def _merge(cfg, z, ys, ws, l):
    tm, tn = cfg.tm, 512
    gate_specs = [pl.BlockSpec((tm, tn), functools.partial(
        lambda j, i, b: (i, (GATE_COL0 + b * D_MODEL) // tn + j), b=b)) for b in range(3)]
    y_spec = pl.BlockSpec((tm, D_BRANCH), lambda j, i: (i, 0))
    w_spec = pl.BlockSpec((None, D_BRANCH, tn), lambda j, i: (l, 0, j))
    return pl.pallas_call(
        _merge_kernel,
        grid=(D_MODEL // tn, cfg.r // tm),
        in_specs=gate_specs + [y_spec] * 3 + [w_spec] * 3,
        out_specs=pl.BlockSpec((tm, tn), lambda j, i: (i, j)),
        out_shape=jax.ShapeDtypeStruct((cfg.r, D_MODEL), BF16),
        scratch_shapes=[pltpu.VMEM((3, D_BRANCH, tn), BF16)],
        compiler_params=_cparams(2, 3 * _nbytes((tm, tn), F32), 3 * _nbytes((tm, D_BRANCH), BF16),
                                 3 * _nbytes((D_BRANCH, tn), F32), _nbytes((tm, tn), BF16),
                                 extra=3 * _nbytes((D_BRANCH, tn), BF16) + 2 * _nbytes((tm, tn), F32)),
        name="merge",
    )(z, z, z, *ys, *ws)


def _seq_pos(cfg, shape):
    sl = jnp.where(pl.program_id(0) < cfg.rc // cfg.rb, cfg.lc, cfg.ll)
    return lax.broadcasted_iota(jnp.int32, shape, 0) & (sl - 1), sl


def _dwconv3(x, w_ref, b_ref, t, sl):
    rows = x.shape[0]
    xm = jnp.where(t == 0, 0.0, pltpu.roll(x, 1, 0))
    xp = jnp.where(t == sl - 1, 0.0, pltpu.roll(x, rows - 1, 0))
    return xm * w_ref[0:1, :] + x * w_ref[1:2, :] + xp * w_ref[2:3, :] + b_ref[...]


def _hy_dwconv_kernel(x_ref, w_ref, b_ref, o_ref, *, cfg):
    x = x_ref[...]
    t, sl = _seq_pos(cfg, x.shape)
    o_ref[...] = _dwconv3(x, w_ref, b_ref, t, sl)


def _hy_dwconv(cfg, z, w, b, l):
    tn = 256
    blk = _nbytes((cfg.rb, tn), F32)
    return pl.pallas_call(
        functools.partial(_hy_dwconv_kernel, cfg=cfg),
        grid=(cfg.r // cfg.rb, HY_COLS // tn),
        in_specs=[pl.BlockSpec((cfg.rb, tn), lambda r, j: (r, j)),
                  pl.BlockSpec((None, 3, tn), lambda r, j: (l, 0, j)),
                  pl.BlockSpec((None, 1, tn), lambda r, j: (l, 0, j))],
        out_specs=pl.BlockSpec((cfg.rb, tn), lambda r, j: (r, j)),
        out_shape=jax.ShapeDtypeStruct((cfg.r, HY_COLS), F32),
        compiler_params=_cparams(2, 2 * blk, extra=4 * blk),
        name="hy_dwconv",
    )(z, w, b.reshape(DEPTH, 1, HY_COLS))


def _ffn_up_kernel(x_ref, wa_ref, wb_ref, cwa_ref, cwb_ref, cba_ref, cbb_ref, o_ref, w_scr, ab_scr, last_scr,
                   *, cfg, tm, nt):
    i = pl.program_id(1)
    slot = lax.rem(i, 2)

    @pl.when(i == 0)
    def _():
        w_scr[0] = wa_ref[...].astype(BF16)
        w_scr[1] = wb_ref[...].astype(BF16)
        last_scr[...] = jnp.zeros_like(last_scr)

    @pl.when(i < nt)
    def _():
        x = x_ref[...]
        ab_scr[slot, 0] = _dot(x, w_scr[0])
        ab_scr[slot, 1] = _dot(x, w_scr[1])

    @pl.when(i > 0)
    def _():
        prev = 1 - slot
        row0 = (i - 1) * tm
        sl = jnp.where(row0 < cfg.rc, cfg.lc, cfg.ll)
        shape = (tm, o_ref.shape[1])
        r = lax.broadcasted_iota(jnp.int32, shape, 0)
        t = (row0 + r) & (sl - 1)
        conv = []
        for h, (cw, cb) in enumerate(((cwa_ref, cba_ref), (cwb_ref, cbb_ref))):
            cur = ab_scr[prev, h]
            below = last_scr[h, 7:8, :]
            above = ab_scr[slot, h, 0:1, :]
            xm = jnp.where(r == 0, below, pltpu.roll(cur, 1, 0))
            xm = jnp.where(t == 0, 0.0, xm)
            xp = jnp.where(r == tm - 1, above, pltpu.roll(cur, tm - 1, 0))
            xp = jnp.where(t == sl - 1, 0.0, xp)
            conv.append(xm * cw[0:1, :] + cur * cw[1:2, :] + xp * cw[2:3, :] + cb[...])
            last_scr[h] = cur[tm - 8:tm, :]
        o_ref[...] = (_gelu(conv[0]) * conv[1]).astype(BF16)


def _ffn_up(cfg, x, w, cw, cb, l):
    tm, tn = cfg.tm, 256
    nt = cfg.r // tm
    nb = D_FF // tn
    k = x.shape[1]
    cb3 = cb.reshape(DEPTH, 1, 2 * D_FF)
    return pl.pallas_call(
        functools.partial(_ffn_up_kernel, cfg=cfg, tm=tm, nt=nt),
        grid=(nb, nt + 1),
        in_specs=[pl.BlockSpec((tm, k), lambda j, i: (jnp.minimum(i, nt - 1), 0)),
                  pl.BlockSpec((None, k, tn), lambda j, i: (l, 0, j)),
                  pl.BlockSpec((None, k, tn), lambda j, i: (l, 0, j + nb)),
                  pl.BlockSpec((None, 3, tn), lambda j, i: (l, 0, j)),
                  pl.BlockSpec((None, 3, tn), lambda j, i: (l, 0, j + nb)),
                  pl.BlockSpec((None, 1, tn), lambda j, i: (l, 0, j)),
                  pl.BlockSpec((None, 1, tn), lambda j, i: (l, 0, j + nb))],
        out_specs=pl.BlockSpec((tm, tn), lambda j, i: (jnp.maximum(i - 1, 0), j)),
        out_shape=jax.ShapeDtypeStruct((cfg.r, D_FF), BF16),
        scratch_shapes=[pltpu.VMEM((2, k, tn), BF16), pltpu.VMEM((2, 2, tm, tn), F32),
                        pltpu.VMEM((2, 8, tn), F32)],
        compiler_params=_cparams(2, _nbytes((tm, k), BF16), 2 * _nbytes((k, tn), F32), _nbytes((tm, tn), BF16),
                                 extra=_nbytes((2, k, tn), BF16) + 12 * _nbytes((tm, tn), F32)),
        name="ffn_up",
    )(x, w, w, cw, cw, cb3, cb3)


def _dft_tables(length):
    period = 4 * length
    r = 1 << ((length.bit_length()) // 2)
    idx = np.arange(length, dtype=np.int64)

    def cs(m):
        ang = 2.0 * np.pi * (m % period).astype(np.float64) / period
        return jnp.asarray(np.cos(ang), F32), jnp.asarray(np.sin(ang), F32)

    def combine(ca, sa, cb, sb):
        c = ca[:, :, None] * cb[:, None, :] - sa[:, :, None] * sb[:, None, :]
        s = sa[:, :, None] * cb[:, None, :] + ca[:, :, None] * sb[:, None, :]
        return c.reshape(length, length), s.reshape(length, length)

    odd = 2 * idx + 1
    ca, sa = cs(odd[:, None] * (np.arange(length // r) * r)[None, :])
    cb, sb = cs(odd[:, None] * np.arange(r)[None, :])
    fc, fs = combine(ca, sa, cb, sb)
    ca, sa = cs(idx[:, None] * (2 * r * np.arange(length // r))[None, :])
    cb, sb = cs(idx[:, None] * (2 * np.arange(r) + 1)[None, :])
    fct, fst = combine(ca, sa, cb, sb)
    return fc, fs, fct, fst


def _hy_feats(length):
    pos = jnp.arange(length, dtype=F32)
    t = (pos / length)[:, None]
    bands = jnp.linspace(1e-4, HY_BANDS - 1, HY_BANDS, dtype=F32)
    wpos = 2.0 * math.pi * t * bands
    feats = jnp.concatenate([t, jnp.cos(wpos), -jnp.sin(wpos)], axis=-1)
    return jnp.pad(feats, ((0, 0), (0, LANES - HY_FEAT)))


def _hy_filter_kernel(f_ref, w1_ref, b1_ref, w2_ref, b2_ref, w3_ref, fr_ref, dec_ref, hs_ref, hd_ref, hb0_ref):
    feats = f_ref[...]
    fr = fr_ref[...]
    h = jnp.sin(fr * (_dot3(feats, w1_ref[...]) + b1_ref[...]))
    h = jnp.sin(fr * (_dot3(h, w2_ref[...]) + b2_ref[...]))
    h = _dot3(h, w3_ref[...]) * jnp.exp(-feats[:, 0:1] * jnp.abs(dec_ref[...]))
    half = 2 * HY_WIDTH
    hf, hb = h[:, :half], h[:, half:]
    hs_ref[...] = (hf + hb).astype(BF16)
    hd_ref[...] = (hb - hf).astype(BF16)

    @pl.when(pl.program_id(0) == 0)
    def _():
        hb0_ref[...] = jnp.broadcast_to(hb[0:1, :], hb0_ref.shape)


def _hy_filter(length, feats, w1p, b1, w2, b2, w3, freq, decay, l):
    tl = 256
    half = 2 * HY_WIDTH
    lsel = lambda i: (l, 0, 0)
    return pl.pallas_call(
        _hy_filter_kernel,
        grid=(length // tl,),
        in_specs=[pl.BlockSpec((tl, LANES), lambda i: (i, 0)),
                  pl.BlockSpec((None, LANES, HY_HID), lsel),
                  pl.BlockSpec((None, 1, HY_HID), lsel),
                  pl.BlockSpec((None, HY_HID, HY_HID), lsel),
                  pl.BlockSpec((None, 1, HY_HID), lsel),
                  pl.BlockSpec((None, HY_HID, 2 * half), lsel),
                  pl.BlockSpec((None, 1, HY_HID), lsel),
                  pl.BlockSpec((None, 1, 2 * half), lsel)],
        out_specs=[pl.BlockSpec((tl, half), lambda i: (i, 0)),
                   pl.BlockSpec((tl, half), lambda i: (i, 0)),
                   pl.BlockSpec((8, half), lambda i: (0, 0))],
        out_shape=[jax.ShapeDtypeStruct((length, half), BF16),
                   jax.ShapeDtypeStruct((length, half), BF16),
                   jax.ShapeDtypeStruct((8, half), F32)],
        compiler_params=_cparams(1, 2 * _nbytes((tl, half), F32), extra=6 * _nbytes((tl, 2 * half), F32)),
        name="hy_filter",
    )(feats, w1p, b1, w2, b2, w3, freq, decay)


def _hy_spec_kernel(fc, fs, hs_ref, hd_ref, hb0_ref, kre_ref, kim_ref):
    kre_ref[...] = _dot(fc[...], hs_ref[...]) - hb0_ref[0:1, :]
    kim_ref[...] = _dot(fs[...], hd_ref[...])


def _hy_spec(length, mats, hs, hd, hb0):
    tk = min(512, length)
    tn = 512
    half = 2 * HY_WIDTH
    fspec = pl.BlockSpec((tk, length), lambda j, i: (i, 0))
    hspec = pl.BlockSpec((length, tn), lambda j, i: (0, j))
    kspec = pl.BlockSpec((tk, tn), lambda j, i: (i, j))
    return pl.pallas_call(
        _hy_spec_kernel,
        grid=(half // tn, length // tk),
        in_specs=[fspec, fspec, hspec, hspec, pl.BlockSpec((8, tn), lambda j, i: (0, j))],
        out_specs=[kspec, kspec],
        out_shape=[jax.ShapeDtypeStruct((length, half), F32)] * 2,
        compiler_params=_cparams(2, 2 * _nbytes((tk, length), BF16), 2 * _nbytes((length, tn), BF16),
                                 2 * _nbytes((tk, tn), F32), extra=2 * _nbytes((tk, tn), F32)),
        name="hy_spec",
    )(mats["fc_hi"], mats["fs_hi"], hs, hd, hb0)


def _hy_fwd_kernel(fc, fs, u_ref, kre, kim, yre_ref, yim_ref, ub_ref):
    @pl.when(pl.program_id(2) == 0)
    def _():
        ub_ref[...] = u_ref[...].astype(BF16)

    ure = _dot(fc[...], ub_ref[...])
    uim = -_dot(fs[...], ub_ref[...])
    yre_ref[...] = (ure * kre[...] - uim * kim[...]).astype(BF16)
    yim_ref[...] = (ure * kim[...] + uim * kre[...]).astype(BF16)


def _hy_inv_kernel(fct, fst, yre, yim, u_ref, gate_ref, bias_ref, o_ref, *, inv_len):
    acc = _dot(fct[...], yre[...]) - _dot(fst[...], yim[...])
    o_ref[...] = (gate_ref[...] * (acc * inv_len + u_ref[...] * bias_ref[...])).astype(o_ref.dtype)


def _hy_conv(length, nseq, mats, kre, kim, order, data, data_rb0, data_cb0, zc, gate_rb0, gate_col0, bias, l,
             out_dtype):
    tn = min(512, HY_WIDTH)
    tk = min(512, length)
    ncb = HY_WIDTH // tn
    ntk = length // tk
    fspec = pl.BlockSpec((tk, length), lambda s, j, i: (i, 0))
    kspec = pl.BlockSpec((tk, tn), lambda s, j, i: (i, order * ncb + j))
    yspec = pl.BlockSpec((tk, tn), lambda s, j, i: (s * ntk + i, j))
    yre, yim = pl.pallas_call(
        _hy_fwd_kernel,
        grid=(nseq, ncb, ntk),
        in_specs=[fspec, fspec,
                  pl.BlockSpec((length, tn), lambda s, j, i: (data_rb0 + s, data_cb0 + j)),
                  kspec, kspec],
        out_specs=[yspec, yspec],
        out_shape=[jax.ShapeDtypeStruct((nseq * length, HY_WIDTH), BF16)] * 2,
        scratch_shapes=[pltpu.VMEM((length, tn), BF16)],
        compiler_params=_cparams(3, 2 * _nbytes((tk, length), BF16), _nbytes((length, tn), F32),
                                 2 * _nbytes((tk, tn), F32), 2 * _nbytes((tk, tn), BF16),
                                 extra=_nbytes((length, tn), BF16) + 6 * _nbytes((tk, tn), F32)),
        name="hy_fwd",
    )(mats["fc_hi"], mats["fs_hi"], data, kre, kim)

    yfull = pl.BlockSpec((length, tn), lambda s, j, i: (s, j))
    return pl.pallas_call(
        functools.partial(_hy_inv_kernel, inv_len=1.0 / length),
        grid=(nseq, ncb, ntk),
        in_specs=[fspec, fspec, yfull, yfull,
                  pl.BlockSpec((tk, tn), lambda s, j, i: ((data_rb0 + s) * ntk + i, data_cb0 + j)),
                  pl.BlockSpec((tk, tn), lambda s, j, i: ((gate_rb0 + s) * ntk + i, gate_col0 // tn + j)),
                  pl.BlockSpec((None, None, 1, tn), lambda s, j, i: (l, order, 0, j))],
        out_specs=pl.BlockSpec((tk, tn), lambda s, j, i: (s * ntk + i, j)),
        out_shape=jax.ShapeDtypeStruct((nseq * length, HY_WIDTH), out_dtype),
        compiler_params=_cparams(3, 2 * _nbytes((tk, length), BF16), 2 * _nbytes((length, tn), BF16),
                                 3 * _nbytes((tk, tn), F32), extra=4 * _nbytes((tk, tn), F32)),
        name="hy_inv",
    )(mats["fct_hi"], mats["fst_hi"], yre, yim, data, zc, bias)


def _hyena(cfg, consts, z, p, l):
    zc = _hy_dwconv(cfg, z, p["hy_conv_w"], p["hy_conv_b"], l)
    bias = p["hy_bias"].reshape(DEPTH, 2, 1, HY_WIDTH)
    w1p = jnp.pad(p["hy_w1"], ((0, 0), (0, LANES - HY_FEAT), (0, 0)))
    outs = []
    for length, nseq, row0 in ((cfg.lc, cfg.nc, 0), (cfg.ll, cfg.nl, cfg.rc)):
        mats = consts["dft"][length]
        hs, hd, hb0 = _hy_filter(length, consts["feats"][length], w1p, p["hy_b1"].reshape(DEPTH, 1, HY_HID),
                                 p["hy_w2"], p["hy_b2"].reshape(DEPTH, 1, HY_HID), p["hy_w3"],
                                 p["hy_freq"].reshape(DEPTH, 1, HY_HID),
                                 p["hy_decay"].reshape(DEPTH, 1, 4 * HY_WIDTH), l)
        kre, kim = _hy_spec(length, mats, hs, hd, hb0)
        rb0 = row0 // length
        y1 = _hy_conv(length, nseq, mats, kre, kim, 0, zc, rb0, 0, zc, rb0, HY_WIDTH, bias, l, F32)
        y2 = _hy_conv(length, nseq, mats, kre, kim, 1, y1, 0, 0, zc, rb0, 2 * HY_WIDTH, bias, l, BF16)
        outs.append(y2)
    return jnp.concatenate(outs, axis=0)


def _softmax_sink_pv(parts, sink):
    m = sink
    for s, _ in parts:
        m = jnp.maximum(m, jnp.max(s, axis=-1, keepdims=True))
    den = jnp.exp(sink - m)
    acc = None
    for s, v in parts:
        e = jnp.exp(s - m)
        den = den + jnp.sum(e, axis=-1, keepdims=True)
        pv = _dot(e.astype(BF16), v)
        acc = pv if acc is None else acc + pv
    return acc / den


def _group_sink(sink_ref, h):
    return jnp.concatenate([jnp.broadcast_to(sink_ref[hq:hq + 1, 0:1], (BLOCK, 1))
                            for hq in range(h * GQA_GROUP, (h + 1) * GQA_GROUP)], axis=0)


def _store_group(o_ref, h, o):
    for g in range(GQA_GROUP):
        hq = h * GQA_GROUP + g
        o_ref[:, hq * HEAD_DIM:(hq + 1) * HEAD_DIM] = o[g * BLOCK:(g + 1) * BLOCK].astype(BF16)


def _attn_ctx_kernel(q_ref, k_ref, v_ref, sink_ref, o_ref):
    scale = HEAD_DIM ** -0.5
    for h in range(N_KV_HEADS):
        hs = slice(h * HEAD_DIM, (h + 1) * HEAD_DIM)
        k = k_ref[:, hs].astype(BF16)
        v = v_ref[:, hs].astype(BF16)
        q = jnp.concatenate([q_ref[:, hq * HEAD_DIM:(hq + 1) * HEAD_DIM]
                             for hq in range(h * GQA_GROUP, (h + 1) * GQA_GROUP)], axis=0).astype(BF16)
        s = _dot_nt(q, k) * scale
        _store_group(o_ref, h, _softmax_sink_pv([(s, v)], _group_sink(sink_ref, h)))


def _attn_ctx(cfg, z, sink):
    nb = cfg.lc // BLOCK
    qw = N_Q_HEADS * HEAD_DIM
    return pl.pallas_call(
        _attn_ctx_kernel,
        grid=(cfg.nc, nb),
        in_specs=[pl.BlockSpec((BLOCK, qw), lambda s, i: (s * nb + i, Q_COL0 // qw)),
                  pl.BlockSpec((cfg.lc, KV_COLS), lambda s, i: (s, K_COL0 // KV_COLS)),
                  pl.BlockSpec((cfg.lc, KV_COLS), lambda s, i: (s, V_COL0 // KV_COLS)),
                  pl.BlockSpec((N_Q_HEADS, LANES), lambda s, i: (0, 0))],
        out_specs=pl.BlockSpec((BLOCK, qw), lambda s, i: (s * nb + i, 0)),
        out_shape=jax.ShapeDtypeStruct((cfg.rc, qw), BF16),
        compiler_params=_cparams(2, _nbytes((BLOCK, qw), F32), 2 * _nbytes((cfg.lc, KV_COLS), F32)),
        name="attn_ctx",
    )(z, z, z, sink)


def _rope(x, c, s):
    return x * c + pltpu.roll(x, HEAD_DIM // 2, 1) * s


def _attn_lat_kernel(q_ref, kp_ref, kc_ref, kn_ref, vp_ref, vc_ref, vn_ref, ck_ref, cv_ref,
                     cq, sq, cp, sp, cn, sn, sink_ref, o_ref, *, nb):
    i = pl.program_id(1)
    scale = HEAD_DIM ** -0.5
    shape = (GQA_GROUP * BLOCK, 3 * BLOCK)
    qi = lax.broadcasted_iota(jnp.int32, shape, 0) & (BLOCK - 1)
    kj = lax.broadcasted_iota(jnp.int32, shape, 1)
    ok = jnp.abs(kj - BLOCK - qi) <= WINDOW
    ok = ok & ((kj >= BLOCK) | (i > 0)) & ((kj < 2 * BLOCK) | (i < nb - 1))
    for h in range(N_KV_HEADS):
        hs = slice(h * HEAD_DIM, (h + 1) * HEAD_DIM)
        kw = jnp.concatenate([_rope(kp_ref[:, hs], cp[...], sp[...]),
                              _rope(kc_ref[:, hs], cq[...], sq[...]),
                              _rope(kn_ref[:, hs], cn[...], sn[...])], axis=0).astype(BF16)
        vw = jnp.concatenate([vp_ref[:, hs], vc_ref[:, hs], vn_ref[:, hs]], axis=0).astype(BF16)
        kctx = ck_ref[:, hs].astype(BF16)
        vctx = cv_ref[:, hs].astype(BF16)
        q = jnp.concatenate([_rope(q_ref[:, hq * HEAD_DIM:(hq + 1) * HEAD_DIM], cq[...], sq[...])
                             for hq in range(h * GQA_GROUP, (h + 1) * GQA_GROUP)], axis=0).astype(BF16)
        s1 = _dot_nt(q, kctx) * scale
        s2 = jnp.where(ok, _dot_nt(q, kw) * scale, NEG_INF)
        _store_group(o_ref, h, _softmax_sink_pv([(s1, vctx), (s2, vw)], _group_sink(sink_ref, h)))


def _attn_lat(cfg, consts, z, ck, cv, sink):
    nb = cfg.ll // BLOCK
    rb0 = cfg.rc // BLOCK
    qw = N_Q_HEADS * HEAD_DIM
    past = ck.shape[1]
    cos2, sin2 = consts["rope"]

    def blk(delta):
        return lambda b, i: rb0 + b * nb + jnp.clip(i + delta, 0, nb - 1)

    def zspec(width, col0, delta):
        rowf = blk(delta)
        return pl.BlockSpec((BLOCK, width), lambda b, i: (rowf(b, i), col0 // width))

    def tspec(delta):
        return pl.BlockSpec((BLOCK, HEAD_DIM), lambda b, i: (jnp.clip(i + delta, 0, nb - 1), 0))

    cspec = pl.BlockSpec((None, past, KV_COLS), lambda b, i: (b, 0, 0))
    return pl.pallas_call(
        functools.partial(_attn_lat_kernel, nb=nb),
        grid=(cfg.nl, nb),
        in_specs=[zspec(qw, Q_COL0, 0),
                  zspec(KV_COLS, K_COL0, -1), zspec(KV_COLS, K_COL0, 0), zspec(KV_COLS, K_COL0, 1),
                  zspec(KV_COLS, V_COL0, -1), zspec(KV_COLS, V_COL0, 0), zspec(KV_COLS, V_COL0, 1),
                  cspec, cspec,
                  tspec(0), tspec(0), tspec(-1), tspec(-1), tspec(1), tspec(1),
                  pl.BlockSpec((N_Q_HEADS, LANES), lambda b, i: (0, 0))],
        out_specs=pl.BlockSpec((BLOCK, qw), lambda b, i: (b * nb + i, 0)),
        out_shape=jax.ShapeDtypeStruct((cfg.rl, qw), BF16),
        compiler_params=_cparams(2, _nbytes((BLOCK, qw), F32), 6 * _nbytes((BLOCK, KV_COLS), F32),
                                 2 * _nbytes((past, KV_COLS), F32)),
        name="attn_lat",
    )(z, z, z, z, z, z, z, ck, cv, cos2, sin2, cos2, sin2, cos2, sin2, sink)


def _rope_tables(length):
    rows = length // GRID_W
    row = jnp.repeat(jnp.arange(rows, dtype=F32), GRID_W)
    col = jnp.tile(jnp.arange(GRID_W, dtype=F32), rows)
    n_freq = HEAD_DIM // 4
    inv = ROPE_BASE ** (-jnp.arange(n_freq, dtype=F32) / n_freq)
    ang = jnp.concatenate([row[:, None] * inv, col[:, None] * inv], axis=-1)
    cos, sin = jnp.cos(ang), jnp.sin(ang)
    return jnp.concatenate([cos, cos], axis=-1), jnp.concatenate([-sin, sin], axis=-1)


def _s5_matrices(p):
    t_len = S5_CHUNK
    lr, li = p["s5_lam_re"].astype(F32), p["s5_lam_im"].astype(F32)
    dt = jnp.exp(p["s5_log_dt"].astype(F32))[..., None]
    zr, zi = lr * dt, li * dt
    mag = jnp.exp(zr)
    ar, ai = mag * jnp.cos(zi), mag * jnp.sin(zi)
    den = lr * lr + li * li
    cr = ((ar - 1.0) * lr + ai * li) / den
    ci = (ai * lr - (ar - 1.0) * li) / den
    b_re, b_im = p["s5_b_re"].astype(F32), p["s5_b_im"].astype(F32)
    bbr = cr[..., None] * b_re - ci[..., None] * b_im
    bbi = cr[..., None] * b_im + ci[..., None] * b_re
    c_re, c_im = p["s5_c_re"].astype(F32), p["s5_c_im"].astype(F32)

    k = jnp.arange(t_len + 1, dtype=F32)[:, None, None, None, None]
    pm = jnp.exp(k * zr)
    pr, pi = pm * jnp.cos(k * zi), pm * jnp.sin(k * zi)
    abr = pr[..., None] * bbr - pi[..., None] * bbi
    abi = pr[..., None] * bbi + pi[..., None] * bbr
    abr_t = abr[:t_len].transpose(0, 1, 2, 3, 5, 4)[:, :, :, :, None]
    abi_t = abi[:t_len].transpose(0, 1, 2, 3, 5, 4)[:, :, :, :, None]
    w = jnp.sum(c_re[None, :, :, :, :, None, :] * abr_t - c_im[None, :, :, :, :, None, :] * abi_t,
                axis=-1)
    n_layers = w.shape[1]
    oct_, gpo = S5_OCTETS, S5_GROUPS // S5_OCTETS
    eye = jnp.eye(gpo, dtype=F32)
    w_lag = jnp.concatenate([jnp.flip(w[1:, :, 1], 0), (w[0, :, 0] + w[0, :, 1])[None], w[1:, :, 0]], axis=0)
    w_lag = w_lag.reshape(2 * t_len - 1, n_layers, oct_, gpo, S5_CH, S5_CH).transpose(1, 2, 0, 3, 5, 4)
    bd = (w_lag[..., None, :] * eye[:, None, :, None]).reshape(n_layers, oct_, 2 * t_len - 1, LANES, LANES)

    def expand_p(pw_r, pw_i):
        pc = jnp.concatenate([pw_r, pw_i], axis=3).reshape(t_len, n_layers, oct_, gpo, 2 * S5_STATE, S5_CH)
        pc = pc.transpose(1, 2, 0, 3, 5, 4)
        return (pc[..., None, :] * eye[:, None, :, None]).reshape(n_layers, oct_, S5_OCT_K, gpo * 2 * S5_STATE)

    p_in = jnp.stack([expand_p(jnp.flip(abr[:t_len, :, 0], 0), jnp.flip(abi[:t_len, :, 0], 0)),
                      expand_p(abr[:t_len, :, 1], abi[:t_len, :, 1])], axis=1)

    def expand_q(pw_r, pw_i, d):
        qr = c_re[:, d][None] * pw_r[:, :, :, None, :] - c_im[:, d][None] * pw_i[:, :, :, None, :]
        qi = c_re[:, d][None] * pw_i[:, :, :, None, :] + c_im[:, d][None] * pw_r[:, :, :, None, :]
        q = lax.optimization_barrier(jnp.concatenate([qr, -qi], axis=-1))
        q = q.reshape(t_len, n_layers, oct_, gpo, 1, S5_CH, 2 * S5_STATE) * eye[:, :, None, None]
        return q.transpose(1, 2, 3, 6, 0, 4, 5).reshape(n_layers, oct_, gpo * 2 * S5_STATE, S5_OCT_K)

    q_out = jnp.stack([expand_q(pr[1:, :, 0], pi[1:, :, 0], 0),
                       expand_q(jnp.flip(pr[1:, :, 1], 0), jnp.flip(pi[1:, :, 1], 0), 1)], axis=1)
    a_chunk = jnp.stack([jnp.concatenate([pr[t_len], pr[t_len]], axis=-1),
                         jnp.concatenate([-pi[t_len], pi[t_len]], axis=-1)], axis=2)
    return bd.astype(BF16), p_in.astype(BF16), q_out.astype(BF16), a_chunk


def _s5_ucat(z_ref):
    return jnp.concatenate([z_ref[:, t, :].astype(BF16) for t in range(S5_CHUNK)], axis=1)


def _s5_in_kernel(z_ref, p_ref, x_ref):
    u = _s5_ucat(z_ref)
    gpo = S5_GROUPS // S5_OCTETS
    for d in range(2):
        x_ref[d] = _dot(u, p_ref[d]).reshape(u.shape[0], gpo, 2 * S5_STATE)


def _s5_in(z3, p_in, l, tr):
    r16 = z3.shape[0]
    gpo = S5_GROUPS // S5_OCTETS
    w = 2 * S5_STATE
    return pl.pallas_call(
        _s5_in_kernel,
        grid=(S5_OCTETS, r16 // tr),
        in_specs=[pl.BlockSpec((tr, S5_CHUNK, LANES), lambda o, i: (i, 0, S5_COL0 // LANES + o)),
                  pl.BlockSpec((None, 2, None, S5_OCT_K, gpo * w), lambda o, i: (l, 0, o, 0, 0))],
        out_specs=pl.BlockSpec((2, tr, gpo, w), lambda o, i: (0, i, o, 0)),
        out_shape=jax.ShapeDtypeStruct((2, r16, S5_GROUPS, w), F32),
        compiler_params=_cparams(2, _nbytes((tr, S5_CHUNK, LANES), F32), _nbytes((2, S5_OCT_K, gpo * w), BF16),
                                 _nbytes((2, tr, gpo, w), F32), extra=4 * _nbytes((tr, S5_OCT_K), F32)),
        name="s5_in",
    )(z3, p_in)


def _s5_scan_kernel(*refs, n_in, nsb, jb):
    a_ref, h0_ref = refs[0], refs[1]
    xf = refs[2:2 + n_in]
    xb = refs[2 + n_in:2 + 2 * n_in]
    sf = refs[2 + 2 * n_in:2 + 3 * n_in]
    sb = refs[2 + 3 * n_in:2 + 4 * n_in]
    fin_ref, st_ref = refs[2 + 4 * n_in], refs[3 + 4 * n_in]
    t = pl.program_id(0)

    @pl.when(t == 0)
    def _():
        st_ref[...] = h0_ref[...]

    half = S5_STATE

    def body(s, carry):
        jf = s
        jr = jb - 1 - s
        for r in range(n_in):
            for q in range(nsb):
                idx = r * nsb + q
                cur = st_ref[0, idx]
                sf[r][q * jb + jf] = cur
                st_ref[0, idx] = cur * a_ref[0, 0] + pltpu.roll(cur, half, 1) * a_ref[0, 1] + xf[r][q * jb + jf]
                cur = st_ref[1, idx]
                sb[r][q * jb + jr] = cur
                st_ref[1, idx] = cur * a_ref[1, 0] + pltpu.roll(cur, half, 1) * a_ref[1, 1] + xb[r][q * jb + jr]
        return carry

    lax.fori_loop(0, jb, body, 0)

    @pl.when(t == pl.num_programs(0) - 1)
    def _():
        fin_ref[...] = st_ref[...]


def _s5_scan(x, a_chunk, h0, l, row0, n_in, nsb, nj, jb):
    g = x.shape[2]
    w = x.shape[3]
    nblk = nj // jb
    nseq = n_in * nsb
    rows = nsb * jb
    base = row0 // rows

    def xspec(d, r):
        if d == 0:
            return pl.BlockSpec((None, rows, g, w), lambda t: (0, base + r * nblk + t, 0, 0))
        return pl.BlockSpec((None, rows, g, w), lambda t: (1, base + r * nblk + nblk - 1 - t, 0, 0))

    def sspec(d):
        if d == 0:
            return pl.BlockSpec((rows, g, w), lambda t: (t, 0, 0))
        return pl.BlockSpec((rows, g, w), lambda t: (nblk - 1 - t, 0, 0))

    st_spec = pl.BlockSpec((2, nseq, g, w), lambda t: (0, 0, 0, 0))
    res = pl.pallas_call(
        functools.partial(_s5_scan_kernel, n_in=n_in, nsb=nsb, jb=jb),
        grid=(nblk,),
        in_specs=[pl.BlockSpec((None, 2, 2, g, w), lambda t: (l, 0, 0, 0, 0)), st_spec]
                 + [xspec(0, r) for r in range(n_in)] + [xspec(1, r) for r in range(n_in)],
        out_specs=[sspec(0)] * n_in + [sspec(1)] * n_in + [st_spec],
        out_shape=[jax.ShapeDtypeStruct((nsb * nj, g, w), F32)] * (2 * n_in)
                  + [jax.ShapeDtypeStruct((2, nseq, g, w), F32)],
        scratch_shapes=[pltpu.VMEM((2, nseq, g, w), F32)],
        compiler_params=_cparams(1, 4 * n_in * _nbytes((rows, g, w), F32), 3 * _nbytes((2, nseq, g, w), F32)),
        name="s5_scan",
    )(a_chunk, h0, *([x] * (2 * n_in)))
    return res[:n_in], res[n_in:2 * n_in], res[2 * n_in]


def _s5_out_kernel(z_ref, bd_ref, q_ref, d_ref, sf_ref, sb_ref, y_ref, m_ref):
    @pl.when(pl.program_id(1) == 0)
    def _():
        for t in range(S5_CHUNK):
            for t2 in range(S5_CHUNK):
                m_ref[t * LANES:(t + 1) * LANES, t2 * LANES:(t2 + 1) * LANES] = bd_ref[t2 - t + S5_CHUNK - 1]

    u = _s5_ucat(z_ref)
    rows = u.shape[0]
    width = sf_ref.shape[1] * sf_ref.shape[2]
    y = _dot(u, m_ref[...])
    y += _dot(sf_ref[...].reshape(rows, width).astype(BF16), q_ref[0])
    y += _dot(sb_ref[...].reshape(rows, width).astype(BF16), q_ref[1])
    for t in range(S5_CHUNK):
        y_ref[:, t, :] = y[:, t * LANES:(t + 1) * LANES] + z_ref[:, t, :] * d_ref[...]


def _s5_out(z3, bd, q_out, d_oct, sin_f, sin_b, l, tr):
    r16 = z3.shape[0]
    gpo = S5_GROUPS // S5_OCTETS
    w = 2 * S5_STATE
    sspec = pl.BlockSpec((tr, gpo, w), lambda o, i: (i, o, 0))
    return pl.pallas_call(
        _s5_out_kernel,
        grid=(S5_OCTETS, r16 // tr),
        in_specs=[pl.BlockSpec((tr, S5_CHUNK, LANES), lambda o, i: (i, 0, S5_COL0 // LANES + o)),
                  pl.BlockSpec((None, None, 2 * S5_CHUNK - 1, LANES, LANES), lambda o, i: (l, o, 0, 0, 0)),
                  pl.BlockSpec((None, 2, None, gpo * w, S5_OCT_K), lambda o, i: (l, 0, o, 0, 0)),
                  pl.BlockSpec((None, None, 1, LANES), lambda o, i: (l, o, 0, 0)),
                  sspec, sspec],
        out_specs=pl.BlockSpec((tr, S5_CHUNK, LANES), lambda o, i: (i, 0, o)),
        out_shape=jax.ShapeDtypeStruct((r16, S5_CHUNK, D_BRANCH), F32),
        scratch_shapes=[pltpu.VMEM((S5_OCT_K, S5_OCT_K), BF16)],
        compiler_params=_cparams(2, 2 * _nbytes((tr, S5_CHUNK, LANES), F32), _nbytes((2, gpo * w, S5_OCT_K), BF16),
                                 2 * _nbytes((tr, gpo, w), F32), _nbytes((2 * S5_CHUNK - 1, LANES, LANES), BF16),
                                 extra=_nbytes((S5_OCT_K, S5_OCT_K), BF16) + 4 * _nbytes((tr, S5_OCT_K), F32)),
        name="s5_out",
    )(z3, bd, q_out, d_oct, sin_f, sin_b)


def _s5_glu_kernel(y_ref, w_ref, b_ref, o_ref, wb_ref):
    @pl.when(pl.program_id(0) == 0)
    def _():
        wb_ref[...] = w_ref[...].astype(BF16)

    y = _gelu(y_ref[...])
    o_ref[...] = (y * _sigmoid(_dot(y.astype(BF16), wb_ref[...]) + b_ref[...])).astype(BF16)


def _s5_glu(cfg, y, w, b, l):
    tm = min(512, cfg.ll)
    return pl.pallas_call(
        _s5_glu_kernel,
        grid=(cfg.r // tm,),
        in_specs=[pl.BlockSpec((tm, D_BRANCH), lambda i: (i, 0)),
                  pl.BlockSpec((None, D_BRANCH, D_BRANCH), lambda i: (l, 0, 0)),
                  pl.BlockSpec((None, 1, D_BRANCH), lambda i: (l, 0, 0))],
        out_specs=pl.BlockSpec((tm, D_BRANCH), lambda i: (i, 0)),
        out_shape=jax.ShapeDtypeStruct((cfg.r, D_BRANCH), BF16),
        scratch_shapes=[pltpu.VMEM((D_BRANCH, D_BRANCH), BF16)],
        compiler_params=_cparams(1, 2 * _nbytes((tm, D_BRANCH), F32), _nbytes((D_BRANCH, D_BRANCH), F32),
                                 extra=_nbytes((D_BRANCH, D_BRANCH), BF16) + 4 * _nbytes((tm, D_BRANCH), F32)),
        name="s5_glu",
    )(y, w, b.reshape(DEPTH, 1, D_BRANCH))


def _s5(cfg, s5m, z, p, h0_lat, l):
    bd, p_in, q_out, a_chunk = s5m
    r16 = cfg.r // S5_CHUNK
    tr = min(256, cfg.ll // S5_CHUNK)
    z3 = z.reshape(r16, S5_CHUNK, N_IN)
    x = _s5_in(z3, p_in, l, tr)
    njc, njl = cfg.lc // S5_CHUNK, cfg.ll // S5_CHUNK
    rc16 = cfg.rc // S5_CHUNK
    zero_h0 = jnp.zeros((2, cfg.nc, S5_GROUPS, 2 * S5_STATE), F32)
    sf_c, sb_c, fin_ctx = _s5_scan(x, a_chunk, zero_h0, l, 0, 1, cfg.nc, njc, njc)
    sf_l, sb_l, _ = _s5_scan(x, a_chunk, h0_lat, l, rc16, cfg.nl, 1, njl, min(32, njl))
    sin_f = jnp.concatenate(list(sf_c) + list(sf_l), axis=0)
    sin_b = jnp.concatenate(list(sb_c) + list(sb_l), axis=0)
    d_oct = p["s5_d"].reshape(DEPTH, S5_OCTETS, 1, LANES)
    y = _s5_out(z3, bd, q_out, d_oct, sin_f, sin_b, l, tr).reshape(cfg.r, D_BRANCH)
    return _s5_glu(cfg, y, p["s5_glu_w"], p["s5_glu_b"], l), fin_ctx


def _forward(cfg, x_prompt, x_sample, c, cache_k, cache_v, state_ssm_re, state_ssm_im, c_ctx, p):
    d = D_MODEL
    x = jnp.concatenate([x_prompt.reshape(cfg.rc, d), x_sample.reshape(cfg.rl, d)], axis=0)
    cvec = jnp.concatenate([c_ctx[None], c, jnp.zeros((MOD_ROWS - cfg.nseg, d), F32)], axis=0)
    mod = _mod(cvec, p["w_mod"], p["b_mod"])[:, :cfg.nseg].reshape(DEPTH, cfg.nseg, 6, 1, d)
    mod = [[mod[l, :, i] for i in range(6)] for l in range(DEPTH)]
    norm_g = p["norm_g"]

    consts = {
        "dft": {},
        "feats": {n: _hy_feats(n) for n in {cfg.lc, cfg.ll}},
        "rope": _rope_tables(cfg.ll),
    }
    for n in {cfg.lc, cfg.ll}:
        fc, fs, fct, fst = _dft_tables(n)
        consts["dft"][n] = {"fc_hi": fc.astype(BF16), "fs_hi": fs.astype(BF16),
                            "fct_hi": fct.astype(BF16), "fst_hi": fst.astype(BF16)}
    s5m = _s5_matrices(p)

    kv_shape = (cfg.nc, cfg.lc, N_KV_HEADS, HEAD_DIM)
    ks, vs, srs, sis = [], [], [], []
    h = _resid_norm(cfg, x, gpre=norm_g[0, 0], sc=mod[0][1], sh=mod[0][0])
    for l in range(DEPTH):
        z = _mm(h, p["w_in"], l, tm=cfg.tm, tn=512)
        ks.append(z[:cfg.rc, K_COL0:K_COL0 + KV_COLS].reshape(kv_shape))
        vs.append(z[:cfg.rc, V_COL0:V_COL0 + KV_COLS].reshape(kv_shape))

        y_hy = _hyena(cfg, consts, z, p, l)
        sink = jnp.broadcast_to(p["attn_sink"][l][:, None], (N_Q_HEADS, LANES))
        past = cache_k.shape[2]
        y_at = jnp.concatenate([
            _attn_ctx(cfg, z, sink),
            _attn_lat(cfg, consts, z, cache_k[:, l].reshape(cfg.nl, past, KV_COLS),
                      cache_v[:, l].reshape(cfg.nl, past, KV_COLS), sink)], axis=0)
        h0 = jnp.concatenate([state_ssm_re[:, l], state_ssm_im[:, l]], axis=-1).transpose(1, 0, 2, 3)
        y_s5, fin = _s5(cfg, s5m, z, p, h0, l)
        srs.append(fin[..., :S5_STATE].transpose(1, 0, 2, 3))
        sis.append(fin[..., S5_STATE:].transpose(1, 0, 2, 3))

        merged = _merge(cfg, z, (y_hy, y_at, y_s5),
                        (p["w_branch_hy"], p["w_branch_attn"], p["w_branch_s5"]), l)
        y = _mm(merged, p["w_out"], l, tm=cfg.tm, tn=512)
        x, h = _resid_norm(cfg, x, y, gate=mod[l][2], gpost=norm_g[l, 1],
                           gpre=norm_g[l, 2], sc=mod[l][4], sh=mod[l][3])
        act = _ffn_up(cfg, h, p["ffn_w_up"], p["ffn_conv_w"], p["ffn_conv_b"], l)
        f = _mm(act, p["ffn_w_down"], l, tm=min(512, cfg.tm), tn=512, w_buffers=1)
        if l + 1 < DEPTH:
            x, h = _resid_norm(cfg, x, f, gate=mod[l][5], gpost=norm_g[l, 3],
                               gpre=norm_g[l + 1, 0], sc=mod[l + 1][1], sh=mod[l + 1][0])
        else:
            x = _resid_norm(cfg, x, f, gate=mod[l][5], gpost=norm_g[l, 3])

    return (x[:cfg.rc].reshape(cfg.nc, cfg.lc, d), x[cfg.rc:].reshape(cfg.nl, cfg.ll, d),
            jnp.stack(ks, axis=1), jnp.stack(vs, axis=1), jnp.stack(srs, axis=1), jnp.stack(sis, axis=1))


def kernel(x_prompt, x_sample, c, cache_k, cache_v, state_ssm_re, state_ssm_im, c_ctx, w_mod, b_mod, norm_g, w_in, hy_conv_w, hy_conv_b, hy_w1, hy_b1, hy_w2, hy_b2, hy_w3, hy_freq, hy_decay, hy_bias, attn_sink, s5_lam_re, s5_lam_im, s5_log_dt, s5_b_re, s5_b_im, s5_c_re, s5_c_im, s5_d, s5_glu_w, s5_glu_b, w_branch_hy, w_branch_attn, w_branch_s5, w_out, ffn_w_up, ffn_conv_w, ffn_conv_b, ffn_w_down):
    p = dict(w_mod=w_mod, b_mod=b_mod, norm_g=norm_g, w_in=w_in, hy_conv_w=hy_conv_w, hy_conv_b=hy_conv_b,
             hy_w1=hy_w1, hy_b1=hy_b1, hy_w2=hy_w2, hy_b2=hy_b2, hy_w3=hy_w3, hy_freq=hy_freq,
             hy_decay=hy_decay, hy_bias=hy_bias, attn_sink=attn_sink, s5_lam_re=s5_lam_re,
             s5_lam_im=s5_lam_im, s5_log_dt=s5_log_dt, s5_b_re=s5_b_re, s5_b_im=s5_b_im, s5_c_re=s5_c_re,
             s5_c_im=s5_c_im, s5_d=s5_d, s5_glu_w=s5_glu_w, s5_glu_b=s5_glu_b, w_branch_hy=w_branch_hy,
             w_branch_attn=w_branch_attn, w_branch_s5=w_branch_s5, w_out=w_out, ffn_w_up=ffn_w_up,
             ffn_conv_w=ffn_conv_w, ffn_conv_b=ffn_conv_b, ffn_w_down=ffn_w_down)
    cfg = _Cfg(x_prompt.shape[0], x_prompt.shape[1], x_sample.shape[0], x_sample.shape[1])
    return _forward(cfg, x_prompt, x_sample, c, cache_k, cache_v, state_ssm_re, state_ssm_im, c_ctx, p)
```

```python
import functools
import math

import numpy as np
import jax
import jax.numpy as jnp
from jax import lax
from jax.experimental import pallas as pl
from jax.experimental.pallas import tpu as pltpu

F32 = jnp.float32
BF16 = jnp.bfloat16

D_MODEL = 4096
DEPTH = 2
GRID_W = 64
D_BRANCH = D_MODEL // 4
HY_WIDTH = D_BRANCH
HY_BANDS = 16
HY_FEAT = 1 + 2 * HY_BANDS
HY_HID = 64
HEAD_DIM = 128
N_Q_HEADS = D_BRANCH // HEAD_DIM
N_KV_HEADS = 2
GQA_GROUP = N_Q_HEADS // N_KV_HEADS
WINDOW = 128
BLOCK = 128
ROPE_BASE = 10000.0
S5_CH = 16
S5_GROUPS = D_BRANCH // S5_CH
S5_STATE = 64
S5_CHUNK = 16
S5_OCTETS = 8
S5_OCT_K = S5_CHUNK * 128
D_FF = 2 * D_MODEL
EPS = 1e-6
NEG_INF = -1e30

HY_COLS = 3 * HY_WIDTH
Q_COL0 = HY_COLS
K_COL0 = Q_COL0 + N_Q_HEADS * HEAD_DIM
KV_COLS = N_KV_HEADS * HEAD_DIM
V_COL0 = K_COL0 + KV_COLS
S5_COL0 = V_COL0 + KV_COLS
GATE_COL0 = S5_COL0 + D_BRANCH
N_IN = GATE_COL0 + 3 * D_MODEL

V7X_VMEM_REQUEST_MAX = 60 * 1024 * 1024
LANES = 128
MOD_ROWS = 8


def _nbytes(shape, dtype):
    return math.prod(shape) * jnp.dtype(dtype).itemsize


def _cparams(n_grid, *block_bytes, extra=0):
    est = 2 * sum(block_bytes) + extra + (4 << 20)
    return pltpu.CompilerParams(
        dimension_semantics=("arbitrary",) * n_grid,
        vmem_limit_bytes=int(min(max(est, 16 << 20), V7X_VMEM_REQUEST_MAX)))


def _dot(a, b):
    return jnp.dot(a, b, preferred_element_type=F32)


def _dot_nt(a, b):
    return lax.dot_general(a, b, (((1,), (1,)), ((), ())), preferred_element_type=F32)


def _split(a):
    hi = a.astype(BF16)
    return hi, (a - hi.astype(F32)).astype(BF16)


def _dot3(a, b):
    ah, al = _split(a)
    bh, bl = _split(b)
    return _dot(ah, bh) + (_dot(ah, bl) + _dot(al, bh))


def _sigmoid(x):
    return 1.0 / (1.0 + jnp.exp(-x))


def _gelu(x):
    return 0.5 * x * (1.0 + jnp.tanh(math.sqrt(2.0 / math.pi) * (x + 0.044715 * (x * x * x))))


def _rms(x, g):
    return x * lax.rsqrt(jnp.mean(x * x, axis=-1, keepdims=True) + EPS) * g


class _Cfg:
    def __init__(self, nc, lc, nl, ll):
        self.nc, self.lc, self.nl, self.ll = nc, lc, nl, ll
        self.rc = nc * lc
        self.rl = nl * ll
        self.r = self.rc + self.rl
        self.nseg = 1 + nl
        assert self.rc % ll == 0 and ll % lc == 0 and lc % BLOCK == 0
        self.tm = min(1024, ll)
        self.rb = ll

    def seg_of_row(self, row0):
        return jnp.where(row0 >= self.rc, (row0 - self.rc) // self.ll + 1, 0)


def _mod_kernel(c_ref, w_ref, b_ref, o_ref):
    c = c_ref[...]
    s = (c * _sigmoid(c)).astype(BF16)
    o_ref[...] = _dot(s, w_ref[...].astype(BF16)) + b_ref[...]


def _mod(cvec, w_mod, b_mod):
    n = w_mod.shape[-1]
    tn = 512
    return pl.pallas_call(
        _mod_kernel,
        grid=(DEPTH, n // tn),
        in_specs=[pl.BlockSpec((MOD_ROWS, D_MODEL), lambda l, j: (0, 0)),
                  pl.BlockSpec((None, D_MODEL, tn), lambda l, j: (l, 0, j)),
                  pl.BlockSpec((None, 1, tn), lambda l, j: (l, 0, j))],
        out_specs=pl.BlockSpec((None, MOD_ROWS, tn), lambda l, j: (l, 0, j)),
        out_shape=jax.ShapeDtypeStruct((DEPTH, MOD_ROWS, n), F32),
        compiler_params=_cparams(2, _nbytes((D_MODEL, tn), F32), extra=_nbytes((D_MODEL, tn), BF16)),
        name="mod",
    )(cvec, w_mod, b_mod.reshape(DEPTH, 1, n))


def _resid_norm_kernel(*refs, has_y, has_h):
    it = iter(refs)
    x_ref = next(it)
    if has_y:
        y_ref, gate_ref, gpost_ref = next(it), next(it), next(it)
    if has_h:
        gpre_ref, sc_ref, sh_ref = next(it), next(it), next(it)
    x = x_ref[...]
    if has_y:
        x = x + gate_ref[...] * _rms(y_ref[...], gpost_ref[...])
        xo_ref = next(it)
        xo_ref[...] = x
    if has_h:
        ho_ref = next(it)
        ho_ref[...] = (_rms(x, gpre_ref[...]) * (1.0 + sc_ref[...]) + sh_ref[...]).astype(BF16)


def _resid_norm(cfg, x, y=None, gate=None, gpost=None, gpre=None, sc=None, sh=None):
    has_y, has_h = y is not None, gpre is not None
    tm = 256
    row = pl.BlockSpec((tm, D_MODEL), lambda i: (i, 0))
    vec = pl.BlockSpec((1, D_MODEL), lambda i: (0, 0))
    seg = pl.BlockSpec((None, 1, D_MODEL), lambda i: (cfg.seg_of_row(i * tm), 0, 0))
    args, specs, outs, ospecs = [x], [row], [], []
    if has_y:
        args += [y, gate, gpost.reshape(1, D_MODEL)]
        specs += [row, seg, vec]
        outs.append(jax.ShapeDtypeStruct((cfg.r, D_MODEL), F32))
        ospecs.append(row)
    if has_h:
        args += [gpre.reshape(1, D_MODEL), sc, sh]
        specs += [vec, seg, seg]
        outs.append(jax.ShapeDtypeStruct((cfg.r, D_MODEL), BF16))
        ospecs.append(row)
    res = pl.pallas_call(
        functools.partial(_resid_norm_kernel, has_y=has_y, has_h=has_h),
        grid=(cfg.r // tm,),
        in_specs=specs, out_specs=ospecs, out_shape=outs,
        compiler_params=_cparams(1, 4 * _nbytes((tm, D_MODEL), F32)),
        name="resid_norm",
    )(*args)
    return res if len(res) > 1 else res[0]


def _mm_kernel(x_ref, w_ref, o_ref, wb_ref):
    @pl.when(pl.program_id(1) == 0)
    def _():
        wb_ref[...] = w_ref[...].astype(BF16)

    o_ref[...] = _dot(x_ref[...], wb_ref[...]).astype(o_ref.dtype)


def _mm(x, w, l, *, tm, tn, out_dtype=F32, w_buffers=2):
    m, k = x.shape
    n = w.shape[-1]
    w_mode = {} if w_buffers == 2 else {"pipeline_mode": pl.Buffered(w_buffers)}
    return pl.pallas_call(
        _mm_kernel,
        grid=(n // tn, m // tm),
        in_specs=[pl.BlockSpec((tm, k), lambda j, i: (i, 0)),
                  pl.BlockSpec((None, k, tn), lambda j, i: (l, 0, j), **w_mode)],
        out_specs=pl.BlockSpec((tm, tn), lambda j, i: (i, j)),
        out_shape=jax.ShapeDtypeStruct((m, n), out_dtype),
        scratch_shapes=[pltpu.VMEM((k, tn), BF16)],
        compiler_params=_cparams(2, _nbytes((tm, k), BF16), _nbytes((tm, tn), out_dtype),
                                 extra=w_buffers * _nbytes((k, tn), F32) + _nbytes((k, tn), BF16)
                                 + _nbytes((tm, tn), F32)),
        name="mm",
    )(x, w)


def _merge_kernel(g0, g1, g2, y0, y1, y2, w0, w1, w2, o_ref, wb_ref):
    @pl.when(pl.program_id(1) == 0)
    def _():
        wb_ref[0] = w0[...].astype(BF16)
        wb_ref[1] = w1[...].astype(BF16)
        wb_ref[2] = w2[...].astype(BF16)

    acc = _sigmoid(g0[...]) * _dot(y0[...], wb_ref[0])
    acc += _sigmoid(g1[...]) * _dot(y1[...], wb_ref[1])
    acc += _sigmoid(g2[...]) * _dot(y2[...], wb_ref[2])
    o_ref[...] = acc.astype(o_ref.dtype)


def _merge(cfg, z, ys, ws, l):
    tm, tn = cfg.tm, 512
    gate_specs = [pl.BlockSpec((tm, tn), functools.partial(
        lambda j, i, b: (i, (GATE_COL0 + b * D_MODEL) // tn + j), b=b)) for b in range(3)]
    y_spec = pl.BlockSpec((tm, D_BRANCH), lambda j, i: (i, 0))
    w_spec = pl.BlockSpec((None, D_BRANCH, tn), lambda j, i: (l, 0, j))
    return pl.pallas_call(
        _merge_kernel,
        grid=(D_MODEL // tn, cfg.r // tm),
        in_specs=gate_specs + [y_spec] * 3 + [w_spec] * 3,
        out_specs=pl.BlockSpec((tm, tn), lambda j, i: (i, j)),
        out_shape=jax.ShapeDtypeStruct((cfg.r, D_MODEL), BF16),
        scratch_shapes=[pltpu.VMEM((3, D_BRANCH, tn), BF16)],
        compiler_params=_cparams(2, 3 * _nbytes((tm, tn), F32), 3 * _nbytes((tm, D_BRANCH), BF16),
                                 3 * _nbytes((D_BRANCH, tn), F32), _nbytes((tm, tn), BF16),
                                 extra=3 * _nbytes((D_BRANCH, tn), BF16) + 2 * _nbytes((tm, tn), F32)),
        name="merge",
    )(z, z, z, *ys, *ws)


def _seq_pos(cfg, shape):
    sl = jnp.where(pl.program_id(0) < cfg.rc // cfg.rb, cfg.lc, cfg.ll)
    return lax.broadcasted_iota(jnp.int32, shape, 0) & (sl - 1), sl


def _dwconv3(x, w_ref, b_ref, t, sl):
    rows = x.shape[0]
    xm = jnp.where(t == 0, 0.0, pltpu.roll(x, 1, 0))
    xp = jnp.where(t == sl - 1, 0.0, pltpu.roll(x, rows - 1, 0))
    return xm * w_ref[0:1, :] + x * w_ref[1:2, :] + xp * w_ref[2:3, :] + b_ref[...]


def _hy_dwconv_kernel(x_ref, w_ref, b_ref, o_ref, *, cfg):
    x = x_ref[...]
    t, sl = _seq_pos(cfg, x.shape)
    o_ref[...] = _dwconv3(x, w_ref, b_ref, t, sl)


def _hy_dwconv(cfg, z, w, b, l):
    tn = 256
    blk = _nbytes((cfg.rb, tn), F32)
    return pl.pallas_call(
        functools.partial(_hy_dwconv_kernel, cfg=cfg),
        grid=(cfg.r // cfg.rb, HY_COLS // tn),
        in_specs=[pl.BlockSpec((cfg.rb, tn), lambda r, j: (r, j)),
                  pl.BlockSpec((None, 3, tn), lambda r, j: (l, 0, j)),
                  pl.BlockSpec((None, 1, tn), lambda r, j: (l, 0, j))],
        out_specs=pl.BlockSpec((cfg.rb, tn), lambda r, j: (r, j)),
        out_shape=jax.ShapeDtypeStruct((cfg.r, HY_COLS), F32),
        compiler_params=_cparams(2, 2 * blk, extra=4 * blk),
        name="hy_dwconv",
    )(z, w, b.reshape(DEPTH, 1, HY_COLS))


def _ffn_act_kernel(a_ref, b_ref, wa_ref, wb_ref, ba_ref, bb_ref, o_ref, *, cfg):
    a = a_ref[...]
    t, sl = _seq_pos(cfg, a.shape)
    a = _dwconv3(a, wa_ref, ba_ref, t, sl)
    b = _dwconv3(b_ref[...], wb_ref, bb_ref, t, sl)
    o_ref[...] = (_gelu(a) * b).astype(BF16)


def _ffn_act(cfg, u, w, b, l):
    tn = 256
    nb = D_FF // tn
    blk = _nbytes((cfg.rb, tn), F32)
    b3 = b.reshape(DEPTH, 1, 2 * D_FF)
    return pl.pallas_call(
        functools.partial(_ffn_act_kernel, cfg=cfg),
        grid=(cfg.r // cfg.rb, nb),
        in_specs=[pl.BlockSpec((cfg.rb, tn), lambda r, j: (r, j)),
                  pl.BlockSpec((cfg.rb, tn), lambda r, j: (r, j + nb)),
                  pl.BlockSpec((None, 3, tn), lambda r, j: (l, 0, j)),
                  pl.BlockSpec((None, 3, tn), lambda r, j: (l, 0, j + nb)),
                  pl.BlockSpec((None, 1, tn), lambda r, j: (l, 0, j)),
                  pl.BlockSpec((None, 1, tn), lambda r, j: (l, 0, j + nb))],
        out_specs=pl.BlockSpec((cfg.rb, tn), lambda r, j: (r, j)),
        out_shape=jax.ShapeDtypeStruct((cfg.r, D_FF), BF16),
        compiler_params=_cparams(2, 3 * blk, extra=6 * blk),
        name="ffn_act",
    )(u, u, w, w, b3, b3)


def _dft_tables(length):
    period = 4 * length
    r = 1 << ((length.bit_length()) // 2)
    idx = np.arange(length, dtype=np.int64)

    def cs(m):
        ang = 2.0 * np.pi * (m % period).astype(np.float64) / period
        return jnp.asarray(np.cos(ang), F32), jnp.asarray(np.sin(ang), F32)

    def combine(ca, sa, cb, sb):
        c = ca[:, :, None] * cb[:, None, :] - sa[:, :, None] * sb[:, None, :]
        s = sa[:, :, None] * cb[:, None, :] + ca[:, :, None] * sb[:, None, :]
        return c.reshape(length, length), s.reshape(length, length)

    odd = 2 * idx + 1
    ca, sa = cs(odd[:, None] * (np.arange(length // r) * r)[None, :])
    cb, sb = cs(odd[:, None] * np.arange(r)[None, :])
    fc, fs = combine(ca, sa, cb, sb)
    ca, sa = cs(idx[:, None] * (2 * r * np.arange(length // r))[None, :])
    cb, sb = cs(idx[:, None] * (2 * np.arange(r) + 1)[None, :])
    fct, fst = combine(ca, sa, cb, sb)
    return fc, fs, fct, fst


def _hy_feats(length):
    pos = jnp.arange(length, dtype=F32)
    t = (pos / length)[:, None]
    bands = jnp.linspace(1e-4, HY_BANDS - 1, HY_BANDS, dtype=F32)
    wpos = 2.0 * math.pi * t * bands
    feats = jnp.concatenate([t, jnp.cos(wpos), -jnp.sin(wpos)], axis=-1)
    return jnp.pad(feats, ((0, 0), (0, LANES - HY_FEAT)))


def _hy_filter_kernel(f_ref, w1_ref, b1_ref, w2_ref, b2_ref, w3_ref, fr_ref, dec_ref, hs_ref, hd_ref, hb0_ref):
    feats = f_ref[...]
    fr = fr_ref[...]
    h = jnp.sin(fr * (_dot3(feats, w1_ref[...]) + b1_ref[...]))
    h = jnp.sin(fr * (_dot3(h, w2_ref[...]) + b2_ref[...]))
    h = _dot3(h, w3_ref[...]) * jnp.exp(-feats[:, 0:1] * jnp.abs(dec_ref[...]))
    half = 2 * HY_WIDTH
    hf, hb = h[:, :half], h[:, half:]
    hs_ref[...] = (hf + hb).astype(BF16)
    hd_ref[...] = (hb - hf).astype(BF16)

    @pl.when(pl.program_id(0) == 0)
    def _():
        hb0_ref[...] = jnp.broadcast_to(hb[0:1, :], hb0_ref.shape)


def _hy_filter(length, feats, w1p, b1, w2, b2, w3, freq, decay, l):
    tl = 256
    half = 2 * HY_WIDTH
    lsel = lambda i: (l, 0, 0)
    return pl.pallas_call(
        _hy_filter_kernel,
        grid=(length // tl,),
        in_specs=[pl.BlockSpec((tl, LANES), lambda i: (i, 0)),
                  pl.BlockSpec((None, LANES, HY_HID), lsel),
                  pl.BlockSpec((None, 1, HY_HID), lsel),
                  pl.BlockSpec((None, HY_HID, HY_HID), lsel),
                  pl.BlockSpec((None, 1, HY_HID), lsel),
                  pl.BlockSpec((None, HY_HID, 2 * half), lsel),
                  pl.BlockSpec((None, 1, HY_HID), lsel),
                  pl.BlockSpec((None, 1, 2 * half), lsel)],
        out_specs=[pl.BlockSpec((tl, half), lambda i: (i, 0)),
                   pl.BlockSpec((tl, half), lambda i: (i, 0)),
                   pl.BlockSpec((8, half), lambda i: (0, 0))],
        out_shape=[jax.ShapeDtypeStruct((length, half), BF16),
                   jax.ShapeDtypeStruct((length, half), BF16),
                   jax.ShapeDtypeStruct((8, half), F32)],
        compiler_params=_cparams(1, 2 * _nbytes((tl, half), F32), extra=6 * _nbytes((tl, 2 * half), F32)),
        name="hy_filter",
    )(feats, w1p, b1, w2, b2, w3, freq, decay)


def _hy_spec_kernel(fc, fs, hs_ref, hd_ref, hb0_ref, kre_ref, kim_ref):
    kre_ref[...] = _dot(fc[...], hs_ref[...]) - hb0_ref[0:1, :]
    kim_ref[...] = _dot(fs[...], hd_ref[...])


def _hy_spec(length, mats, hs, hd, hb0):
    tk = min(512, length)
    tn = 512
    half = 2 * HY_WIDTH
    fspec = pl.BlockSpec((tk, length), lambda j, i: (i, 0))
    hspec = pl.BlockSpec((length, tn), lambda j, i: (0, j))
    kspec = pl.BlockSpec((tk, tn), lambda j, i: (i, j))
    return pl.pallas_call(
        _hy_spec_kernel,
        grid=(half // tn, length // tk),
        in_specs=[fspec, fspec, hspec, hspec, pl.BlockSpec((8, tn), lambda j, i: (0, j))],
        out_specs=[kspec, kspec],
        out_shape=[jax.ShapeDtypeStruct((length, half), F32)] * 2,
        compiler_params=_cparams(2, 2 * _nbytes((tk, length), BF16), 2 * _nbytes((length, tn), BF16),
                                 2 * _nbytes((tk, tn), F32), extra=2 * _nbytes((tk, tn), F32)),
        name="hy_spec",
    )(mats["fc_hi"], mats["fs_hi"], hs, hd, hb0)


def _hy_fwd_kernel(fc, fs, u_ref, kre, kim, yre_ref, yim_ref, ub_ref):
    @pl.when(pl.program_id(2) == 0)
    def _():
        ub_ref[...] = u_ref[...].astype(BF16)

    ure = _dot(fc[...], ub_ref[...])
    uim = -_dot(fs[...], ub_ref[...])
    yre_ref[...] = (ure * kre[...] - uim * kim[...]).astype(BF16)
    yim_ref[...] = (ure * kim[...] + uim * kre[...]).astype(BF16)


def _hy_inv_kernel(fct, fst, yre, yim, u_ref, gate_ref, bias_ref, o_ref, *, inv_len):
    acc = _dot(fct[...], yre[...]) - _dot(fst[...], yim[...])
    o_ref[...] = (gate_ref[...] * (acc * inv_len + u_ref[...] * bias_ref[...])).astype(o_ref.dtype)


def _hy_conv(length, nseq, mats, kre, kim, order, data, data_rb0, data_cb0, zc, gate_rb0, gate_col0, bias, l,
             out_dtype):
    tn = min(512, HY_WIDTH)
    tk = min(512, length)
    ncb = HY_WIDTH // tn
    ntk = length // tk
    fspec = pl.BlockSpec((tk, length), lambda s, j, i: (i, 0))
    kspec = pl.BlockSpec((tk, tn), lambda s, j, i: (i, order * ncb + j))
    yspec = pl.BlockSpec((tk, tn), lambda s, j, i: (s * ntk + i, j))
    yre, yim = pl.pallas_call(
        _hy_fwd_kernel,
        grid=(nseq, ncb, ntk),
        in_specs=[fspec, fspec,
                  pl.BlockSpec((length, tn), lambda s, j, i: (data_rb0 + s, data_cb0 + j)),
                  kspec, kspec],
        out_specs=[yspec, yspec],
        out_shape=[jax.ShapeDtypeStruct((nseq * length, HY_WIDTH), BF16)] * 2,
        scratch_shapes=[pltpu.VMEM((length, tn), BF16)],
        compiler_params=_cparams(3, 2 * _nbytes((tk, length), BF16), _nbytes((length, tn), F32),
                                 2 * _nbytes((tk, tn), F32), 2 * _nbytes((tk, tn), BF16),
                                 extra=_nbytes((length, tn), BF16) + 6 * _nbytes((tk, tn), F32)),
        name="hy_fwd",
    )(mats["fc_hi"], mats["fs_hi"], data, kre, kim)

    yfull = pl.BlockSpec((length, tn), lambda s, j, i: (s, j))
    return pl.pallas_call(
        functools.partial(_hy_inv_kernel, inv_len=1.0 / length),
        grid=(nseq, ncb, ntk),
        in_specs=[fspec, fspec, yfull, yfull,
                  pl.BlockSpec((tk, tn), lambda s, j, i: ((data_rb0 + s) * ntk + i, data_cb0 + j)),
                  pl.BlockSpec((tk, tn), lambda s, j, i: ((gate_rb0 + s) * ntk + i, gate_col0 // tn + j)),
                  pl.BlockSpec((None, None, 1, tn), lambda s, j, i: (l, order, 0, j))],
        out_specs=pl.BlockSpec((tk, tn), lambda s, j, i: (s * ntk + i, j)),
        out_shape=jax.ShapeDtypeStruct((nseq * length, HY_WIDTH), out_dtype),
        compiler_params=_cparams(3, 2 * _nbytes((tk, length), BF16), 2 * _nbytes((length, tn), BF16),
                                 3 * _nbytes((tk, tn), F32), extra=4 * _nbytes((tk, tn), F32)),
        name="hy_inv",
    )(mats["fct_hi"], mats["fst_hi"], yre, yim, data, zc, bias)


def _hyena(cfg, consts, z, p, l):
    zc = _hy_dwconv(cfg, z, p["hy_conv_w"], p["hy_conv_b"], l)
    bias = p["hy_bias"].reshape(DEPTH, 2, 1, HY_WIDTH)
    w1p = jnp.pad(p["hy_w1"], ((0, 0), (0, LANES - HY_FEAT), (0, 0)))
    outs = []
    for length, nseq, row0 in ((cfg.lc, cfg.nc, 0), (cfg.ll, cfg.nl, cfg.rc)):
        mats = consts["dft"][length]
        hs, hd, hb0 = _hy_filter(length, consts["feats"][length], w1p, p["hy_b1"].reshape(DEPTH, 1, HY_HID),
                                 p["hy_w2"], p["hy_b2"].reshape(DEPTH, 1, HY_HID), p["hy_w3"],
                                 p["hy_freq"].reshape(DEPTH, 1, HY_HID),
                                 p["hy_decay"].reshape(DEPTH, 1, 4 * HY_WIDTH), l)
        kre, kim = _hy_spec(length, mats, hs, hd, hb0)
        rb0 = row0 // length
        y1 = _hy_conv(length, nseq, mats, kre, kim, 0, zc, rb0, 0, zc, rb0, HY_WIDTH, bias, l, F32)
        y2 = _hy_conv(length, nseq, mats, kre, kim, 1, y1, 0, 0, zc, rb0, 2 * HY_WIDTH, bias, l, BF16)
        outs.append(y2)
    return jnp.concatenate(outs, axis=0)


def _softmax_sink_pv(parts, sink):
    m = sink
    for s, _ in parts:
        m = jnp.maximum(m, jnp.max(s, axis=-1, keepdims=True))
    den = jnp.exp(sink - m)
    acc = None
    for s, v in parts:
        e = jnp.exp(s - m)
        den = den + jnp.sum(e, axis=-1, keepdims=True)
        pv = _dot(e.astype(BF16), v)
        acc = pv if acc is None else acc + pv
    return acc / den


def _group_sink(sink_ref, h):
    return jnp.concatenate([jnp.broadcast_to(sink_ref[hq:hq + 1, 0:1], (BLOCK, 1))
                            for hq in range(h * GQA_GROUP, (h + 1) * GQA_GROUP)], axis=0)


def _store_group(o_ref, h, o):
    for g in range(GQA_GROUP):
        hq = h * GQA_GROUP + g
        o_ref[:, hq * HEAD_DIM:(hq + 1) * HEAD_DIM] = o[g * BLOCK:(g + 1) * BLOCK].astype(BF16)


def _attn_ctx_kernel(q_ref, k_ref, v_ref, sink_ref, o_ref):
    scale = HEAD_DIM ** -0.5
    for h in range(N_KV_HEADS):
        hs = slice(h * HEAD_DIM, (h + 1) * HEAD_DIM)
        k = k_ref[:, hs].astype(BF16)
        v = v_ref[:, hs].astype(BF16)
        q = jnp.concatenate([q_ref[:, hq * HEAD_DIM:(hq + 1) * HEAD_DIM]
                             for hq in range(h * GQA_GROUP, (h + 1) * GQA_GROUP)], axis=0).astype(BF16)
        s = _dot_nt(q, k) * scale
        _store_group(o_ref, h, _softmax_sink_pv([(s, v)], _group_sink(sink_ref, h)))


def _attn_ctx(cfg, z, sink):
    nb = cfg.lc // BLOCK
    qw = N_Q_HEADS * HEAD_DIM
    return pl.pallas_call(
        _attn_ctx_kernel,
        grid=(cfg.nc, nb),
        in_specs=[pl.BlockSpec((BLOCK, qw), lambda s, i: (s * nb + i, Q_COL0 // qw)),
                  pl.BlockSpec((cfg.lc, KV_COLS), lambda s, i: (s, K_COL0 // KV_COLS)),
                  pl.BlockSpec((cfg.lc, KV_COLS), lambda s, i: (s, V_COL0 // KV_COLS)),
                  pl.BlockSpec((N_Q_HEADS, LANES), lambda s, i: (0, 0))],
        out_specs=pl.BlockSpec((BLOCK, qw), lambda s, i: (s * nb + i, 0)),
        out_shape=jax.ShapeDtypeStruct((cfg.rc, qw), BF16),
        compiler_params=_cparams(2, _nbytes((BLOCK, qw), F32), 2 * _nbytes((cfg.lc, KV_COLS), F32)),
        name="attn_ctx",
    )(z, z, z, sink)


def _rope(x, c, s):
    return x * c + pltpu.roll(x, HEAD_DIM // 2, 1) * s


def _attn_lat_kernel(q_ref, kp_ref, kc_ref, kn_ref, vp_ref, vc_ref, vn_ref, ck_ref, cv_ref,
                     cq, sq, cp, sp, cn, sn, sink_ref, o_ref, *, nb):
    i = pl.program_id(1)
    scale = HEAD_DIM ** -0.5
    shape = (GQA_GROUP * BLOCK, 3 * BLOCK)
    qi = lax.broadcasted_iota(jnp.int32, shape, 0) & (BLOCK - 1)
    kj = lax.broadcasted_iota(jnp.int32, shape, 1)
    ok = jnp.abs(kj - BLOCK - qi) <= WINDOW
    ok = ok & ((kj >= BLOCK) | (i > 0)) & ((kj < 2 * BLOCK) | (i < nb - 1))
    for h in range(N_KV_HEADS):
        hs = slice(h * HEAD_DIM, (h + 1) * HEAD_DIM)
        kw = jnp.concatenate([_rope(kp_ref[:, hs], cp[...], sp[...]),
                              _rope(kc_ref[:, hs], cq[...], sq[...]),
                              _rope(kn_ref[:, hs], cn[...], sn[...])], axis=0).astype(BF16)
        vw = jnp.concatenate([vp_ref[:, hs], vc_ref[:, hs], vn_ref[:, hs]], axis=0).astype(BF16)
        kctx = ck_ref[:, hs].astype(BF16)
        vctx = cv_ref[:, hs].astype(BF16)
        q = jnp.concatenate([_rope(q_ref[:, hq * HEAD_DIM:(hq + 1) * HEAD_DIM], cq[...], sq[...])
                             for hq in range(h * GQA_GROUP, (h + 1) * GQA_GROUP)], axis=0).astype(BF16)
        s1 = _dot_nt(q, kctx) * scale
        s2 = jnp.where(ok, _dot_nt(q, kw) * scale, NEG_INF)
        _store_group(o_ref, h, _softmax_sink_pv([(s1, vctx), (s2, vw)], _group_sink(sink_ref, h)))


def _attn_lat(cfg, consts, z, ck, cv, sink):
    nb = cfg.ll // BLOCK
    rb0 = cfg.rc // BLOCK
    qw = N_Q_HEADS * HEAD_DIM
    past = ck.shape[1]
    cos2, sin2 = consts["rope"]

    def blk(delta):
        return lambda b, i: rb0 + b * nb + jnp.clip(i + delta, 0, nb - 1)

    def zspec(width, col0, delta):
        rowf = blk(delta)
        return pl.BlockSpec((BLOCK, width), lambda b, i: (rowf(b, i), col0 // width))

    def tspec(delta):
        return pl.BlockSpec((BLOCK, HEAD_DIM), lambda b, i: (jnp.clip(i + delta, 0, nb - 1), 0))

    cspec = pl.BlockSpec((None, past, KV_COLS), lambda b, i: (b, 0, 0))
    return pl.pallas_call(
        functools.partial(_attn_lat_kernel, nb=nb),
        grid=(cfg.nl, nb),
        in_specs=[zspec(qw, Q_COL0, 0),
                  zspec(KV_COLS, K_COL0, -1), zspec(KV_COLS, K_COL0, 0), zspec(KV_COLS, K_COL0, 1),
                  zspec(KV_COLS, V_COL0, -1), zspec(KV_COLS, V_COL0, 0), zspec(KV_COLS, V_COL0, 1),
                  cspec, cspec,
                  tspec(0), tspec(0), tspec(-1), tspec(-1), tspec(1), tspec(1),
                  pl.BlockSpec((N_Q_HEADS, LANES), lambda b, i: (0, 0))],
        out_specs=pl.BlockSpec((BLOCK, qw), lambda b, i: (b * nb + i, 0)),
        out_shape=jax.ShapeDtypeStruct((cfg.rl, qw), BF16),
        compiler_params=_cparams(2, _nbytes((BLOCK, qw), F32), 6 * _nbytes((BLOCK, KV_COLS), F32),
                                 2 * _nbytes((past, KV_COLS), F32)),
        name="attn_lat",
    )(z, z, z, z, z, z, z, ck, cv, cos2, sin2, cos2, sin2, cos2, sin2, sink)


def _rope_tables(length):
    rows = length // GRID_W
    row = jnp.repeat(jnp.arange(rows, dtype=F32), GRID_W)
    col = jnp.tile(jnp.arange(GRID_W, dtype=F32), rows)
    n_freq = HEAD_DIM // 4
    inv = ROPE_BASE ** (-jnp.arange(n_freq, dtype=F32) / n_freq)
    ang = jnp.concatenate([row[:, None] * inv, col[:, None] * inv], axis=-1)
    cos, sin = jnp.cos(ang), jnp.sin(ang)
    return jnp.concatenate([cos, cos], axis=-1), jnp.concatenate([-sin, sin], axis=-1)


def _s5_operands(p):
    t_len = S5_CHUNK
    lr, li = p["s5_lam_re"].astype(F32), p["s5_lam_im"].astype(F32)
    n_layers = lr.shape[0]
    dt = jnp.exp(p["s5_log_dt"].astype(F32))[..., None]
    zr, zi = lr * dt, li * dt
    mag = jnp.exp(zr)
    ar, ai = mag * jnp.cos(zi), mag * jnp.sin(zi)
    den = lr * lr + li * li
    cr = ((ar - 1.0) * lr + ai * li) / den
    ci = (ai * lr - (ar - 1.0) * li) / den
    k = jnp.arange(t_len + 1, dtype=F32)[:, None, None, None, None]
    pm = jnp.exp(k * zr)
    pr, pi = pm * jnp.cos(k * zi), pm * jnp.sin(k * zi)

    bt_re = p["s5_b_re"].astype(F32).transpose(0, 1, 2, 4, 3)
    bt_im = p["s5_b_im"].astype(F32).transpose(0, 1, 2, 4, 3)
    bbr = cr[:, :, :, None, :] * bt_re - ci[:, :, :, None, :] * bt_im
    bbi = cr[:, :, :, None, :] * bt_im + ci[:, :, :, None, :] * bt_re
    pk_r, pk_i = pr[:t_len, :, :, :, None, :], pi[:t_len, :, :, :, None, :]
    ab = jnp.concatenate([pk_r * bbr - pk_i * bbi, pk_r * bbi + pk_i * bbr], axis=-1)
    ab = ab.reshape(t_len, n_layers, 2, D_BRANCH, 2 * S5_STATE).transpose(1, 2, 0, 3, 4)
    pf = jnp.concatenate([jnp.flip(ab[:, 0:1], axis=2), ab[:, 1:2]], axis=1)

    c_re, c_im = p["s5_c_re"].astype(F32), p["s5_c_im"].astype(F32)
    cmat = jnp.concatenate([c_re, -c_im], axis=-1).reshape(n_layers, 2, D_BRANCH, 2 * S5_STATE)
    ct_re = c_re.reshape(n_layers, 2, D_BRANCH, S5_STATE).transpose(0, 1, 3, 2)
    ct_im = c_im.reshape(n_layers, 2, D_BRANCH, S5_STATE).transpose(0, 1, 3, 2)
    pt_r = jnp.repeat(pr[1:].transpose(0, 1, 2, 4, 3), S5_CH, axis=-1)
    pt_i = jnp.repeat(pi[1:].transpose(0, 1, 2, 4, 3), S5_CH, axis=-1)
    q = jnp.concatenate([ct_re * pt_r - ct_im * pt_i, -(ct_re * pt_i + ct_im * pt_r)], axis=3)
    q = q.transpose(1, 2, 0, 3, 4)
    qf = jnp.concatenate([q[:, 0:1], jnp.flip(q[:, 1:2], axis=2)], axis=1)

    a_chunk = jnp.stack([jnp.concatenate([pr[t_len], pr[t_len]], axis=-1),
                         jnp.concatenate([-pi[t_len], pi[t_len]], axis=-1)], axis=2)
    return pf, qf, cmat, a_chunk


def _s5_ucat(z_ref):
    return jnp.concatenate([z_ref[:, t, :].astype(BF16) for t in range(S5_CHUNK)], axis=1)


def _octet_group(shape, axis):
    return lax.broadcasted_iota(jnp.int32, shape, axis) // S5_CH


def _s5_in_kernel(z_ref, pf_ref, x_ref, p_scr):
    gpo = S5_GROUPS // S5_OCTETS

    @pl.when(pl.program_id(1) == 0)
    def _():
        row_group = _octet_group((LANES, LANES), 0)
        for d in range(2):
            for t in range(S5_CHUNK):
                blk = pf_ref[d, t]
                for g in range(gpo):
                    p_scr[d, t * LANES:(t + 1) * LANES, g * LANES:(g + 1) * LANES] = (
                        jnp.where(row_group == g, blk, 0.0).astype(BF16))

    u = _s5_ucat(z_ref)
    for d in range(2):
        x_ref[d] = _dot(u, p_scr[d]).reshape(u.shape[0], gpo, 2 * S5_STATE)


def _s5_in(z3, pf, l, tr):
    r16 = z3.shape[0]
    gpo = S5_GROUPS // S5_OCTETS
    w = 2 * S5_STATE
    return pl.pallas_call(
        _s5_in_kernel,
        grid=(S5_OCTETS, r16 // tr),
        in_specs=[pl.BlockSpec((tr, S5_CHUNK, LANES), lambda o, i: (i, 0, S5_COL0 // LANES + o)),
                  pl.BlockSpec((None, 2, S5_CHUNK, LANES, w), lambda o, i: (l, 0, 0, o, 0))],
        out_specs=pl.BlockSpec((2, tr, gpo, w), lambda o, i: (0, i, o, 0)),
        out_shape=jax.ShapeDtypeStruct((2, r16, S5_GROUPS, w), F32),
        scratch_shapes=[pltpu.VMEM((2, S5_OCT_K, gpo * w), BF16)],
        compiler_params=_cparams(2, _nbytes((tr, S5_CHUNK, LANES), F32), _nbytes((2, S5_CHUNK, LANES, w), F32),
                                 _nbytes((2, tr, gpo, w), F32),
                                 extra=_nbytes((2, S5_OCT_K, gpo * w), BF16) + 4 * _nbytes((tr, S5_OCT_K), F32)),
        name="s5_in",
    )(z3, pf)


def _s5_scan_kernel(*refs, n_in, nsb, jb):
    a_ref, h0_ref = refs[0], refs[1]
    xf = refs[2:2 + n_in]
    xb = refs[2 + n_in:2 + 2 * n_in]
    sf = refs[2 + 2 * n_in:2 + 3 * n_in]
    sb = refs[2 + 3 * n_in:2 + 4 * n_in]
    fin_ref, st_ref = refs[2 + 4 * n_in], refs[3 + 4 * n_in]
    t = pl.program_id(0)

    @pl.when(t == 0)
    def _():
        st_ref[...] = h0_ref[...]

    half = S5_STATE

    def body(s, carry):
        jf = s
        jr = jb - 1 - s
        for r in range(n_in):
            for q in range(nsb):
                idx = r * nsb + q
                cur = st_ref[0, idx]
                sf[r][q * jb + jf] = cur
                st_ref[0, idx] = cur * a_ref[0, 0] + pltpu.roll(cur, half, 1) * a_ref[0, 1] + xf[r][q * jb + jf]
                cur = st_ref[1, idx]
                sb[r][q * jb + jr] = cur
                st_ref[1, idx] = cur * a_ref[1, 0] + pltpu.roll(cur, half, 1) * a_ref[1, 1] + xb[r][q * jb + jr]
        return carry

    lax.fori_loop(0, jb, body, 0)

    @pl.when(t == pl.num_programs(0) - 1)
    def _():
        fin_ref[...] = st_ref[...]


def _s5_scan(x, a_chunk, h0, l, row0, n_in, nsb, nj, jb):
    g = x.shape[2]
    w = x.shape[3]
    nblk = nj // jb
    nseq = n_in * nsb
    rows = nsb * jb
    base = row0 // rows

    def xspec(d, r):
        if d == 0:
            return pl.BlockSpec((None, rows, g, w), lambda t: (0, base + r * nblk + t, 0, 0))
        return pl.BlockSpec((None, rows, g, w), lambda t: (1, base + r * nblk + nblk - 1 - t, 0, 0))

    def sspec(d):
        if d == 0:
            return pl.BlockSpec((rows, g, w), lambda t: (t, 0, 0))
        return pl.BlockSpec((rows, g, w), lambda t: (nblk - 1 - t, 0, 0))

    st_spec = pl.BlockSpec((2, nseq, g, w), lambda t: (0, 0, 0, 0))
    res = pl.pallas_call(
        functools.partial(_s5_scan_kernel, n_in=n_in, nsb=nsb, jb=jb),
        grid=(nblk,),
        in_specs=[pl.BlockSpec((None, 2, 2, g, w), lambda t: (l, 0, 0, 0, 0)), st_spec]
                 + [xspec(0, r) for r in range(n_in)] + [xspec(1, r) for r in range(n_in)],
        out_specs=[sspec(0)] * n_in + [sspec(1)] * n_in + [st_spec],
        out_shape=[jax.ShapeDtypeStruct((nsb * nj, g, w), F32)] * (2 * n_in)
                  + [jax.ShapeDtypeStruct((2, nseq, g, w), F32)],
        scratch_shapes=[pltpu.VMEM((2, nseq, g, w), F32)],
        compiler_params=_cparams(1, 4 * n_in * _nbytes((rows, g, w), F32), 3 * _nbytes((2, nseq, g, w), F32)),
        name="s5_scan",
    )(a_chunk, h0, *([x] * (2 * n_in)))
    return res[:n_in], res[n_in:2 * n_in], res[2 * n_in]


def _s5_out_kernel(z_ref, pf_ref, cm_ref, qf_ref, d_ref, sf_ref, sb_ref, y_ref, m_scr, q_scr):
    gpo = S5_GROUPS // S5_OCTETS

    @pl.when(pl.program_id(1) == 0)
    def _():
        same_group = _octet_group((LANES, LANES), 0) == _octet_group((LANES, LANES), 1)

        def lag_op(d, k):
            ab = pf_ref[d, S5_CHUNK - 1 - k] if d == 0 else pf_ref[d, k]
            ah, al = _split(ab)
            ch, cl = _split(cm_ref[d])
            return jnp.where(same_group, _dot_nt(ah, ch) + (_dot_nt(ah, cl) + _dot_nt(al, ch)), 0.0)

        fwd = [lag_op(0, k) for k in range(S5_CHUNK)]
        bwd = [lag_op(1, k) for k in range(S5_CHUNK)]
        blocks = {0: (fwd[0] + bwd[0]).astype(BF16)}
        for k in range(1, S5_CHUNK):
            blocks[k] = fwd[k].astype(BF16)
            blocks[-k] = bwd[k].astype(BF16)
        for t in range(S5_CHUNK):
            for t2 in range(S5_CHUNK):
                m_scr[t * LANES:(t + 1) * LANES, t2 * LANES:(t2 + 1) * LANES] = blocks[t2 - t]
        lane_group = _octet_group((LANES, LANES), 1)
        for d in range(2):
            for t in range(S5_CHUNK):
                blk = qf_ref[d, t]
                for g in range(gpo):
                    q_scr[d, g * LANES:(g + 1) * LANES, t * LANES:(t + 1) * LANES] = (
                        jnp.where(lane_group == g, blk, 0.0).astype(BF16))

    u = _s5_ucat(z_ref)
    rows = u.shape[0]
    width = sf_ref.shape[1] * sf_ref.shape[2]
    y = _dot(u, m_scr[...])
    y += _dot(sf_ref[...].reshape(rows, width).astype(BF16), q_scr[0])
    y += _dot(sb_ref[...].reshape(rows, width).astype(BF16), q_scr[1])
    for t in range(S5_CHUNK):
        y_ref[:, t, :] = y[:, t * LANES:(t + 1) * LANES] + z_ref[:, t, :] * d_ref[...]


def _s5_out(z3, pf, cmat, qf, d_oct, sin_f, sin_b, l, tr):
    r16 = z3.shape[0]
    gpo = S5_GROUPS // S5_OCTETS
    w = 2 * S5_STATE
    sspec = pl.BlockSpec((tr, gpo, w), lambda o, i: (i, o, 0))
    return pl.pallas_call(
        _s5_out_kernel,
        grid=(S5_OCTETS, r16 // tr),
        in_specs=[pl.BlockSpec((tr, S5_CHUNK, LANES), lambda o, i: (i, 0, S5_COL0 // LANES + o)),
                  pl.BlockSpec((None, 2, S5_CHUNK, LANES, w), lambda o, i: (l, 0, 0, o, 0)),
                  pl.BlockSpec((None, 2, LANES, w), lambda o, i: (l, 0, o, 0)),
                  pl.BlockSpec((None, 2, S5_CHUNK, w, LANES), lambda o, i: (l, 0, 0, 0, o)),
                  pl.BlockSpec((None, None, 1, LANES), lambda o, i: (l, o, 0, 0)),
                  sspec, sspec],
        out_specs=pl.BlockSpec((tr, S5_CHUNK, LANES), lambda o, i: (i, 0, o)),
        out_shape=jax.ShapeDtypeStruct((r16, S5_CHUNK, D_BRANCH), F32),
        scratch_shapes=[pltpu.VMEM((S5_OCT_K, S5_OCT_K), BF16), pltpu.VMEM((2, gpo * w, S5_OCT_K), BF16)],
        compiler_params=_cparams(2, 2 * _nbytes((tr, S5_CHUNK, LANES), F32), 2 * _nbytes((2, S5_CHUNK, LANES, w), F32),
                                 2 * _nbytes((tr, gpo, w), F32),
                                 extra=_nbytes((S5_OCT_K, S5_OCT_K), BF16) + _nbytes((2, gpo * w, S5_OCT_K), BF16)
                                 + 4 * _nbytes((tr, S5_OCT_K), F32)),
        name="s5_out",
    )(z3, pf, cmat, qf, d_oct, sin_f, sin_b)


def _s5_glu_kernel(y_ref, w_ref, b_ref, o_ref, wb_ref):
    @pl.when(pl.program_id(0) == 0)
    def _():
        wb_ref[...] = w_ref[...].astype(BF16)

    y = _gelu(y_ref[...])
    o_ref[...] = (y * _sigmoid(_dot(y.astype(BF16), wb_ref[...]) + b_ref[...])).astype(BF16)


def _s5_glu(cfg, y, w, b, l):
    tm = min(512, cfg.ll)
    return pl.pallas_call(
        _s5_glu_kernel,
        grid=(cfg.r // tm,),
        in_specs=[pl.BlockSpec((tm, D_BRANCH), lambda i: (i, 0)),
                  pl.BlockSpec((None, D_BRANCH, D_BRANCH), lambda i: (l, 0, 0)),
                  pl.BlockSpec((None, 1, D_BRANCH), lambda i: (l, 0, 0))],
        out_specs=pl.BlockSpec((tm, D_BRANCH), lambda i: (i, 0)),
        out_shape=jax.ShapeDtypeStruct((cfg.r, D_BRANCH), BF16),
        scratch_shapes=[pltpu.VMEM((D_BRANCH, D_BRANCH), BF16)],
        compiler_params=_cparams(1, 2 * _nbytes((tm, D_BRANCH), F32), _nbytes((D_BRANCH, D_BRANCH), F32),
                                 extra=_nbytes((D_BRANCH, D_BRANCH), BF16) + 4 * _nbytes((tm, D_BRANCH), F32)),
        name="s5_glu",
    )(y, w, b.reshape(DEPTH, 1, D_BRANCH))


def _s5(cfg, s5m, z, p, h0_lat, l):
    pf, qf, cmat, a_chunk = s5m
    r16 = cfg.r // S5_CHUNK
    tr = min(256, cfg.ll // S5_CHUNK)
    z3 = z.reshape(r16, S5_CHUNK, N_IN)
    x = _s5_in(z3, pf, l, tr)
    njc, njl = cfg.lc // S5_CHUNK, cfg.ll // S5_CHUNK
    rc16 = cfg.rc // S5_CHUNK
    zero_h0 = jnp.zeros((2, cfg.nc, S5_GROUPS, 2 * S5_STATE), F32)
    sf_c, sb_c, fin_ctx = _s5_scan(x, a_chunk, zero_h0, l, 0, 1, cfg.nc, njc, njc)
    sf_l, sb_l, _ = _s5_scan(x, a_chunk, h0_lat, l, rc16, cfg.nl, 1, njl, min(32, njl))
    sin_f = jnp.concatenate(list(sf_c) + list(sf_l), axis=0)
    sin_b = jnp.concatenate(list(sb_c) + list(sb_l), axis=0)
    d_oct = p["s5_d"].reshape(DEPTH, S5_OCTETS, 1, LANES)
    y = _s5_out(z3, pf, cmat, qf, d_oct, sin_f, sin_b, l, tr).reshape(cfg.r, D_BRANCH)
    return _s5_glu(cfg, y, p["s5_glu_w"], p["s5_glu_b"], l), fin_ctx


def _forward(cfg, x_prompt, x_sample, c, cache_k, cache_v, state_ssm_re, state_ssm_im, c_ctx, p):
    d = D_MODEL
    x = jnp.concatenate([x_prompt.reshape(cfg.rc, d), x_sample.reshape(cfg.rl, d)], axis=0)
    cvec = jnp.concatenate([c_ctx[None], c, jnp.zeros((MOD_ROWS - cfg.nseg, d), F32)], axis=0)
    mod = _mod(cvec, p["w_mod"], p["b_mod"])[:, :cfg.nseg].reshape(DEPTH, cfg.nseg, 6, 1, d)
    mod = [[mod[l, :, i] for i in range(6)] for l in range(DEPTH)]
    norm_g = p["norm_g"]

    consts = {
        "dft": {},
        "feats": {n: _hy_feats(n) for n in {cfg.lc, cfg.ll}},
        "rope": _rope_tables(cfg.ll),
    }
    for n in {cfg.lc, cfg.ll}:
        fc, fs, fct, fst = _dft_tables(n)
        consts["dft"][n] = {"fc_hi": fc.astype(BF16), "fs_hi": fs.astype(BF16),
                            "fct_hi": fct.astype(BF16), "fst_hi": fst.astype(BF16)}
    s5m = _s5_operands(p)

    kv_shape = (cfg.nc, cfg.lc, N_KV_HEADS, HEAD_DIM)
    ks, vs, srs, sis = [], [], [], []
    h = _resid_norm(cfg, x, gpre=norm_g[0, 0], sc=mod[0][1], sh=mod[0][0])
    for l in range(DEPTH):
        z = _mm(h, p["w_in"], l, tm=cfg.tm, tn=512)
        ks.append(z[:cfg.rc, K_COL0:K_COL0 + KV_COLS].reshape(kv_shape))
        vs.append(z[:cfg.rc, V_COL0:V_COL0 + KV_COLS].reshape(kv_shape))

        y_hy = _hyena(cfg, consts, z, p, l)
        sink = jnp.broadcast_to(p["attn_sink"][l][:, None], (N_Q_HEADS, LANES))
        past = cache_k.shape[2]
        y_at = jnp.concatenate([
            _attn_ctx(cfg, z, sink),
            _attn_lat(cfg, consts, z, cache_k[:, l].reshape(cfg.nl, past, KV_COLS),
                      cache_v[:, l].reshape(cfg.nl, past, KV_COLS), sink)], axis=0)
        h0 = jnp.concatenate([state_ssm_re[:, l], state_ssm_im[:, l]], axis=-1).transpose(1, 0, 2, 3)
        y_s5, fin = _s5(cfg, s5m, z, p, h0, l)
        srs.append(fin[..., :S5_STATE].transpose(1, 0, 2, 3))
        sis.append(fin[..., S5_STATE:].transpose(1, 0, 2, 3))

        merged = _merge(cfg, z, (y_hy, y_at, y_s5),
                        (p["w_branch_hy"], p["w_branch_attn"], p["w_branch_s5"]), l)
        y = _mm(merged, p["w_out"], l, tm=cfg.tm, tn=512)
        x, h = _resid_norm(cfg, x, y, gate=mod[l][2], gpost=norm_g[l, 1],
                           gpre=norm_g[l, 2], sc=mod[l][4], sh=mod[l][3])
        u = _mm(h, p["ffn_w_up"], l, tm=cfg.tm, tn=512)
        act = _ffn_act(cfg, u, p["ffn_conv_w"], p["ffn_conv_b"], l)
        f = _mm(act, p["ffn_w_down"], l, tm=min(512, cfg.tm), tn=512, w_buffers=1)
        if l + 1 < DEPTH:
            x, h = _resid_norm(cfg, x, f, gate=mod[l][5], gpost=norm_g[l, 3],
                               gpre=norm_g[l + 1, 0], sc=mod[l + 1][1], sh=mod[l + 1][0])
        else:
            x = _resid_norm(cfg, x, f, gate=mod[l][5], gpost=norm_g[l, 3])

    return (x[:cfg.rc].reshape(cfg.nc, cfg.lc, d), x[cfg.rc:].reshape(cfg.nl, cfg.ll, d),
            jnp.stack(ks, axis=1), jnp.stack(vs, axis=1), jnp.stack(srs, axis=1), jnp.stack(sis, axis=1))


def kernel(x_prompt, x_sample, c, cache_k, cache_v, state_ssm_re, state_ssm_im, c_ctx, w_mod, b_mod, norm_g, w_in, hy_conv_w, hy_conv_b, hy_w1, hy_b1, hy_w2, hy_b2, hy_w3, hy_freq, hy_decay, hy_bias, attn_sink, s5_lam_re, s5_lam_im, s5_log_dt, s5_b_re, s5_b_im, s5_c_re, s5_c_im, s5_d, s5_glu_w, s5_glu_b, w_branch_hy, w_branch_attn, w_branch_s5, w_out, ffn_w_up, ffn_conv_w, ffn_conv_b, ffn_w_down):
    p = dict(w_mod=w_mod, b_mod=b_mod, norm_g=norm_g, w_in=w_in, hy_conv_w=hy_conv_w, hy_conv_b=hy_conv_b,
             hy_w1=hy_w1, hy_b1=hy_b1, hy_w2=hy_w2, hy_b2=hy_b2, hy_w3=hy_w3, hy_freq=hy_freq,
             hy_decay=hy_decay, hy_bias=hy_bias, attn_sink=attn_sink, s5_lam_re=s5_lam_re,
             s5_lam_im=s5_lam_im, s5_log_dt=s5_log_dt, s5_b_re=s5_b_re, s5_b_im=s5_b_im, s5_c_re=s5_c_re,
             s5_c_im=s5_c_im, s5_d=s5_d, s5_glu_w=s5_glu_w, s5_glu_b=s5_glu_b, w_branch_hy=w_branch_hy,
             w_branch_attn=w_branch_attn, w_branch_s5=w_branch_s5, w_out=w_out, ffn_w_up=ffn_w_up,
             ffn_conv_w=ffn_conv_w, ffn_conv_b=ffn_conv_b, ffn_w_down=ffn_w_down)
    cfg = _Cfg(x_prompt.shape[0], x_prompt.shape[1], x_sample.shape[0], x_sample.shape[1])
    return _forward(cfg, x_prompt, x_sample, c, cache_k, cache_v, state_ssm_re, state_ssm_im, c_ctx, p)
```

```python
import functools
import math

import numpy as np
import jax
import jax.numpy as jnp
from jax import lax
from jax.experimental import pallas as pl
from jax.experimental.pallas import tpu as pltpu

F32 = jnp.float32
BF16 = jnp.bfloat16

D_MODEL = 4096
DEPTH = 2
GRID_W = 64
D_BRANCH = D_MODEL // 4
HY_WIDTH = D_BRANCH
HY_BANDS = 16
HY_FEAT = 1 + 2 * HY_BANDS
HY_HID = 64
HEAD_DIM = 128
N_Q_HEADS = D_BRANCH // HEAD_DIM
N_KV_HEADS = 2
GQA_GROUP = N_Q_HEADS // N_KV_HEADS
WINDOW = 128
BLOCK = 128
ROPE_BASE = 10000.0
S5_CH = 16
S5_GROUPS = D_BRANCH // S5_CH
S5_STATE = 64
S5_CHUNK = 16
S5_OCTETS = 8
S5_OCT_K = S5_CHUNK * 128
D_FF = 2 * D_MODEL
EPS = 1e-6
NEG_INF = -1e30

HY_COLS = 3 * HY_WIDTH
Q_COL0 = HY_COLS
K_COL0 = Q_COL0 + N_Q_HEADS * HEAD_DIM
KV_COLS = N_KV_HEADS * HEAD_DIM
V_COL0 = K_COL0 + KV_COLS
S5_COL0 = V_COL0 + KV_COLS
GATE_COL0 = S5_COL0 + D_BRANCH
N_IN = GATE_COL0 + 3 * D_MODEL

V7X_VMEM_REQUEST_MAX = 60 * 1024 * 1024
LANES = 128
SUBLANES = 8
MOD_ROWS = 8


def _nbytes(shape, dtype):
    return math.prod(shape) * jnp.dtype(dtype).itemsize


def _cparams(n_grid, *block_bytes, extra=0):
    est = 2 * sum(block_bytes) + extra + (4 << 20)
    return pltpu.CompilerParams(
        dimension_semantics=("arbitrary",) * n_grid,
        vmem_limit_bytes=int(min(max(est, 16 << 20), V7X_VMEM_REQUEST_MAX)))


def _dot(a, b):
    return jnp.dot(a, b, preferred_element_type=F32)


def _dot_nt(a, b):
    return lax.dot_general(a, b, (((1,), (1,)), ((), ())), preferred_element_type=F32)


def _split(a):
    hi = a.astype(BF16)
    return hi, (a - hi.astype(F32)).astype(BF16)


def _dot3(a, b):
    ah, al = _split(a)
    bh, bl = _split(b)
    return _dot(ah, bh) + (_dot(ah, bl) + _dot(al, bh))


def _sigmoid(x):
    return 1.0 / (1.0 + jnp.exp(-x))


def _gelu(x):
    c = math.sqrt(2.0 / math.pi)
    return x * (0.5 + 0.5 * jnp.tanh(x * (c + (c * 0.044715) * (x * x))))


def _rms(x, g):
    return x * lax.rsqrt(jnp.mean(x * x, axis=-1, keepdims=True) + EPS) * g


class _Cfg:
    def __init__(self, nc, lc, nl, ll):
        self.nc, self.lc, self.nl, self.ll = nc, lc, nl, ll
        self.rc = nc * lc
        self.rl = nl * ll
        self.r = self.rc + self.rl
        self.nseg = 1 + nl
        assert self.rc % ll == 0 and ll % lc == 0 and lc % BLOCK == 0
        self.tm = min(1024, ll)
        self.rb = ll

    def seg_of_row(self, row0):
        return jnp.where(row0 >= self.rc, (row0 - self.rc) // self.ll + 1, 0)


def _mod_kernel(c_ref, w_ref, b_ref, o_ref):
    c = c_ref[...]
    s = (c * _sigmoid(c)).astype(BF16)
    o_ref[...] = _dot(s, w_ref[...].astype(BF16)) + b_ref[...]


def _mod(cvec, w_mod, b_mod):
    n = w_mod.shape[-1]
    tn = 512
    return pl.pallas_call(
        _mod_kernel,
        grid=(DEPTH, n // tn),
        in_specs=[pl.BlockSpec((MOD_ROWS, D_MODEL), lambda l, j: (0, 0)),
                  pl.BlockSpec((None, D_MODEL, tn), lambda l, j: (l, 0, j)),
                  pl.BlockSpec((None, 1, tn), lambda l, j: (l, 0, j))],
        out_specs=pl.BlockSpec((None, MOD_ROWS, tn), lambda l, j: (l, 0, j)),
        out_shape=jax.ShapeDtypeStruct((DEPTH, MOD_ROWS, n), F32),
        compiler_params=_cparams(2, _nbytes((D_MODEL, tn), F32), extra=_nbytes((D_MODEL, tn), BF16)),
        name="mod",
    )(cvec, w_mod, b_mod.reshape(DEPTH, 1, n))


def _resid_norm_kernel(*refs, has_y, has_h, split_in, split_out, n_ctx_tiles):
    it = iter(refs)
    in_ctx = pl.program_id(0) < n_ctx_tiles
    if split_in:
        xa_ref, xb_ref = next(it), next(it)
        x = jnp.where(in_ctx, xa_ref[...], xb_ref[...])
    else:
        x = next(it)[...]
    if has_y:
        y_ref, gate_ref, gpost_ref = next(it), next(it), next(it)
    if has_h:
        gpre_ref, sc_ref, sh_ref = next(it), next(it), next(it)
    if has_y:
        x = x + gate_ref[...] * _rms(y_ref[...], gpost_ref[...])
        if split_out:
            xa_out, xb_out = next(it), next(it)

            @pl.when(in_ctx)
            def _():
                xa_out[...] = x

            @pl.when(jnp.logical_not(in_ctx))
            def _():
                xb_out[...] = x
        else:
            xo_ref = next(it)
            xo_ref[...] = x
    if has_h:
        ho_ref = next(it)
        ho_ref[...] = (_rms(x, gpre_ref[...]) * (1.0 + sc_ref[...]) + sh_ref[...]).astype(BF16)


def _resid_norm(cfg, x, y=None, gate=None, gpost=None, gpre=None, sc=None, sh=None, split_out=False):
    has_y, has_h, split_in = y is not None, gpre is not None, isinstance(x, tuple)
    tm = 256
    nca = cfg.rc // tm
    row = pl.BlockSpec((tm, D_MODEL), lambda i: (i, 0))
    row_a = pl.BlockSpec((tm, D_MODEL), lambda i: (jnp.minimum(i, nca - 1), 0))
    row_b = pl.BlockSpec((tm, D_MODEL), lambda i: (jnp.maximum(i - nca, 0), 0))
    vec = pl.BlockSpec((1, D_MODEL), lambda i: (0, 0))
    seg = pl.BlockSpec((None, 1, D_MODEL), lambda i: (cfg.seg_of_row(i * tm), 0, 0))
    args, specs = (list(x), [row_a, row_b]) if split_in else ([x], [row])
    outs, ospecs = [], []
    if has_y:
        args += [y, gate, gpost.reshape(1, D_MODEL)]
        specs += [row, seg, vec]
        if split_out:
            outs += [jax.ShapeDtypeStruct((cfg.rc, D_MODEL), F32), jax.ShapeDtypeStruct((cfg.rl, D_MODEL), F32)]
            ospecs += [row_a, row_b]
        else:
            outs.append(jax.ShapeDtypeStruct((cfg.r, D_MODEL), F32))
            ospecs.append(row)
    if has_h:
        args += [gpre.reshape(1, D_MODEL), sc, sh]
        specs += [vec, seg, seg]
        outs.append(jax.ShapeDtypeStruct((cfg.r, D_MODEL), BF16))
        ospecs.append(row)
    res = pl.pallas_call(
        functools.partial(_resid_norm_kernel, has_y=has_y, has_h=has_h, split_in=split_in, split_out=split_out,
                          n_ctx_tiles=nca),
        grid=(cfg.r // tm,),
        in_specs=specs, out_specs=ospecs, out_shape=outs,
        compiler_params=_cparams(1, 5 * _nbytes((tm, D_MODEL), F32)),
        name="resid_norm",
    )(*args)
    return res if len(res) > 1 else res[0]


def _mm_kernel(x_ref, w_ref, o_ref, wb_ref):
    @pl.when(pl.program_id(1) == 0)
    def _():
        wb_ref[...] = w_ref[...].astype(BF16)

    o_ref[...] = _dot(x_ref[...], wb_ref[...]).astype(o_ref.dtype)


def _mm(x, w, l, *, tm, tn, out_dtype=F32, w_buffers=2):
    m, k = x.shape
    n = w.shape[-1]
    w_mode = {} if w_buffers == 2 else {"pipeline_mode": pl.Buffered(w_buffers)}
    return pl.pallas_call(
        _mm_kernel,
        grid=(n // tn, m // tm),
        in_specs=[pl.BlockSpec((tm, k), lambda j, i: (i, 0)),
                  pl.BlockSpec((None, k, tn), lambda j, i: (l, 0, j), **w_mode)],
        out_specs=pl.BlockSpec((tm, tn), lambda j, i: (i, j)),
        out_shape=jax.ShapeDtypeStruct((m, n), out_dtype),
        scratch_shapes=[pltpu.VMEM((k, tn), BF16)],
        compiler_params=_cparams(2, _nbytes((tm, k), BF16), _nbytes((tm, tn), out_dtype),
                                 extra=w_buffers * _nbytes((k, tn), F32) + _nbytes((k, tn), BF16)
                                 + _nbytes((tm, tn), F32)),
        name="mm",
    )(x, w)


def _merge_kernel(g0, g1, g2, y0, y1, y2, w0, w1, w2, o_ref, wb_ref):
    @pl.when(pl.program_id(1) == 0)
    def _():
        wb_ref[0] = w0[...].astype(BF16)
        wb_ref[1] = w1[...].astype(BF16)
        wb_ref[2] = w2[...].astype(BF16)

    acc = _sigmoid(g0[...]) * _dot(y0[...], wb_ref[0])
    acc += _sigmoid(g1[...]) * _dot(y1[...], wb_ref[1])
    acc += _sigmoid(g2[...]) * _dot(y2[...], wb_ref[2])
    o_ref[...] = acc.astype(o_ref.dtype)


def _merge(cfg, z, ys, ws, l):
    tm, tn = cfg.tm, 512
    gate_specs = [pl.BlockSpec((tm, tn), functools.partial(
        lambda j, i, b: (i, (GATE_COL0 + b * D_MODEL) // tn + j), b=b)) for b in range(3)]
    y_spec = pl.BlockSpec((tm, D_BRANCH), lambda j, i: (i, 0))
    w_spec = pl.BlockSpec((None, D_BRANCH, tn), lambda j, i: (l, 0, j))
    return pl.pallas_call(
        _merge_kernel,
        grid=(D_MODEL // tn, cfg.r // tm),
        in_specs=gate_specs + [y_spec] * 3 + [w_spec] * 3,
        out_specs=pl.BlockSpec((tm, tn), lambda j, i: (i, j)),
        out_shape=jax.ShapeDtypeStruct((cfg.r, D_MODEL), BF16),
        scratch_shapes=[pltpu.VMEM((3, D_BRANCH, tn), BF16)],
        compiler_params=_cparams(2, 3 * _nbytes((tm, tn), F32), 3 * _nbytes((tm, D_BRANCH), BF16),
                                 3 * _nbytes((D_BRANCH, tn), F32), _nbytes((tm, tn), BF16),
                                 extra=3 * _nbytes((D_BRANCH, tn), BF16) + 2 * _nbytes((tm, tn), F32)),
        name="merge",
    )(z, z, z, *ys, *ws)


def _block_seq_len(cfg):
    return jnp.where(pl.program_id(0) < cfg.rc // cfg.rb, cfg.lc, cfg.ll)


def _dwconv3_rows(x_ref, w_ref, b_ref, r0, sl):
    total = x_ref.shape[0]
    cur = x_ref[r0:r0 + BLOCK, :]
    sub = lax.broadcasted_iota(jnp.int32, (SUBLANES, cur.shape[1]), 0)
    xm = x_ref[r0 - 1:r0 - 1 + BLOCK, :] if r0 > 0 else pltpu.roll(cur, 1, 0)
    first = jnp.where(((r0 + sub) & (sl - 1)) == 0, 0.0, xm[:SUBLANES])
    xm = jnp.concatenate([first, xm[SUBLANES:]], axis=0)
    xp = x_ref[r0 + 1:r0 + 1 + BLOCK, :] if r0 + BLOCK < total else pltpu.roll(cur, BLOCK - 1, 0)
    last = jnp.where(((r0 + BLOCK - SUBLANES + sub) & (sl - 1)) == sl - 1, 0.0, xp[BLOCK - SUBLANES:])
    xp = jnp.concatenate([xp[:BLOCK - SUBLANES], last], axis=0)
    return xm * w_ref[0:1, :] + cur * w_ref[1:2, :] + xp * w_ref[2:3, :] + b_ref[...]


def _hy_dwconv_kernel(x_ref, w_ref, b_ref, o_ref, *, cfg):
    sl = _block_seq_len(cfg)
    for r0 in range(0, x_ref.shape[0], BLOCK):
        o_ref[r0:r0 + BLOCK, :] = _dwconv3_rows(x_ref, w_ref, b_ref, r0, sl)


def _hy_dwconv(cfg, z, w, b, l):
    tn = 256
    blk = _nbytes((cfg.rb, tn), F32)
    return pl.pallas_call(
        functools.partial(_hy_dwconv_kernel, cfg=cfg),
        grid=(cfg.r // cfg.rb, HY_COLS // tn),
        in_specs=[pl.BlockSpec((cfg.rb, tn), lambda r, j: (r, j)),
                  pl.BlockSpec((None, 3, tn), lambda r, j: (l, 0, j)),
                  pl.BlockSpec((None, 1, tn), lambda r, j: (l, 0, j))],
        out_specs=pl.BlockSpec((cfg.rb, tn), lambda r, j: (r, j)),
        out_shape=jax.ShapeDtypeStruct((cfg.r, HY_COLS), F32),
        compiler_params=_cparams(2, 2 * blk, extra=4 * blk),
        name="hy_dwconv",
    )(z, w, b.reshape(DEPTH, 1, HY_COLS))


def _ffn_act_kernel(a_ref, b_ref, wa_ref, wb_ref, ba_ref, bb_ref, o_ref, *, cfg):
    sl = _block_seq_len(cfg)
    for r0 in range(0, a_ref.shape[0], BLOCK):
        a = _dwconv3_rows(a_ref, wa_ref, ba_ref, r0, sl)
        b = _dwconv3_rows(b_ref, wb_ref, bb_ref, r0, sl)
        o_ref[r0:r0 + BLOCK, :] = (_gelu(a) * b).astype(BF16)


def _ffn_act(cfg, u, w, b, l):
    tn = 256
    nb = D_FF // tn
    blk = _nbytes((cfg.rb, tn), F32)
    b3 = b.reshape(DEPTH, 1, 2 * D_FF)
    return pl.pallas_call(
        functools.partial(_ffn_act_kernel, cfg=cfg),
        grid=(cfg.r // cfg.rb, nb),
        in_specs=[pl.BlockSpec((cfg.rb, tn), lambda r, j: (r, j)),
                  pl.BlockSpec((cfg.rb, tn), lambda r, j: (r, j + nb)),
                  pl.BlockSpec((None, 3, tn), lambda r, j: (l, 0, j)),
                  pl.BlockSpec((None, 3, tn), lambda r, j: (l, 0, j + nb)),
                  pl.BlockSpec((None, 1, tn), lambda r, j: (l, 0, j)),
                  pl.BlockSpec((None, 1, tn), lambda r, j: (l, 0, j + nb))],
        out_specs=pl.BlockSpec((cfg.rb, tn), lambda r, j: (r, j)),
        out_shape=jax.ShapeDtypeStruct((cfg.r, D_FF), BF16),
        compiler_params=_cparams(2, 3 * blk, extra=6 * blk),
        name="ffn_act",
    )(u, u, w, w, b3, b3)


def _dft_tables(length):
    period = 4 * length
    r = 1 << ((length.bit_length()) // 2)
    idx = np.arange(length, dtype=np.int64)

    def cs(m):
        ang = 2.0 * np.pi * (m % period).astype(np.float64) / period
        return jnp.asarray(np.cos(ang), F32), jnp.asarray(np.sin(ang), F32)

    def combine(ca, sa, cb, sb):
        c = ca[:, :, None] * cb[:, None, :] - sa[:, :, None] * sb[:, None, :]
        s = sa[:, :, None] * cb[:, None, :] + ca[:, :, None] * sb[:, None, :]
        return c.reshape(length, length), s.reshape(length, length)

    odd = 2 * idx + 1
    ca, sa = cs(odd[:, None] * (np.arange(length // r) * r)[None, :])
    cb, sb = cs(odd[:, None] * np.arange(r)[None, :])
    fc, fs = combine(ca, sa, cb, sb)
    ca, sa = cs(idx[:, None] * (2 * r * np.arange(length // r))[None, :])
    cb, sb = cs(idx[:, None] * (2 * np.arange(r) + 1)[None, :])
    fct, fst = combine(ca, sa, cb, sb)
    return fc, fs, fct, fst


def _hy_feats(length):
    pos = jnp.arange(length, dtype=F32)
    t = (pos / length)[:, None]
    bands = jnp.linspace(1e-4, HY_BANDS - 1, HY_BANDS, dtype=F32)
    wpos = 2.0 * math.pi * t * bands
    feats = jnp.concatenate([t, jnp.cos(wpos), -jnp.sin(wpos)], axis=-1)
    return jnp.pad(feats, ((0, 0), (0, LANES - HY_FEAT)))


def _hy_filter_kernel(f_ref, w1_ref, b1_ref, w2_ref, b2_ref, w3_ref, fr_ref, dec_ref, hs_ref, hd_ref, hb0_ref):
    feats = f_ref[...]
    fr = fr_ref[...]
    h = jnp.sin(fr * (_dot3(feats, w1_ref[...]) + b1_ref[...]))
    h = jnp.sin(fr * (_dot3(h, w2_ref[...]) + b2_ref[...]))
    h = _dot3(h, w3_ref[...]) * jnp.exp(-feats[:, 0:1] * jnp.abs(dec_ref[...]))
    half = 2 * HY_WIDTH
    hf, hb = h[:, :half], h[:, half:]
    hs_ref[...] = (hf + hb).astype(BF16)
    hd_ref[...] = (hb - hf).astype(BF16)

    @pl.when(pl.program_id(0) == 0)
    def _():
        hb0_ref[...] = jnp.broadcast_to(hb[0:1, :], hb0_ref.shape)


def _hy_filter(length, feats, w1p, b1, w2, b2, w3, freq, decay, l):
    tl = 256
    half = 2 * HY_WIDTH
    lsel = lambda i: (l, 0, 0)
    return pl.pallas_call(
        _hy_filter_kernel,
        grid=(length // tl,),
        in_specs=[pl.BlockSpec((tl, LANES), lambda i: (i, 0)),
                  pl.BlockSpec((None, LANES, HY_HID), lsel),
                  pl.BlockSpec((None, 1, HY_HID), lsel),
                  pl.BlockSpec((None, HY_HID, HY_HID), lsel),
                  pl.BlockSpec((None, 1, HY_HID), lsel),
                  pl.BlockSpec((None, HY_HID, 2 * half), lsel),
                  pl.BlockSpec((None, 1, HY_HID), lsel),
                  pl.BlockSpec((None, 1, 2 * half), lsel)],
        out_specs=[pl.BlockSpec((tl, half), lambda i: (i, 0)),
                   pl.BlockSpec((tl, half), lambda i: (i, 0)),
                   pl.BlockSpec((8, half), lambda i: (0, 0))],
        out_shape=[jax.ShapeDtypeStruct((length, half), BF16),
                   jax.ShapeDtypeStruct((length, half), BF16),
                   jax.ShapeDtypeStruct((8, half), F32)],
        compiler_params=_cparams(1, 2 * _nbytes((tl, half), F32), extra=6 * _nbytes((tl, 2 * half), F32)),
        name="hy_filter",
    )(feats, w1p, b1, w2, b2, w3, freq, decay)


def _hy_spec_kernel(fc, fs, hs_ref, hd_ref, hb0_ref, kre_ref, kim_ref):
    kre_ref[...] = _dot(fc[...], hs_ref[...]) - hb0_ref[0:1, :]
    kim_ref[...] = _dot(fs[...], hd_ref[...])


def _hy_spec(length, mats, hs, hd, hb0):
    tk = min(512, length)
    tn = 512
    half = 2 * HY_WIDTH
    fspec = pl.BlockSpec((tk, length), lambda j, i: (i, 0))
    hspec = pl.BlockSpec((length, tn), lambda j, i: (0, j))
    kspec = pl.BlockSpec((tk, tn), lambda j, i: (i, j))
    return pl.pallas_call(
        _hy_spec_kernel,
        grid=(half // tn, length // tk),
        in_specs=[fspec, fspec, hspec, hspec, pl.BlockSpec((8, tn), lambda j, i: (0, j))],
        out_specs=[kspec, kspec],
        out_shape=[jax.ShapeDtypeStruct((length, half), F32)] * 2,
        compiler_params=_cparams(2, 2 * _nbytes((tk, length), BF16), 2 * _nbytes((length, tn), BF16),
                                 2 * _nbytes((tk, tn), F32), extra=2 * _nbytes((tk, tn), F32)),
        name="hy_spec",
    )(mats["fc_hi"], mats["fs_hi"], hs, hd, hb0)


def _hy_fwd_kernel(fc, fs, u_ref, kre, kim, yre_ref, yim_ref, ub_ref):
    @pl.when(pl.program_id(2) == 0)
    def _():
        ub_ref[...] = u_ref[...].astype(BF16)

    ure = _dot(fc[...], ub_ref[...])
    uim = -_dot(fs[...], ub_ref[...])
    yre_ref[...] = (ure * kre[...] - uim * kim[...]).astype(BF16)
    yim_ref[...] = (ure * kim[...] + uim * kre[...]).astype(BF16)


def _hy_inv_kernel(fct, fst, yre, yim, u_ref, gate_ref, bias_ref, o_ref, *, inv_len):
    acc = _dot(fct[...], yre[...]) - _dot(fst[...], yim[...])
    o_ref[...] = (gate_ref[...] * (acc * inv_len + u_ref[...] * bias_ref[...])).astype(o_ref.dtype)


def _hy_conv(length, nseq, mats, kre, kim, order, data, data_rb0, data_cb0, zc, gate_rb0, gate_col0, bias, l,
             out_dtype):
    tn = min(512, HY_WIDTH)
    tk = min(512, length)
    ncb = HY_WIDTH // tn
    ntk = length // tk
    fspec = pl.BlockSpec((tk, length), lambda s, j, i: (i, 0))
    kspec = pl.BlockSpec((tk, tn), lambda s, j, i: (i, order * ncb + j))
    yspec = pl.BlockSpec((tk, tn), lambda s, j, i: (s * ntk + i, j))
    yre, yim = pl.pallas_call(
        _hy_fwd_kernel,
        grid=(nseq, ncb, ntk),
        in_specs=[fspec, fspec,
                  pl.BlockSpec((length, tn), lambda s, j, i: (data_rb0 + s, data_cb0 + j)),
                  kspec, kspec],
        out_specs=[yspec, yspec],
        out_shape=[jax.ShapeDtypeStruct((nseq * length, HY_WIDTH), BF16)] * 2,
        scratch_shapes=[pltpu.VMEM((length, tn), BF16)],
        compiler_params=_cparams(3, 2 * _nbytes((tk, length), BF16), _nbytes((length, tn), F32),
                                 2 * _nbytes((tk, tn), F32), 2 * _nbytes((tk, tn), BF16),
                                 extra=_nbytes((length, tn), BF16) + 6 * _nbytes((tk, tn), F32)),
        name="hy_fwd",
    )(mats["fc_hi"], mats["fs_hi"], data, kre, kim)

    yfull = pl.BlockSpec((length, tn), lambda s, j, i: (s, j))
    return pl.pallas_call(
        functools.partial(_hy_inv_kernel, inv_len=1.0 / length),
        grid=(nseq, ncb, ntk),
        in_specs=[fspec, fspec, yfull, yfull,
                  pl.BlockSpec((tk, tn), lambda s, j, i: ((data_rb0 + s) * ntk + i, data_cb0 + j)),
                  pl.BlockSpec((tk, tn), lambda s, j, i: ((gate_rb0 + s) * ntk + i, gate_col0 // tn + j)),
                  pl.BlockSpec((None, None, 1, tn), lambda s, j, i: (l, order, 0, j))],
        out_specs=pl.BlockSpec((tk, tn), lambda s, j, i: (s * ntk + i, j)),
        out_shape=jax.ShapeDtypeStruct((nseq * length, HY_WIDTH), out_dtype),
        compiler_params=_cparams(3, 2 * _nbytes((tk, length), BF16), 2 * _nbytes((length, tn), BF16),
                                 3 * _nbytes((tk, tn), F32), extra=4 * _nbytes((tk, tn), F32)),
        name="hy_inv",
    )(mats["fct_hi"], mats["fst_hi"], yre, yim, data, zc, bias)


def _hyena(cfg, consts, z, p, l):
    zc = _hy_dwconv(cfg, z, p["hy_conv_w"], p["hy_conv_b"], l)
    bias = p["hy_bias"].reshape(DEPTH, 2, 1, HY_WIDTH)
    w1p = jnp.pad(p["hy_w1"], ((0, 0), (0, LANES - HY_FEAT), (0, 0)))
    outs = []
    for length, nseq, row0 in ((cfg.lc, cfg.nc, 0), (cfg.ll, cfg.nl, cfg.rc)):
        mats = consts["dft"][length]
        hs, hd, hb0 = _hy_filter(length, consts["feats"][length], w1p, p["hy_b1"].reshape(DEPTH, 1, HY_HID),
                                 p["hy_w2"], p["hy_b2"].reshape(DEPTH, 1, HY_HID), p["hy_w3"],
                                 p["hy_freq"].reshape(DEPTH, 1, HY_HID),
                                 p["hy_decay"].reshape(DEPTH, 1, 4 * HY_WIDTH), l)
        kre, kim = _hy_spec(length, mats, hs, hd, hb0)
        rb0 = row0 // length
        y1 = _hy_conv(length, nseq, mats, kre, kim, 0, zc, rb0, 0, zc, rb0, HY_WIDTH, bias, l, F32)
        y2 = _hy_conv(length, nseq, mats, kre, kim, 1, y1, 0, 0, zc, rb0, 2 * HY_WIDTH, bias, l, BF16)
        outs.append(y2)
    return jnp.concatenate(outs, axis=0)


def _softmax_sink_pv(parts, sink):
    m = sink
    for s, _ in parts:
        m = jnp.maximum(m, jnp.max(s, axis=-1, keepdims=True))
    den = jnp.exp(sink - m)
    acc = None
    for s, v in parts:
        e = jnp.exp(s - m)
        den = den + jnp.sum(e, axis=-1, keepdims=True)
        pv = _dot(e.astype(BF16), v)
        acc = pv if acc is None else acc + pv
    return acc / den


def _group_sink(sink_ref, h):
    return jnp.concatenate([jnp.broadcast_to(sink_ref[hq:hq + 1, 0:1], (BLOCK, 1))
                            for hq in range(h * GQA_GROUP, (h + 1) * GQA_GROUP)], axis=0)


def _store_group(o_ref, h, o):
    for g in range(GQA_GROUP):
        hq = h * GQA_GROUP + g
        o_ref[:, hq * HEAD_DIM:(hq + 1) * HEAD_DIM] = o[g * BLOCK:(g + 1) * BLOCK].astype(BF16)


def _attn_ctx_kernel(q_ref, k_ref, v_ref, sink_ref, o_ref):
    scale = HEAD_DIM ** -0.5
    for h in range(N_KV_HEADS):
        hs = slice(h * HEAD_DIM, (h + 1) * HEAD_DIM)
        k = k_ref[:, hs].astype(BF16)
        v = v_ref[:, hs].astype(BF16)
        q = jnp.concatenate([q_ref[:, hq * HEAD_DIM:(hq + 1) * HEAD_DIM]
                             for hq in range(h * GQA_GROUP, (h + 1) * GQA_GROUP)], axis=0).astype(BF16)
        s = _dot_nt(q, k) * scale
        _store_group(o_ref, h, _softmax_sink_pv([(s, v)], _group_sink(sink_ref, h)))


def _attn_ctx(cfg, z, sink):
    nb = cfg.lc // BLOCK
    qw = N_Q_HEADS * HEAD_DIM
    return pl.pallas_call(
        _attn_ctx_kernel,
        grid=(cfg.nc, nb),
        in_specs=[pl.BlockSpec((BLOCK, qw), lambda s, i: (s * nb + i, Q_COL0 // qw)),
                  pl.BlockSpec((cfg.lc, KV_COLS), lambda s, i: (s, K_COL0 // KV_COLS)),
                  pl.BlockSpec((cfg.lc, KV_COLS), lambda s, i: (s, V_COL0 // KV_COLS)),
                  pl.BlockSpec((N_Q_HEADS, LANES), lambda s, i: (0, 0))],
        out_specs=pl.BlockSpec((BLOCK, qw), lambda s, i: (s * nb + i, 0)),
        out_shape=jax.ShapeDtypeStruct((cfg.rc, qw), BF16),
        compiler_params=_cparams(2, _nbytes((BLOCK, qw), F32), 2 * _nbytes((cfg.lc, KV_COLS), F32)),
        name="attn_ctx",
    )(z, z, z, sink)


def _rope(x, c, s):
    return x * c + pltpu.roll(x, HEAD_DIM // 2, 1) * s


def _attn_lat_kernel(q_ref, kp_ref, kc_ref, kn_ref, vp_ref, vc_ref, vn_ref, ck_ref, cv_ref,
                     cq, sq, cp, sp, cn, sn, sink_ref, o_ref, *, nb):
    i = pl.program_id(1)
    scale = HEAD_DIM ** -0.5
    shape = (GQA_GROUP * BLOCK, 3 * BLOCK)
    qi = lax.broadcasted_iota(jnp.int32, shape, 0) & (BLOCK - 1)
    kj = lax.broadcasted_iota(jnp.int32, shape, 1)
    ok = jnp.abs(kj - BLOCK - qi) <= WINDOW
    ok = ok & ((kj >= BLOCK) | (i > 0)) & ((kj < 2 * BLOCK) | (i < nb - 1))
    for h in range(N_KV_HEADS):
        hs = slice(h * HEAD_DIM, (h + 1) * HEAD_DIM)
        kw = jnp.concatenate([_rope(kp_ref[:, hs], cp[...], sp[...]),
                              _rope(kc_ref[:, hs], cq[...], sq[...]),
                              _rope(kn_ref[:, hs], cn[...], sn[...])], axis=0).astype(BF16)
        vw = jnp.concatenate([vp_ref[:, hs], vc_ref[:, hs], vn_ref[:, hs]], axis=0).astype(BF16)
        kctx = ck_ref[:, hs].astype(BF16)
        vctx = cv_ref[:, hs].astype(BF16)
        q = jnp.concatenate([_rope(q_ref[:, hq * HEAD_DIM:(hq + 1) * HEAD_DIM], cq[...], sq[...])
                             for hq in range(h * GQA_GROUP, (h + 1) * GQA_GROUP)], axis=0).astype(BF16)
        s1 = _dot_nt(q, kctx) * scale
        s2 = jnp.where(ok, _dot_nt(q, kw) * scale, NEG_INF)
        _store_group(o_ref, h, _softmax_sink_pv([(s1, vctx), (s2, vw)], _group_sink(sink_ref, h)))


def _attn_lat(cfg, consts, z, ck, cv, sink):
    nb = cfg.ll // BLOCK
    rb0 = cfg.rc // BLOCK
    qw = N_Q_HEADS * HEAD_DIM
    past = ck.shape[1]
    cos2, sin2 = consts["rope"]

    def blk(delta):
        return lambda b, i: rb0 + b * nb + jnp.clip(i + delta, 0, nb - 1)

    def zspec(width, col0, delta):
        rowf = blk(delta)
        return pl.BlockSpec((BLOCK, width), lambda b, i: (rowf(b, i), col0 // width))

    def tspec(delta):
        return pl.BlockSpec((BLOCK, HEAD_DIM), lambda b, i: (jnp.clip(i + delta, 0, nb - 1), 0))

    cspec = pl.BlockSpec((None, past, KV_COLS), lambda b, i: (b, 0, 0))
    return pl.pallas_call(
        functools.partial(_attn_lat_kernel, nb=nb),
        grid=(cfg.nl, nb),
        in_specs=[zspec(qw, Q_COL0, 0),
                  zspec(KV_COLS, K_COL0, -1), zspec(KV_COLS, K_COL0, 0), zspec(KV_COLS, K_COL0, 1),
                  zspec(KV_COLS, V_COL0, -1), zspec(KV_COLS, V_COL0, 0), zspec(KV_COLS, V_COL0, 1),
                  cspec, cspec,
                  tspec(0), tspec(0), tspec(-1), tspec(-1), tspec(1), tspec(1),
                  pl.BlockSpec((N_Q_HEADS, LANES), lambda b, i: (0, 0))],
        out_specs=pl.BlockSpec((BLOCK, qw), lambda b, i: (b * nb + i, 0)),
        out_shape=jax.ShapeDtypeStruct((cfg.rl, qw), BF16),
        compiler_params=_cparams(2, _nbytes((BLOCK, qw), F32), 6 * _nbytes((BLOCK, KV_COLS), F32),
                                 2 * _nbytes((past, KV_COLS), F32)),
        name="attn_lat",
    )(z, z, z, z, z, z, z, ck, cv, cos2, sin2, cos2, sin2, cos2, sin2, sink)


def _rope_tables(length):
    rows = length // GRID_W
    row = jnp.repeat(jnp.arange(rows, dtype=F32), GRID_W)
    col = jnp.tile(jnp.arange(GRID_W, dtype=F32), rows)
    n_freq = HEAD_DIM // 4
    inv = ROPE_BASE ** (-jnp.arange(n_freq, dtype=F32) / n_freq)
    ang = jnp.concatenate([row[:, None] * inv, col[:, None] * inv], axis=-1)
    cos, sin = jnp.cos(ang), jnp.sin(ang)
    return jnp.concatenate([cos, cos], axis=-1), jnp.concatenate([-sin, sin], axis=-1)


def _s5_operands(p):
    t_len = S5_CHUNK
    lr, li = p["s5_lam_re"].astype(F32), p["s5_lam_im"].astype(F32)
    n_layers = lr.shape[0]
    dt = jnp.exp(p["s5_log_dt"].astype(F32))[..., None]
    zr, zi = lr * dt, li * dt
    mag = jnp.exp(zr)
    ar, ai = mag * jnp.cos(zi), mag * jnp.sin(zi)
    den = lr * lr + li * li
    cr = ((ar - 1.0) * lr + ai * li) / den
    ci = (ai * lr - (ar - 1.0) * li) / den
    k = jnp.arange(t_len + 1, dtype=F32)[:, None, None, None, None]
    pm = jnp.exp(k * zr)
    pr, pi = pm * jnp.cos(k * zi), pm * jnp.sin(k * zi)

    bt_re = p["s5_b_re"].astype(F32).transpose(0, 1, 2, 4, 3)
    bt_im = p["s5_b_im"].astype(F32).transpose(0, 1, 2, 4, 3)
    bbr = cr[:, :, :, None, :] * bt_re - ci[:, :, :, None, :] * bt_im
    bbi = cr[:, :, :, None, :] * bt_im + ci[:, :, :, None, :] * bt_re
    pk_r, pk_i = pr[:t_len, :, :, :, None, :], pi[:t_len, :, :, :, None, :]
    ab = jnp.concatenate([pk_r * bbr - pk_i * bbi, pk_r * bbi + pk_i * bbr], axis=-1)
    ab = ab.reshape(t_len, n_layers, 2, D_BRANCH, 2 * S5_STATE).transpose(1, 2, 0, 3, 4)
    pf = jnp.concatenate([jnp.flip(ab[:, 0:1], axis=2), ab[:, 1:2]], axis=1)

    c_re, c_im = p["s5_c_re"].astype(F32), p["s5_c_im"].astype(F32)
    cmat = jnp.concatenate([c_re, -c_im], axis=-1).reshape(n_layers, 2, D_BRANCH, 2 * S5_STATE)
    ct_re = c_re.reshape(n_layers, 2, D_BRANCH, S5_STATE).transpose(0, 1, 3, 2)
    ct_im = c_im.reshape(n_layers, 2, D_BRANCH, S5_STATE).transpose(0, 1, 3, 2)
    pt_r = jnp.repeat(pr[1:].transpose(0, 1, 2, 4, 3), S5_CH, axis=-1)
    pt_i = jnp.repeat(pi[1:].transpose(0, 1, 2, 4, 3), S5_CH, axis=-1)
    q = jnp.concatenate([ct_re * pt_r - ct_im * pt_i, -(ct_re * pt_i + ct_im * pt_r)], axis=3)
    q = q.transpose(1, 2, 0, 3, 4)
    qf = jnp.concatenate([q[:, 0:1], jnp.flip(q[:, 1:2], axis=2)], axis=1)

    a_chunk = jnp.stack([jnp.concatenate([pr[t_len], pr[t_len]], axis=-1),
                         jnp.concatenate([-pi[t_len], pi[t_len]], axis=-1)], axis=2)
    return pf, qf, cmat, a_chunk


def _s5_ucat(z_ref):
    return jnp.concatenate([z_ref[:, t, :].astype(BF16) for t in range(S5_CHUNK)], axis=1)


def _octet_group(shape, axis):
    return lax.broadcasted_iota(jnp.int32, shape, axis) // S5_CH


def _s5_in_kernel(z_ref, pf_ref, x_ref, p_scr):
    gpo = S5_GROUPS // S5_OCTETS

    @pl.when(pl.program_id(1) == 0)
    def _():
        row_group = _octet_group((LANES, LANES), 0)
        for d in range(2):
            for t in range(S5_CHUNK):
                blk = pf_ref[d, t]
                for g in range(gpo):
                    p_scr[d, t * LANES:(t + 1) * LANES, g * LANES:(g + 1) * LANES] = (
                        jnp.where(row_group == g, blk, 0.0).astype(BF16))

    u = _s5_ucat(z_ref)
    for d in range(2):
        x_ref[d] = _dot(u, p_scr[d]).reshape(u.shape[0], gpo, 2 * S5_STATE)


def _s5_in(z3, pf, l, tr):
    r16 = z3.shape[0]
    gpo = S5_GROUPS // S5_OCTETS
    w = 2 * S5_STATE
    return pl.pallas_call(
        _s5_in_kernel,
        grid=(S5_OCTETS, r16 // tr),
        in_specs=[pl.BlockSpec((tr, S5_CHUNK, LANES), lambda o, i: (i, 0, S5_COL0 // LANES + o)),
                  pl.BlockSpec((None, 2, S5_CHUNK, LANES, w), lambda o, i: (l, 0, 0, o, 0))],
        out_specs=pl.BlockSpec((2, tr, gpo, w), lambda o, i: (0, i, o, 0)),
        out_shape=jax.ShapeDtypeStruct((2, r16, S5_GROUPS, w), F32),
        scratch_shapes=[pltpu.VMEM((2, S5_OCT_K, gpo * w), BF16)],
        compiler_params=_cparams(2, _nbytes((tr, S5_CHUNK, LANES), F32), _nbytes((2, S5_CHUNK, LANES, w), F32),
                                 _nbytes((2, tr, gpo, w), F32),
                                 extra=_nbytes((2, S5_OCT_K, gpo * w), BF16) + 4 * _nbytes((tr, S5_OCT_K), F32)),
        name="s5_in",
    )(z3, pf)


def _s5_scan_kernel(*refs, n_in, nsb, jb):
    a_ref, h0_ref = refs[0], refs[1]
    xf = refs[2:2 + n_in]
    xb = refs[2 + n_in:2 + 2 * n_in]
    sf = refs[2 + 2 * n_in:2 + 3 * n_in]
    sb = refs[2 + 3 * n_in:2 + 4 * n_in]
    fin_ref, st_ref = refs[2 + 4 * n_in], refs[3 + 4 * n_in]
    t = pl.program_id(0)

    @pl.when(t == 0)
    def _():
        st_ref[...] = h0_ref[...]

    half = S5_STATE

    def body(s, carry):
        jf = s
        jr = jb - 1 - s
        for r in range(n_in):
            for q in range(nsb):
                idx = r * nsb + q
                cur = st_ref[0, idx]
                sf[r][q * jb + jf] = cur
                st_ref[0, idx] = cur * a_ref[0, 0] + pltpu.roll(cur, half, 1) * a_ref[0, 1] + xf[r][q * jb + jf]
                cur = st_ref[1, idx]
                sb[r][q * jb + jr] = cur
                st_ref[1, idx] = cur * a_ref[1, 0] + pltpu.roll(cur, half, 1) * a_ref[1, 1] + xb[r][q * jb + jr]
        return carry

    lax.fori_loop(0, jb, body, 0)

    @pl.when(t == pl.num_programs(0) - 1)
    def _():
        fin_ref[...] = st_ref[...]


def _s5_scan(x, a_chunk, h0, l, row0, n_in, nsb, nj, jb):
    g = x.shape[2]
    w = x.shape[3]
    nblk = nj // jb
    nseq = n_in * nsb
    rows = nsb * jb
    base = row0 // rows

    def xspec(d, r):
        if d == 0:
            return pl.BlockSpec((None, rows, g, w), lambda t: (0, base + r * nblk + t, 0, 0))
        return pl.BlockSpec((None, rows, g, w), lambda t: (1, base + r * nblk + nblk - 1 - t, 0, 0))

    def sspec(d):
        if d == 0:
            return pl.BlockSpec((rows, g, w), lambda t: (t, 0, 0))
        return pl.BlockSpec((rows, g, w), lambda t: (nblk - 1 - t, 0, 0))

    st_spec = pl.BlockSpec((2, nseq, g, w), lambda t: (0, 0, 0, 0))
    res = pl.pallas_call(
        functools.partial(_s5_scan_kernel, n_in=n_in, nsb=nsb, jb=jb),
        grid=(nblk,),
        in_specs=[pl.BlockSpec((None, 2, 2, g, w), lambda t: (l, 0, 0, 0, 0)), st_spec]
                 + [xspec(0, r) for r in range(n_in)] + [xspec(1, r) for r in range(n_in)],
        out_specs=[sspec(0)] * n_in + [sspec(1)] * n_in + [st_spec],
        out_shape=[jax.ShapeDtypeStruct((nsb * nj, g, w), F32)] * (2 * n_in)
                  + [jax.ShapeDtypeStruct((2, nseq, g, w), F32)],
        scratch_shapes=[pltpu.VMEM((2, nseq, g, w), F32)],
        compiler_params=_cparams(1, 4 * n_in * _nbytes((rows, g, w), F32), 3 * _nbytes((2, nseq, g, w), F32)),
        name="s5_scan",
    )(a_chunk, h0, *([x] * (2 * n_in)))
    return res[:n_in], res[n_in:2 * n_in], res[2 * n_in]


def _s5_out_kernel(z_ref, pf_ref, cm_ref, qf_ref, d_ref, *rest, starts):
    n = len(starts)
    sf_refs, sb_refs = rest[:n], rest[n:2 * n]
    y_ref, m_scr, q_scr = rest[2 * n:]
    gpo = S5_GROUPS // S5_OCTETS

    @pl.when(pl.program_id(1) == 0)
    def _():
        same_group = _octet_group((LANES, LANES), 0) == _octet_group((LANES, LANES), 1)

        def lag_op(d, k):
            ab = pf_ref[d, S5_CHUNK - 1 - k] if d == 0 else pf_ref[d, k]
            ah, al = _split(ab)
            ch, cl = _split(cm_ref[d])
            return jnp.where(same_group, _dot_nt(ah, ch) + (_dot_nt(ah, cl) + _dot_nt(al, ch)), 0.0)

        fwd = [lag_op(0, k) for k in range(S5_CHUNK)]
        bwd = [lag_op(1, k) for k in range(S5_CHUNK)]
        blocks = {0: (fwd[0] + bwd[0]).astype(BF16)}
        for k in range(1, S5_CHUNK):
            blocks[k] = fwd[k].astype(BF16)
            blocks[-k] = bwd[k].astype(BF16)
        for t in range(S5_CHUNK):
            for t2 in range(S5_CHUNK):
                m_scr[t * LANES:(t + 1) * LANES, t2 * LANES:(t2 + 1) * LANES] = blocks[t2 - t]
        lane_group = _octet_group((LANES, LANES), 1)
        for d in range(2):
            for t in range(S5_CHUNK):
                blk = qf_ref[d, t]
                for g in range(gpo):
                    q_scr[d, g * LANES:(g + 1) * LANES, t * LANES:(t + 1) * LANES] = (
                        jnp.where(lane_group == g, blk, 0.0).astype(BF16))

    u = _s5_ucat(z_ref)
    rows = u.shape[0]
    width = gpo * 2 * S5_STATE

    def entering_state(refs):
        s = refs[0][...]
        for ref, start in zip(refs[1:], starts[1:]):
            s = jnp.where(pl.program_id(1) >= start, ref[...], s)
        return s.reshape(rows, width).astype(BF16)

    y = _dot(u, m_scr[...])
    y += _dot(entering_state(sf_refs), q_scr[0])
    y += _dot(entering_state(sb_refs), q_scr[1])
    for t in range(S5_CHUNK):
        y_ref[:, t, :] = y[:, t * LANES:(t + 1) * LANES] + z_ref[:, t, :] * d_ref[...]


def _s5_out(z3, pf, cmat, qf, d_oct, sin_f, sin_b, l, tr):
    r16 = z3.shape[0]
    gpo = S5_GROUPS // S5_OCTETS
    w = 2 * S5_STATE
    tiles = [s.shape[0] // tr for s in sin_f]
    starts = tuple(sum(tiles[:n]) for n in range(len(tiles)))
    sspecs = [pl.BlockSpec((tr, gpo, w), functools.partial(
        lambda o, i, start, count: (jnp.clip(i - start, 0, count - 1), o, 0), start=start, count=count))
        for start, count in zip(starts, tiles)]
    return pl.pallas_call(
        functools.partial(_s5_out_kernel, starts=starts),
        grid=(S5_OCTETS, r16 // tr),
        in_specs=[pl.BlockSpec((tr, S5_CHUNK, LANES), lambda o, i: (i, 0, S5_COL0 // LANES + o)),
                  pl.BlockSpec((None, 2, S5_CHUNK, LANES, w), lambda o, i: (l, 0, 0, o, 0)),
                  pl.BlockSpec((None, 2, LANES, w), lambda o, i: (l, 0, o, 0)),
                  pl.BlockSpec((None, 2, S5_CHUNK, w, LANES), lambda o, i: (l, 0, 0, 0, o)),
                  pl.BlockSpec((None, None, 1, LANES), lambda o, i: (l, o, 0, 0)),
                  *sspecs, *sspecs],
        out_specs=pl.BlockSpec((tr, S5_CHUNK, LANES), lambda o, i: (i, 0, o)),
        out_shape=jax.ShapeDtypeStruct((r16, S5_CHUNK, D_BRANCH), F32),
        scratch_shapes=[pltpu.VMEM((S5_OCT_K, S5_OCT_K), BF16), pltpu.VMEM((2, gpo * w, S5_OCT_K), BF16)],
        compiler_params=_cparams(2, 2 * _nbytes((tr, S5_CHUNK, LANES), F32), 2 * _nbytes((2, S5_CHUNK, LANES, w), F32),
                                 2 * len(tiles) * _nbytes((tr, gpo, w), F32),
                                 extra=_nbytes((S5_OCT_K, S5_OCT_K), BF16) + _nbytes((2, gpo * w, S5_OCT_K), BF16)
                                 + 4 * _nbytes((tr, S5_OCT_K), F32)),
        name="s5_out",
    )(z3, pf, cmat, qf, d_oct, *sin_f, *sin_b)


def _s5_glu_kernel(y_ref, w_ref, b_ref, o_ref, wb_ref):
    @pl.when(pl.program_id(0) == 0)
    def _():
        wb_ref[...] = w_ref[...].astype(BF16)

    y = _gelu(y_ref[...])
    o_ref[...] = (y * _sigmoid(_dot(y.astype(BF16), wb_ref[...]) + b_ref[...])).astype(BF16)


def _s5_glu(cfg, y, w, b, l):
    tm = min(512, cfg.ll)
    return pl.pallas_call(
        _s5_glu_kernel,
        grid=(cfg.r // tm,),
        in_specs=[pl.BlockSpec((tm, D_BRANCH), lambda i: (i, 0)),
                  pl.BlockSpec((None, D_BRANCH, D_BRANCH), lambda i: (l, 0, 0)),
                  pl.BlockSpec((None, 1, D_BRANCH), lambda i: (l, 0, 0))],
        out_specs=pl.BlockSpec((tm, D_BRANCH), lambda i: (i, 0)),
        out_shape=jax.ShapeDtypeStruct((cfg.r, D_BRANCH), BF16),
        scratch_shapes=[pltpu.VMEM((D_BRANCH, D_BRANCH), BF16)],
        compiler_params=_cparams(1, 2 * _nbytes((tm, D_BRANCH), F32), _nbytes((D_BRANCH, D_BRANCH), F32),
                                 extra=_nbytes((D_BRANCH, D_BRANCH), BF16) + 4 * _nbytes((tm, D_BRANCH), F32)),
        name="s5_glu",
    )(y, w, b.reshape(DEPTH, 1, D_BRANCH))


def _s5(cfg, s5m, z, p, h0_lat, l):
    pf, qf, cmat, a_chunk = s5m
    r16 = cfg.r // S5_CHUNK
    tr = min(256, cfg.ll // S5_CHUNK)
    z3 = z.reshape(r16, S5_CHUNK, N_IN)
    x = _s5_in(z3, pf, l, tr)
    njc, njl = cfg.lc // S5_CHUNK, cfg.ll // S5_CHUNK
    rc16 = cfg.rc // S5_CHUNK
    zero_h0 = jnp.zeros((2, cfg.nc, S5_GROUPS, 2 * S5_STATE), F32)
    sf_c, sb_c, fin_ctx = _s5_scan(x, a_chunk, zero_h0, l, 0, 1, cfg.nc, njc, njc)
    sf_l, sb_l, _ = _s5_scan(x, a_chunk, h0_lat, l, rc16, cfg.nl, 1, njl, min(32, njl))
    sin_f = list(sf_c) + list(sf_l)
    sin_b = list(sb_c) + list(sb_l)
    d_oct = p["s5_d"].reshape(DEPTH, S5_OCTETS, 1, LANES)
    y = _s5_out(z3, pf, cmat, qf, d_oct, sin_f, sin_b, l, tr).reshape(cfg.r, D_BRANCH)
    return _s5_glu(cfg, y, p["s5_glu_w"], p["s5_glu_b"], l), fin_ctx


def _forward(cfg, x_prompt, x_sample, c, cache_k, cache_v, state_ssm_re, state_ssm_im, c_ctx, p):
    d = D_MODEL
    x = (x_prompt.reshape(cfg.rc, d), x_sample.reshape(cfg.rl, d))
    cvec = jnp.concatenate([c_ctx[None], c, jnp.zeros((MOD_ROWS - cfg.nseg, d), F32)], axis=0)
    mod = _mod(cvec, p["w_mod"], p["b_mod"])[:, :cfg.nseg].reshape(DEPTH, cfg.nseg, 6, 1, d)
    mod = [[mod[l, :, i] for i in range(6)] for l in range(DEPTH)]
    norm_g = p["norm_g"]

    consts = {
        "dft": {},
        "feats": {n: _hy_feats(n) for n in {cfg.lc, cfg.ll}},
        "rope": _rope_tables(cfg.ll),
    }
    for n in {cfg.lc, cfg.ll}:
        fc, fs, fct, fst = _dft_tables(n)
        consts["dft"][n] = {"fc_hi": fc.astype(BF16), "fs_hi": fs.astype(BF16),
                            "fct_hi": fct.astype(BF16), "fst_hi": fst.astype(BF16)}
    s5m = _s5_operands(p)

    kv_shape = (cfg.nc, cfg.lc, N_KV_HEADS, HEAD_DIM)
    ks, vs, srs, sis = [], [], [], []
    h = _resid_norm(cfg, x, gpre=norm_g[0, 0], sc=mod[0][1], sh=mod[0][0])
    for l in range(DEPTH):
        z = _mm(h, p["w_in"], l, tm=cfg.tm, tn=512)
        ks.append(z[:cfg.rc, K_COL0:K_COL0 + KV_COLS].reshape(kv_shape))
        vs.append(z[:cfg.rc, V_COL0:V_COL0 + KV_COLS].reshape(kv_shape))

        y_hy = _hyena(cfg, consts, z, p, l)
        sink = jnp.broadcast_to(p["attn_sink"][l][:, None], (N_Q_HEADS, LANES))
        past = cache_k.shape[2]
        y_at = jnp.concatenate([
            _attn_ctx(cfg, z, sink),
            _attn_lat(cfg, consts, z, cache_k[:, l].reshape(cfg.nl, past, KV_COLS),
                      cache_v[:, l].reshape(cfg.nl, past, KV_COLS), sink)], axis=0)
        h0 = jnp.concatenate([state_ssm_re[:, l], state_ssm_im[:, l]], axis=-1).transpose(1, 0, 2, 3)
        y_s5, fin = _s5(cfg, s5m, z, p, h0, l)
        srs.append(fin[..., :S5_STATE].transpose(1, 0, 2, 3))
        sis.append(fin[..., S5_STATE:].transpose(1, 0, 2, 3))

        merged = _merge(cfg, z, (y_hy, y_at, y_s5),
                        (p["w_branch_hy"], p["w_branch_attn"], p["w_branch_s5"]), l)
        y = _mm(merged, p["w_out"], l, tm=cfg.tm, tn=512)
        x, h = _resid_norm(cfg, x, y, gate=mod[l][2], gpost=norm_g[l, 1],
                           gpre=norm_g[l, 2], sc=mod[l][4], sh=mod[l][3])
        u = _mm(h, p["ffn_w_up"], l, tm=cfg.tm, tn=512)
        act = _ffn_act(cfg, u, p["ffn_conv_w"], p["ffn_conv_b"], l)
        f = _mm(act, p["ffn_w_down"], l, tm=min(512, cfg.tm), tn=512, w_buffers=1)
        if l + 1 < DEPTH:
            x, h = _resid_norm(cfg, x, f, gate=mod[l][5], gpost=norm_g[l, 3],
                               gpre=norm_g[l + 1, 0], sc=mod[l + 1][1], sh=mod[l + 1][0])
        else:
            x = _resid_norm(cfg, x, f, gate=mod[l][5], gpost=norm_g[l, 3], split_out=True)

    return (x[0].reshape(cfg.nc, cfg.lc, d), x[1].reshape(cfg.nl, cfg.ll, d),
            jnp.stack(ks, axis=1), jnp.stack(vs, axis=1), jnp.stack(srs, axis=1), jnp.stack(sis, axis=1))


def kernel(x_prompt, x_sample, c, cache_k, cache_v, state_ssm_re, state_ssm_im, c_ctx, w_mod, b_mod, norm_g, w_in, hy_conv_w, hy_conv_b, hy_w1, hy_b1, hy_w2, hy_b2, hy_w3, hy_freq, hy_decay, hy_bias, attn_sink, s5_lam_re, s5_lam_im, s5_log_dt, s5_b_re, s5_b_im, s5_c_re, s5_c_im, s5_d, s5_glu_w, s5_glu_b, w_branch_hy, w_branch_attn, w_branch_s5, w_out, ffn_w_up, ffn_conv_w, ffn_conv_b, ffn_w_down):
    p = dict(w_mod=w_mod, b_mod=b_mod, norm_g=norm_g, w_in=w_in, hy_conv_w=hy_conv_w, hy_conv_b=hy_conv_b,
             hy_w1=hy_w1, hy_b1=hy_b1, hy_w2=hy_w2, hy_b2=hy_b2, hy_w3=hy_w3, hy_freq=hy_freq,
             hy_decay=hy_decay, hy_bias=hy_bias, attn_sink=attn_sink, s5_lam_re=s5_lam_re,
             s5_lam_im=s5_lam_im, s5_log_dt=s5_log_dt, s5_b_re=s5_b_re, s5_b_im=s5_b_im, s5_c_re=s5_c_re,
             s5_c_im=s5_c_im, s5_d=s5_d, s5_glu_w=s5_glu_w, s5_glu_b=s5_glu_b, w_branch_hy=w_branch_hy,
             w_branch_attn=w_branch_attn, w_branch_s5=w_branch_s5, w_out=w_out, ffn_w_up=ffn_w_up,
             ffn_conv_w=ffn_conv_w, ffn_conv_b=ffn_conv_b, ffn_w_down=ffn_w_down)
    cfg = _Cfg(x_prompt.shape[0], x_prompt.shape[1], x_sample.shape[0], x_sample.shape[1])
    return _forward(cfg, x_prompt, x_sample, c, cache_k, cache_v, state_ssm_re, state_ssm_im, c_ctx, p)
```

```python
import functools
import math

import numpy as np
import jax
import jax.numpy as jnp
from jax import lax
from jax.experimental import pallas as pl
from jax.experimental.pallas import tpu as pltpu

F32 = jnp.float32
BF16 = jnp.bfloat16

D_MODEL = 4096
DEPTH = 2
GRID_W = 64
D_BRANCH = D_MODEL // 4
HY_WIDTH = D_BRANCH
HY_BANDS = 16
HY_FEAT = 1 + 2 * HY_BANDS
HY_HID = 64
HEAD_DIM = 128
N_Q_HEADS = D_BRANCH // HEAD_DIM
N_KV_HEADS = 2
GQA_GROUP = N_Q_HEADS // N_KV_HEADS
WINDOW = 128
BLOCK = 128
ROPE_BASE = 10000.0
S5_CH = 16
S5_GROUPS = D_BRANCH // S5_CH
S5_STATE = 64
S5_CHUNK = 16
S5_OCTETS = 8
S5_OCT_K = S5_CHUNK * 128
D_FF = 2 * D_MODEL
EPS = 1e-6
NEG_INF = -1e30

HY_COLS = 3 * HY_WIDTH
Q_COL0 = HY_COLS
K_COL0 = Q_COL0 + N_Q_HEADS * HEAD_DIM
KV_COLS = N_KV_HEADS * HEAD_DIM
V_COL0 = K_COL0 + KV_COLS
S5_COL0 = V_COL0 + KV_COLS
GATE_COL0 = S5_COL0 + D_BRANCH
N_IN = GATE_COL0 + 3 * D_MODEL

V7X_VMEM_REQUEST_MAX = 60 * 1024 * 1024
LANES = 128
SUBLANES = 8
MOD_ROWS = 8


def _nbytes(shape, dtype):
    return math.prod(shape) * jnp.dtype(dtype).itemsize


def _cparams(n_grid, *block_bytes, extra=0):
    est = 2 * sum(block_bytes) + extra + (4 << 20)
    return pltpu.CompilerParams(
        dimension_semantics=("arbitrary",) * n_grid,
        vmem_limit_bytes=int(min(max(est, 16 << 20), V7X_VMEM_REQUEST_MAX)))


def _dot(a, b):
    return jnp.dot(a, b, preferred_element_type=F32)


def _dot_nt(a, b):
    return lax.dot_general(a, b, (((1,), (1,)), ((), ())), preferred_element_type=F32)


def _split(a):
    hi = a.astype(BF16)
    return hi, (a - hi.astype(F32)).astype(BF16)


def _dot3(a, b):
    ah, al = _split(a)
    bh, bl = _split(b)
    return _dot(ah, bh) + (_dot(ah, bl) + _dot(al, bh))


def _sigmoid(x):
    return 1.0 / (1.0 + jnp.exp(-x))


def _gelu(x):
    c = math.sqrt(2.0 / math.pi)
    return x * (0.5 + 0.5 * jnp.tanh(x * (c + (c * 0.044715) * (x * x))))


def _rms(x, g):
    return x * lax.rsqrt(jnp.mean(x * x, axis=-1, keepdims=True) + EPS) * g


class _Cfg:
    def __init__(self, nc, lc, nl, ll):
        self.nc, self.lc, self.nl, self.ll = nc, lc, nl, ll
        self.rc = nc * lc
        self.rl = nl * ll
        self.r = self.rc + self.rl
        self.nseg = 1 + nl
        assert self.rc % ll == 0 and ll % lc == 0 and lc % BLOCK == 0
        self.tm = min(1024, ll)
        self.rb = ll

    def seg_of_row(self, row0):
        return jnp.where(row0 >= self.rc, (row0 - self.rc) // self.ll + 1, 0)


def _mod_kernel(c_ref, w_ref, b_ref, o_ref):
    c = c_ref[...]
    s = (c * _sigmoid(c)).astype(BF16)
    o_ref[...] = _dot(s, w_ref[...].astype(BF16)) + b_ref[...]


def _mod(cvec, w_mod, b_mod):
    n = w_mod.shape[-1]
    tn = 512
    return pl.pallas_call(
        _mod_kernel,
        grid=(DEPTH, n // tn),
        in_specs=[pl.BlockSpec((MOD_ROWS, D_MODEL), lambda l, j: (0, 0)),
                  pl.BlockSpec((None, D_MODEL, tn), lambda l, j: (l, 0, j)),
                  pl.BlockSpec((None, 1, tn), lambda l, j: (l, 0, j))],
        out_specs=pl.BlockSpec((None, MOD_ROWS, tn), lambda l, j: (l, 0, j)),
        out_shape=jax.ShapeDtypeStruct((DEPTH, MOD_ROWS, n), F32),
        compiler_params=_cparams(2, _nbytes((D_MODEL, tn), F32), extra=_nbytes((D_MODEL, tn), BF16)),
        name="mod",
    )(cvec, w_mod, b_mod.reshape(DEPTH, 1, n))


def _resid_norm_kernel(*refs, has_y, has_h, split_in, split_out, n_ctx_tiles):
    it = iter(refs)
    in_ctx = pl.program_id(0) < n_ctx_tiles
    if split_in:
        xa_ref, xb_ref = next(it), next(it)
        x = jnp.where(in_ctx, xa_ref[...], xb_ref[...])
    else:
        x = next(it)[...]
    if has_y:
        y_ref, gate_ref, gpost_ref = next(it), next(it), next(it)
    if has_h:
        gpre_ref, sc_ref, sh_ref = next(it), next(it), next(it)
    if has_y:
        x = x + gate_ref[...] * _rms(y_ref[...], gpost_ref[...])
        if split_out:
            xa_out, xb_out = next(it), next(it)

            @pl.when(in_ctx)
            def _():
                xa_out[...] = x

            @pl.when(jnp.logical_not(in_ctx))
            def _():
                xb_out[...] = x
        else:
            xo_ref = next(it)
            xo_ref[...] = x
    if has_h:
        ho_ref = next(it)
        ho_ref[...] = (_rms(x, gpre_ref[...]) * (1.0 + sc_ref[...]) + sh_ref[...]).astype(BF16)


def _resid_norm(cfg, x, y=None, gate=None, gpost=None, gpre=None, sc=None, sh=None, split_out=False):
    has_y, has_h, split_in = y is not None, gpre is not None, isinstance(x, tuple)
    tm = 256
    nca = cfg.rc // tm
    row = pl.BlockSpec((tm, D_MODEL), lambda i: (i, 0))
    row_a = pl.BlockSpec((tm, D_MODEL), lambda i: (jnp.minimum(i, nca - 1), 0))
    row_b = pl.BlockSpec((tm, D_MODEL), lambda i: (jnp.maximum(i - nca, 0), 0))
    vec = pl.BlockSpec((1, D_MODEL), lambda i: (0, 0))
    seg = pl.BlockSpec((None, 1, D_MODEL), lambda i: (cfg.seg_of_row(i * tm), 0, 0))
    args, specs = (list(x), [row_a, row_b]) if split_in else ([x], [row])
    outs, ospecs = [], []
    if has_y:
        args += [y, gate, gpost.reshape(1, D_MODEL)]
        specs += [row, seg, vec]
        if split_out:
            outs += [jax.ShapeDtypeStruct((cfg.rc, D_MODEL), F32), jax.ShapeDtypeStruct((cfg.rl, D_MODEL), F32)]
            ospecs += [row_a, row_b]
        else:
            outs.append(jax.ShapeDtypeStruct((cfg.r, D_MODEL), F32))
            ospecs.append(row)
    if has_h:
        args += [gpre.reshape(1, D_MODEL), sc, sh]
        specs += [vec, seg, seg]
        outs.append(jax.ShapeDtypeStruct((cfg.r, D_MODEL), BF16))
        ospecs.append(row)
    res = pl.pallas_call(
        functools.partial(_resid_norm_kernel, has_y=has_y, has_h=has_h, split_in=split_in, split_out=split_out,
                          n_ctx_tiles=nca),
        grid=(cfg.r // tm,),
        in_specs=specs, out_specs=ospecs, out_shape=outs,
        compiler_params=_cparams(1, 5 * _nbytes((tm, D_MODEL), F32)),
        name="resid_norm",
    )(*args)
    return res if len(res) > 1 else res[0]


def _mm_kernel(x_ref, w_ref, o_ref, wb_ref):
    @pl.when(pl.program_id(1) == 0)
    def _():
        wb_ref[...] = w_ref[...].astype(BF16)

    o_ref[...] = _dot(x_ref[...], wb_ref[...]).astype(o_ref.dtype)


def _mm(x, w, l, *, tm, tn, out_dtype=F32, w_buffers=2):
    m, k = x.shape
    n = w.shape[-1]
    w_mode = {} if w_buffers == 2 else {"pipeline_mode": pl.Buffered(w_buffers)}
    return pl.pallas_call(
        _mm_kernel,
        grid=(n // tn, m // tm),
        in_specs=[pl.BlockSpec((tm, k), lambda j, i: (i, 0)),
                  pl.BlockSpec((None, k, tn), lambda j, i: (l, 0, j), **w_mode)],
        out_specs=pl.BlockSpec((tm, tn), lambda j, i: (i, j)),
        out_shape=jax.ShapeDtypeStruct((m, n), out_dtype),
        scratch_shapes=[pltpu.VMEM((k, tn), BF16)],
        compiler_params=_cparams(2, _nbytes((tm, k), BF16), _nbytes((tm, tn), out_dtype),
                                 extra=w_buffers * _nbytes((k, tn), F32) + _nbytes((k, tn), BF16)
                                 + _nbytes((tm, tn), F32)),
        name="mm",
    )(x, w)


def _merge_kernel(g0, g1, g2, y0, y1, y2, w0, w1, w2, o_ref, wb_ref):
    @pl.when(pl.program_id(1) == 0)
    def _():
        wb_ref[0] = w0[...].astype(BF16)
        wb_ref[1] = w1[...].astype(BF16)
        wb_ref[2] = w2[...].astype(BF16)

    acc = _sigmoid(g0[...]) * _dot(y0[...], wb_ref[0])
    acc += _sigmoid(g1[...]) * _dot(y1[...], wb_ref[1])
    acc += _sigmoid(g2[...]) * _dot(y2[...], wb_ref[2])
    o_ref[...] = acc.astype(o_ref.dtype)


def _merge(cfg, z, ys, ws, l):
    tm, tn = cfg.tm, 512
    gate_specs = [pl.BlockSpec((tm, tn), functools.partial(
        lambda j, i, b: (i, (GATE_COL0 + b * D_MODEL) // tn + j), b=b)) for b in range(3)]
    y_spec = pl.BlockSpec((tm, D_BRANCH), lambda j, i: (i, 0))
    w_spec = pl.BlockSpec((None, D_BRANCH, tn), lambda j, i: (l, 0, j))
    return pl.pallas_call(
        _merge_kernel,
        grid=(D_MODEL // tn, cfg.r // tm),
        in_specs=gate_specs + [y_spec] * 3 + [w_spec] * 3,
        out_specs=pl.BlockSpec((tm, tn), lambda j, i: (i, j)),
        out_shape=jax.ShapeDtypeStruct((cfg.r, D_MODEL), BF16),
        scratch_shapes=[pltpu.VMEM((3, D_BRANCH, tn), BF16)],
        compiler_params=_cparams(2, 3 * _nbytes((tm, tn), F32), 3 * _nbytes((tm, D_BRANCH), BF16),
                                 3 * _nbytes((D_BRANCH, tn), F32), _nbytes((tm, tn), BF16),
                                 extra=3 * _nbytes((D_BRANCH, tn), BF16) + 2 * _nbytes((tm, tn), F32)),
        name="merge",
    )(z, z, z, *ys, *ws)


def _block_seq_len(cfg):
    return jnp.where(pl.program_id(0) < cfg.rc // cfg.rb, cfg.lc, cfg.ll)


def _dwconv3_rows(x_ref, w_ref, b_ref, r0, sl):
    total = x_ref.shape[0]
    cur = x_ref[r0:r0 + BLOCK, :]
    sub = lax.broadcasted_iota(jnp.int32, (SUBLANES, cur.shape[1]), 0)
    xm = x_ref[r0 - 1:r0 - 1 + BLOCK, :] if r0 > 0 else pltpu.roll(cur, 1, 0)
    first = jnp.where(((r0 + sub) & (sl - 1)) == 0, 0.0, xm[:SUBLANES])
    xm = jnp.concatenate([first, xm[SUBLANES:]], axis=0)
    xp = x_ref[r0 + 1:r0 + 1 + BLOCK, :] if r0 + BLOCK < total else pltpu.roll(cur, BLOCK - 1, 0)
    last = jnp.where(((r0 + BLOCK - SUBLANES + sub) & (sl - 1)) == sl - 1, 0.0, xp[BLOCK - SUBLANES:])
    xp = jnp.concatenate([xp[:BLOCK - SUBLANES], last], axis=0)
    return xm * w_ref[0:1, :] + cur * w_ref[1:2, :] + xp * w_ref[2:3, :] + b_ref[...]


def _hy_dwconv_kernel(x_ref, w_ref, b_ref, o_ref, *, cfg):
    sl = _block_seq_len(cfg)
    for r0 in range(0, x_ref.shape[0], BLOCK):
        o_ref[r0:r0 + BLOCK, :] = _dwconv3_rows(x_ref, w_ref, b_ref, r0, sl)


def _hy_dwconv(cfg, z, w, b, l):
    tn = 256
    blk = _nbytes((cfg.rb, tn), F32)
    return pl.pallas_call(
        functools.partial(_hy_dwconv_kernel, cfg=cfg),
        grid=(cfg.r // cfg.rb, HY_COLS // tn),
        in_specs=[pl.BlockSpec((cfg.rb, tn), lambda r, j: (r, j)),
                  pl.BlockSpec((None, 3, tn), lambda r, j: (l, 0, j)),
                  pl.BlockSpec((None, 1, tn), lambda r, j: (l, 0, j))],
        out_specs=pl.BlockSpec((cfg.rb, tn), lambda r, j: (r, j)),
        out_shape=jax.ShapeDtypeStruct((cfg.r, HY_COLS), F32),
        compiler_params=_cparams(2, 2 * blk, extra=4 * blk),
        name="hy_dwconv",
    )(z, w, b.reshape(DEPTH, 1, HY_COLS))


def _ffn_act_kernel(a_ref, b_ref, wa_ref, wb_ref, ba_ref, bb_ref, o_ref, *, cfg):
    sl = _block_seq_len(cfg)
    for r0 in range(0, a_ref.shape[0], BLOCK):
        a = _dwconv3_rows(a_ref, wa_ref, ba_ref, r0, sl)
        b = _dwconv3_rows(b_ref, wb_ref, bb_ref, r0, sl)
        o_ref[r0:r0 + BLOCK, :] = (_gelu(a) * b).astype(BF16)


def _ffn_act(cfg, u, w, b, l):
    tn = 256
    nb = D_FF // tn
    blk = _nbytes((cfg.rb, tn), F32)
    b3 = b.reshape(DEPTH, 1, 2 * D_FF)
    return pl.pallas_call(
        functools.partial(_ffn_act_kernel, cfg=cfg),
        grid=(cfg.r // cfg.rb, nb),
        in_specs=[pl.BlockSpec((cfg.rb, tn), lambda r, j: (r, j)),
                  pl.BlockSpec((cfg.rb, tn), lambda r, j: (r, j + nb)),
                  pl.BlockSpec((None, 3, tn), lambda r, j: (l, 0, j)),
                  pl.BlockSpec((None, 3, tn), lambda r, j: (l, 0, j + nb)),
                  pl.BlockSpec((None, 1, tn), lambda r, j: (l, 0, j)),
                  pl.BlockSpec((None, 1, tn), lambda r, j: (l, 0, j + nb))],
        out_specs=pl.BlockSpec((cfg.rb, tn), lambda r, j: (r, j)),
        out_shape=jax.ShapeDtypeStruct((cfg.r, D_FF), BF16),
        compiler_params=_cparams(2, 3 * blk, extra=6 * blk),
        name="ffn_act",
    )(u, u, w, w, b3, b3)


def _dft_tables(length):
    half = length // 2
    period = 4 * length
    r = 1 << (half.bit_length() // 2)
    idx = np.arange(half, dtype=np.int64)
    hi = np.arange(half // r, dtype=np.int64) * r
    lo = np.arange(r, dtype=np.int64)

    def cs(m):
        ang = 2.0 * np.pi * (m % period).astype(np.float64) / period
        return jnp.asarray(np.cos(ang), F32), jnp.asarray(np.sin(ang), F32)

    def combine(phase_hi, phase_lo):
        ca, sa = cs(phase_hi)
        cb, sb = cs(phase_lo)
        c = ca[:, :, None] * cb[:, None, :] - sa[:, :, None] * sb[:, None, :]
        s = sa[:, :, None] * cb[:, None, :] + ca[:, :, None] * sb[:, None, :]
        return c.reshape(half, half), s.reshape(half, half)

    odd = (2 * idx + 1)[:, None]
    row = idx[:, None]
    out = {}
    out["ce"], out["se"] = combine(odd * (2 * hi)[None, :], odd * (2 * lo)[None, :])
    out["co"], out["so"] = combine(odd * (2 * hi)[None, :], odd * (2 * lo + 1)[None, :])
    out["cet"], out["set"] = combine(2 * row * (2 * hi)[None, :], 2 * row * (2 * lo + 1)[None, :])
    out["cot"], out["sot"] = combine((2 * row + 1) * (2 * hi)[None, :], (2 * row + 1) * (2 * lo + 1)[None, :])
    return out


def _hy_feats(length):
    pos = jnp.arange(length, dtype=F32)
    t = (pos / length)[:, None]
    bands = jnp.linspace(1e-4, HY_BANDS - 1, HY_BANDS, dtype=F32)
    wpos = 2.0 * math.pi * t * bands
    feats = jnp.concatenate([t, jnp.cos(wpos), -jnp.sin(wpos)], axis=-1)
    return jnp.pad(feats, ((0, 0), (0, LANES - HY_FEAT)))


def _hy_filter_kernel(f_ref, w1_ref, b1_ref, w2_ref, b2_ref, w3_ref, fr_ref, dec_ref, hs_ref, hd_ref, hb0_ref):
    feats = f_ref[...]
    fr = fr_ref[...]
    h = jnp.sin(fr * (_dot3(feats, w1_ref[...]) + b1_ref[...]))
    h = jnp.sin(fr * (_dot3(h, w2_ref[...]) + b2_ref[...]))
    h = _dot3(h, w3_ref[...]) * jnp.exp(-feats[:, 0:1] * jnp.abs(dec_ref[...]))
    half = 2 * HY_WIDTH
    hf, hb = h[:, :half], h[:, half:]
    hs_ref[...] = (hf + hb).astype(BF16)
    hd_ref[...] = (hb - hf).astype(BF16)

    @pl.when(pl.program_id(0) == 0)
    def _():
        hb0_ref[...] = jnp.broadcast_to(hb[0:1, :], hb0_ref.shape)


def _hy_filter(length, feats, w1p, b1, w2, b2, w3, freq, decay, l):
    tl = 256
    half = 2 * HY_WIDTH
    lsel = lambda i: (l, 0, 0)
    return pl.pallas_call(
        _hy_filter_kernel,
        grid=(length // tl,),
        in_specs=[pl.BlockSpec((tl, LANES), lambda i: (i, 0)),
                  pl.BlockSpec((None, LANES, HY_HID), lsel),
                  pl.BlockSpec((None, 1, HY_HID), lsel),
                  pl.BlockSpec((None, HY_HID, HY_HID), lsel),
                  pl.BlockSpec((None, 1, HY_HID), lsel),
                  pl.BlockSpec((None, HY_HID, 2 * half), lsel),
                  pl.BlockSpec((None, 1, HY_HID), lsel),
                  pl.BlockSpec((None, 1, 2 * half), lsel)],
        out_specs=[pl.BlockSpec((tl, half), lambda i: (i, 0)),
                   pl.BlockSpec((tl, half), lambda i: (i, 0)),
                   pl.BlockSpec((8, half), lambda i: (0, 0))],
        out_shape=[jax.ShapeDtypeStruct((length, half), BF16),
                   jax.ShapeDtypeStruct((length, half), BF16),
                   jax.ShapeDtypeStruct((8, half), F32)],
        compiler_params=_cparams(1, 2 * _nbytes((tl, half), F32), extra=6 * _nbytes((tl, 2 * half), F32)),
        name="hy_filter",
    )(feats, w1p, b1, w2, b2, w3, freq, decay)


def _hy_spec_kernel(ce, se, co, so, hse, hso, hde, hdo, hb0_ref, kar, kai, kbr, kbi):
    a = _dot(ce[...], hse[...])
    b = _dot(co[...], hso[...])
    c = _dot(se[...], hde[...])
    d = _dot(so[...], hdo[...])
    kar[...] = a + b - hb0_ref[0:1, :]
    kai[...] = c + d
    kbr[...] = a - b - hb0_ref[0:1, :]
    kbi[...] = d - c


def _hy_spec(length, mats, hs, hd, hb0):
    half_len = length // 2
    tk = min(512, half_len)
    tn = 512
    width = 2 * HY_WIDTH
    ncb = width // tn
    fspec = pl.BlockSpec((tk, half_len), lambda j, i: (i, 0))
    espec = pl.BlockSpec((half_len, tn), lambda j, i: (0, j))
    ospec = pl.BlockSpec((half_len, tn), lambda j, i: (0, ncb + j))
    kspec = pl.BlockSpec((tk, tn), lambda j, i: (i, j))
    hs2 = hs.reshape(half_len, 2 * width)
    hd2 = hd.reshape(half_len, 2 * width)
    return pl.pallas_call(
        _hy_spec_kernel,
        grid=(ncb, half_len // tk),
        in_specs=[fspec] * 4 + [espec, ospec, espec, ospec, pl.BlockSpec((8, tn), lambda j, i: (0, j))],
        out_specs=[kspec] * 4,
        out_shape=[jax.ShapeDtypeStruct((half_len, width), F32)] * 4,
        compiler_params=_cparams(2, 4 * _nbytes((tk, half_len), BF16), 4 * _nbytes((half_len, tn), BF16),
                                 4 * _nbytes((tk, tn), F32), extra=4 * _nbytes((tk, tn), F32)),
        name="hy_spec",
    )(mats["ce"], mats["se"], mats["co"], mats["so"], hs2, hs2, hd2, hd2, hb0)


def _hy_fwd_kernel(ce, se, co, so, ue_ref, uo_ref, kar, kai, kbr, kbi, pr_ref, pi_ref, qr_ref, qi_ref, ub_ref):
    @pl.when(pl.program_id(2) == 0)
    def _():
        ub_ref[0] = ue_ref[...].astype(BF16)
        ub_ref[1] = uo_ref[...].astype(BF16)

    e_re, e_im = _dot(ce[...], ub_ref[0]), -_dot(se[...], ub_ref[0])
    o_re, o_im = _dot(co[...], ub_ref[1]), -_dot(so[...], ub_ref[1])
    ua_re, ua_im = e_re + o_re, e_im + o_im
    ub_re, ub_im = e_re - o_re, o_im - e_im
    ya_re = ua_re * kar[...] - ua_im * kai[...]
    ya_im = ua_re * kai[...] + ua_im * kar[...]
    yb_re = ub_re * kbr[...] - ub_im * kbi[...]
    yb_im = ub_re * kbi[...] + ub_im * kbr[...]
    pr_ref[...] = (ya_re + yb_re).astype(BF16)
    pi_ref[...] = (ya_im - yb_im).astype(BF16)
    qr_ref[...] = (ya_re - yb_re).astype(BF16)
    qi_ref[...] = (ya_im + yb_im).astype(BF16)


def _hy_inv_kernel(cet, set_, cot, sot, pr, pi_, qr, qi, ue_ref, uo_ref, ge_ref, go_ref, bias_ref, ye_ref, yo_ref,
                   *, inv_len):
    even = _dot(cet[...], pr[...]) - _dot(set_[...], pi_[...])
    odd = _dot(cot[...], qr[...]) - _dot(sot[...], qi[...])
    ye_ref[...] = (ge_ref[...] * (even * inv_len + ue_ref[...] * bias_ref[...])).astype(ye_ref.dtype)
    yo_ref[...] = (go_ref[...] * (odd * inv_len + uo_ref[...] * bias_ref[...])).astype(yo_ref.dtype)


def _hy_conv(length, nseq, mats, spec, order, data, gate, bias, l, out_dtype):
    half_len = length // 2
    tn = min(512, HY_WIDTH)
    tk = min(512, half_len)
    ncb = HY_WIDTH // tn
    ntk = half_len // tk
    fspec = pl.BlockSpec((tk, half_len), lambda s, j, i: (i, 0))
    kspec = pl.BlockSpec((tk, tn), lambda s, j, i: (i, order * ncb + j))
    tile = pl.BlockSpec((tk, tn), lambda s, j, i: (s * ntk + i, j))

    def full(loc):
        _, rb0, cb0 = loc
        return pl.BlockSpec((half_len, tn), lambda s, j, i: (rb0 + s, cb0 + j))

    def rows(loc):
        _, rb0, cb0 = loc
        return pl.BlockSpec((tk, tn), lambda s, j, i: ((rb0 + s) * ntk + i, cb0 + j))

    pq = pl.pallas_call(
        _hy_fwd_kernel,
        grid=(nseq, ncb, ntk),
        in_specs=[fspec] * 4 + [full(data[0]), full(data[1])] + [kspec] * 4,
        out_specs=[tile] * 4,
        out_shape=[jax.ShapeDtypeStruct((nseq * half_len, HY_WIDTH), BF16)] * 4,
        scratch_shapes=[pltpu.VMEM((2, half_len, tn), BF16)],
        compiler_params=_cparams(3, 4 * _nbytes((tk, half_len), BF16), 2 * _nbytes((half_len, tn), F32),
                                 4 * _nbytes((tk, tn), F32), 4 * _nbytes((tk, tn), BF16),
                                 extra=2 * _nbytes((half_len, tn), BF16) + 12 * _nbytes((tk, tn), F32)),
        name="hy_fwd",
    )(mats["ce"], mats["se"], mats["co"], mats["so"], data[0][0], data[1][0], *spec)

    resident = pl.BlockSpec((half_len, tn), lambda s, j, i: (s, j))
    return pl.pallas_call(
        functools.partial(_hy_inv_kernel, inv_len=1.0 / length),
        grid=(nseq, ncb, ntk),
        in_specs=[fspec] * 4 + [resident] * 4 + [rows(data[0]), rows(data[1]), rows(gate[0]), rows(gate[1]),
                                                 pl.BlockSpec((None, None, 1, tn), lambda s, j, i: (l, order, 0, j))],
        out_specs=[tile, tile],
        out_shape=[jax.ShapeDtypeStruct((nseq * half_len, HY_WIDTH), out_dtype)] * 2,
        compiler_params=_cparams(3, 4 * _nbytes((tk, half_len), BF16), 4 * _nbytes((half_len, tn), BF16),
                                 6 * _nbytes((tk, tn), F32), extra=6 * _nbytes((tk, tn), F32)),
        name="hy_inv",
    )(mats["cet"], mats["set"], mats["cot"], mats["sot"], *pq, data[0][0], data[1][0], gate[0][0], gate[1][0], bias)


def _hyena(cfg, consts, z, p, l):
    zc = _hy_dwconv(cfg, z, p["hy_conv_w"], p["hy_conv_b"], l)
    zc2 = zc.reshape(cfg.r // 2, 2 * HY_COLS)
    bias = p["hy_bias"].reshape(DEPTH, 2, 1, HY_WIDTH)
    w1p = jnp.pad(p["hy_w1"], ((0, 0), (0, LANES - HY_FEAT), (0, 0)))
    tn = min(512, HY_WIDTH)
    outs = []
    for length, nseq, row0 in ((cfg.lc, cfg.nc, 0), (cfg.ll, cfg.nl, cfg.rc)):
        mats = consts["dft"][length]
        hs, hd, hb0 = _hy_filter(length, consts["feats"][length], w1p, p["hy_b1"].reshape(DEPTH, 1, HY_HID),
                                 p["hy_w2"], p["hy_b2"].reshape(DEPTH, 1, HY_HID), p["hy_w3"],
                                 p["hy_freq"].reshape(DEPTH, 1, HY_HID),
                                 p["hy_decay"].reshape(DEPTH, 1, 4 * HY_WIDTH), l)
        spec = _hy_spec(length, mats, hs, hd, hb0)
        rb0 = row0 // length

        def zc_cols(col0):
            return (zc2, rb0, col0 // tn), (zc2, rb0, (HY_COLS + col0) // tn)

        y1 = _hy_conv(length, nseq, mats, spec, 0, zc_cols(0), zc_cols(HY_WIDTH), bias, l, F32)
        y2 = _hy_conv(length, nseq, mats, spec, 1, ((y1[0], 0, 0), (y1[1], 0, 0)), zc_cols(2 * HY_WIDTH), bias, l,
                      BF16)
        outs.append(jnp.stack(y2, axis=1).reshape(nseq * length, HY_WIDTH))
    return jnp.concatenate(outs, axis=0)


def _softmax_sink_pv(parts, sink):
    m = sink
    for s, _ in parts:
        m = jnp.maximum(m, jnp.max(s, axis=-1, keepdims=True))
    den = jnp.exp(sink - m)
    acc = None
    for s, v in parts:
        e = jnp.exp(s - m)
        den = den + jnp.sum(e, axis=-1, keepdims=True)
        pv = _dot(e.astype(BF16), v)
        acc = pv if acc is None else acc + pv
    return acc / den


def _group_sink(sink_ref, h):
    return jnp.concatenate([jnp.broadcast_to(sink_ref[hq:hq + 1, 0:1], (BLOCK, 1))
                            for hq in range(h * GQA_GROUP, (h + 1) * GQA_GROUP)], axis=0)


def _store_group(o_ref, h, o):
    for g in range(GQA_GROUP):
        hq = h * GQA_GROUP + g
        o_ref[:, hq * HEAD_DIM:(hq + 1) * HEAD_DIM] = o[g * BLOCK:(g + 1) * BLOCK].astype(BF16)


def _attn_ctx_kernel(q_ref, k_ref, v_ref, sink_ref, o_ref):
    scale = HEAD_DIM ** -0.5
    for h in range(N_KV_HEADS):
        hs = slice(h * HEAD_DIM, (h + 1) * HEAD_DIM)
        k = k_ref[:, hs].astype(BF16)
        v = v_ref[:, hs].astype(BF16)
        q = jnp.concatenate([q_ref[:, hq * HEAD_DIM:(hq + 1) * HEAD_DIM]
                             for hq in range(h * GQA_GROUP, (h + 1) * GQA_GROUP)], axis=0).astype(BF16)
        s = _dot_nt(q, k) * scale
        _store_group(o_ref, h, _softmax_sink_pv([(s, v)], _group_sink(sink_ref, h)))


def _attn_ctx(cfg, z, sink):
    nb = cfg.lc // BLOCK
    qw = N_Q_HEADS * HEAD_DIM
    return pl.pallas_call(
        _attn_ctx_kernel,
        grid=(cfg.nc, nb),
        in_specs=[pl.BlockSpec((BLOCK, qw), lambda s, i: (s * nb + i, Q_COL0 // qw)),
                  pl.BlockSpec((cfg.lc, KV_COLS), lambda s, i: (s, K_COL0 // KV_COLS)),
                  pl.BlockSpec((cfg.lc, KV_COLS), lambda s, i: (s, V_COL0 // KV_COLS)),
                  pl.BlockSpec((N_Q_HEADS, LANES), lambda s, i: (0, 0))],
        out_specs=pl.BlockSpec((BLOCK, qw), lambda s, i: (s * nb + i, 0)),
        out_shape=jax.ShapeDtypeStruct((cfg.rc, qw), BF16),
        compiler_params=_cparams(2, _nbytes((BLOCK, qw), F32), 2 * _nbytes((cfg.lc, KV_COLS), F32)),
        name="attn_ctx",
    )(z, z, z, sink)


def _rope(x, c, s):
    return x * c + pltpu.roll(x, HEAD_DIM // 2, 1) * s


def _attn_lat_kernel(q_ref, kp_ref, kc_ref, kn_ref, vp_ref, vc_ref, vn_ref, ck_ref, cv_ref,
                     cq, sq, cp, sp, cn, sn, sink_ref, o_ref, *, nb):
    i = pl.program_id(1)
    scale = HEAD_DIM ** -0.5
    shape = (GQA_GROUP * BLOCK, 3 * BLOCK)
    qi = lax.broadcasted_iota(jnp.int32, shape, 0) & (BLOCK - 1)
    kj = lax.broadcasted_iota(jnp.int32, shape, 1)
    ok = jnp.abs(kj - BLOCK - qi) <= WINDOW
    ok = ok & ((kj >= BLOCK) | (i > 0)) & ((kj < 2 * BLOCK) | (i < nb - 1))
    for h in range(N_KV_HEADS):
        hs = slice(h * HEAD_DIM, (h + 1) * HEAD_DIM)
        kw = jnp.concatenate([_rope(kp_ref[:, hs], cp[...], sp[...]),
                              _rope(kc_ref[:, hs], cq[...], sq[...]),
                              _rope(kn_ref[:, hs], cn[...], sn[...])], axis=0).astype(BF16)
        vw = jnp.concatenate([vp_ref[:, hs], vc_ref[:, hs], vn_ref[:, hs]], axis=0).astype(BF16)
        kctx = ck_ref[:, hs].astype(BF16)
        vctx = cv_ref[:, hs].astype(BF16)
        q = jnp.concatenate([_rope(q_ref[:, hq * HEAD_DIM:(hq + 1) * HEAD_DIM], cq[...], sq[...])
                             for hq in range(h * GQA_GROUP, (h + 1) * GQA_GROUP)], axis=0).astype(BF16)
        s1 = _dot_nt(q, kctx) * scale
        s2 = jnp.where(ok, _dot_nt(q, kw) * scale, NEG_INF)
        _store_group(o_ref, h, _softmax_sink_pv([(s1, vctx), (s2, vw)], _group_sink(sink_ref, h)))


def _attn_lat(cfg, consts, z, ck, cv, sink):
    nb = cfg.ll // BLOCK
    rb0 = cfg.rc // BLOCK
    qw = N_Q_HEADS * HEAD_DIM
    past = ck.shape[1]
    cos2, sin2 = consts["rope"]

    def blk(delta):
        return lambda b, i: rb0 + b * nb + jnp.clip(i + delta, 0, nb - 1)

    def zspec(width, col0, delta):
        rowf = blk(delta)
        return pl.BlockSpec((BLOCK, width), lambda b, i: (rowf(b, i), col0 // width))

    def tspec(delta):
        return pl.BlockSpec((BLOCK, HEAD_DIM), lambda b, i: (jnp.clip(i + delta, 0, nb - 1), 0))

    cspec = pl.BlockSpec((None, past, KV_COLS), lambda b, i: (b, 0, 0))
    return pl.pallas_call(
        functools.partial(_attn_lat_kernel, nb=nb),
        grid=(cfg.nl, nb),
        in_specs=[zspec(qw, Q_COL0, 0),
                  zspec(KV_COLS, K_COL0, -1), zspec(KV_COLS, K_COL0, 0), zspec(KV_COLS, K_COL0, 1),
                  zspec(KV_COLS, V_COL0, -1), zspec(KV_COLS, V_COL0, 0), zspec(KV_COLS, V_COL0, 1),
                  cspec, cspec,
                  tspec(0), tspec(0), tspec(-1), tspec(-1), tspec(1), tspec(1),
                  pl.BlockSpec((N_Q_HEADS, LANES), lambda b, i: (0, 0))],
        out_specs=pl.BlockSpec((BLOCK, qw), lambda b, i: (b * nb + i, 0)),
        out_shape=jax.ShapeDtypeStruct((cfg.rl, qw), BF16),
        compiler_params=_cparams(2, _nbytes((BLOCK, qw), F32), 6 * _nbytes((BLOCK, KV_COLS), F32),
                                 2 * _nbytes((past, KV_COLS), F32)),
        name="attn_lat",
    )(z, z, z, z, z, z, z, ck, cv, cos2, sin2, cos2, sin2, cos2, sin2, sink)


def _rope_tables(length):
    rows = length // GRID_W
    row = jnp.repeat(jnp.arange(rows, dtype=F32), GRID_W)
    col = jnp.tile(jnp.arange(GRID_W, dtype=F32), rows)
    n_freq = HEAD_DIM // 4
    inv = ROPE_BASE ** (-jnp.arange(n_freq, dtype=F32) / n_freq)
    ang = jnp.concatenate([row[:, None] * inv, col[:, None] * inv], axis=-1)
    cos, sin = jnp.cos(ang), jnp.sin(ang)
    return jnp.concatenate([cos, cos], axis=-1), jnp.concatenate([-sin, sin], axis=-1)


def _s5_operands(p):
    t_len = S5_CHUNK
    lr, li = p["s5_lam_re"].astype(F32), p["s5_lam_im"].astype(F32)
    n_layers = lr.shape[0]
    dt = jnp.exp(p["s5_log_dt"].astype(F32))[..., None]
    zr, zi = lr * dt, li * dt
    mag = jnp.exp(zr)
    ar, ai = mag * jnp.cos(zi), mag * jnp.sin(zi)
    den = lr * lr + li * li
    cr = ((ar - 1.0) * lr + ai * li) / den
    ci = (ai * lr - (ar - 1.0) * li) / den
    k = jnp.arange(t_len + 1, dtype=F32)[:, None, None, None, None]
    pm = jnp.exp(k * zr)
    pr, pi = pm * jnp.cos(k * zi), pm * jnp.sin(k * zi)

    bt_re = p["s5_b_re"].astype(F32).transpose(0, 1, 2, 4, 3)
    bt_im = p["s5_b_im"].astype(F32).transpose(0, 1, 2, 4, 3)
    bbr = cr[:, :, :, None, :] * bt_re - ci[:, :, :, None, :] * bt_im
    bbi = cr[:, :, :, None, :] * bt_im + ci[:, :, :, None, :] * bt_re
    pk_r, pk_i = pr[:t_len, :, :, :, None, :], pi[:t_len, :, :, :, None, :]
    ab = jnp.concatenate([pk_r * bbr - pk_i * bbi, pk_r * bbi + pk_i * bbr], axis=-1)
    ab = ab.reshape(t_len, n_layers, 2, D_BRANCH, 2 * S5_STATE).transpose(1, 2, 0, 3, 4)
    pf = jnp.concatenate([jnp.flip(ab[:, 0:1], axis=2), ab[:, 1:2]], axis=1)

    c_re, c_im = p["s5_c_re"].astype(F32), p["s5_c_im"].astype(F32)
    cmat = jnp.concatenate([c_re, -c_im], axis=-1).reshape(n_layers, 2, D_BRANCH, 2 * S5_STATE)
    ct_re = c_re.reshape(n_layers, 2, D_BRANCH, S5_STATE).transpose(0, 1, 3, 2)
    ct_im = c_im.reshape(n_layers, 2, D_BRANCH, S5_STATE).transpose(0, 1, 3, 2)
    pt_r = jnp.repeat(pr[1:].transpose(0, 1, 2, 4, 3), S5_CH, axis=-1)
    pt_i = jnp.repeat(pi[1:].transpose(0, 1, 2, 4, 3), S5_CH, axis=-1)
    q = jnp.concatenate([ct_re * pt_r - ct_im * pt_i, -(ct_re * pt_i + ct_im * pt_r)], axis=3)
    q = q.transpose(1, 2, 0, 3, 4)
    qf = jnp.concatenate([q[:, 0:1], jnp.flip(q[:, 1:2], axis=2)], axis=1)

    a_chunk = jnp.stack([jnp.concatenate([pr[t_len], pr[t_len]], axis=-1),
                         jnp.concatenate([-pi[t_len], pi[t_len]], axis=-1)], axis=2)
    return pf, qf, cmat, a_chunk


def _s5_ucat(z_ref):
    return jnp.concatenate([z_ref[:, t, :].astype(BF16) for t in range(S5_CHUNK)], axis=1)


def _octet_group(shape, axis):
    return lax.broadcasted_iota(jnp.int32, shape, axis) // S5_CH


def _s5_in_kernel(z_ref, pf_ref, x_ref, p_scr):
    gpo = S5_GROUPS // S5_OCTETS

    @pl.when(pl.program_id(1) == 0)
    def _():
        row_group = _octet_group((LANES, LANES), 0)
        for d in range(2):
            for t in range(S5_CHUNK):
                blk = pf_ref[d, t]
                for g in range(gpo):
                    p_scr[d, t * LANES:(t + 1) * LANES, g * LANES:(g + 1) * LANES] = (
                        jnp.where(row_group == g, blk, 0.0).astype(BF16))

    u = _s5_ucat(z_ref)
    for d in range(2):
        x_ref[d] = _dot(u, p_scr[d]).reshape(u.shape[0], gpo, 2 * S5_STATE)


def _s5_in(z3, pf, l, tr):
    r16 = z3.shape[0]
    gpo = S5_GROUPS // S5_OCTETS
    w = 2 * S5_STATE
    return pl.pallas_call(
        _s5_in_kernel,
        grid=(S5_OCTETS, r16 // tr),
        in_specs=[pl.BlockSpec((tr, S5_CHUNK, LANES), lambda o, i: (i, 0, S5_COL0 // LANES + o)),
                  pl.BlockSpec((None, 2, S5_CHUNK, LANES, w), lambda o, i: (l, 0, 0, o, 0))],
        out_specs=pl.BlockSpec((2, tr, gpo, w), lambda o, i: (0, i, o, 0)),
        out_shape=jax.ShapeDtypeStruct((2, r16, S5_GROUPS, w), F32),
        scratch_shapes=[pltpu.VMEM((2, S5_OCT_K, gpo * w), BF16)],
        compiler_params=_cparams(2, _nbytes((tr, S5_CHUNK, LANES), F32), _nbytes((2, S5_CHUNK, LANES, w), F32),
                                 _nbytes((2, tr, gpo, w), F32),
                                 extra=_nbytes((2, S5_OCT_K, gpo * w), BF16) + 4 * _nbytes((tr, S5_OCT_K), F32)),
        name="s5_in",
    )(z3, pf)


def _s5_scan_kernel(*refs, n_in, nsb, jb):
    a_ref, h0_ref = refs[0], refs[1]
    xf = refs[2:2 + n_in]
    xb = refs[2 + n_in:2 + 2 * n_in]
    sf = refs[2 + 2 * n_in:2 + 3 * n_in]
    sb = refs[2 + 3 * n_in:2 + 4 * n_in]
    fin_ref, st_ref = refs[2 + 4 * n_in], refs[3 + 4 * n_in]
    t = pl.program_id(0)

    @pl.when(t == 0)
    def _():
        st_ref[...] = h0_ref[...]

    half = S5_STATE

    def body(s, carry):
        jf = s
        jr = jb - 1 - s
        for r in range(n_in):
            for q in range(nsb):
                idx = r * nsb + q
                cur = st_ref[0, idx]
                sf[r][q * jb + jf] = cur
                st_ref[0, idx] = cur * a_ref[0, 0] + pltpu.roll(cur, half, 1) * a_ref[0, 1] + xf[r][q * jb + jf]
                cur = st_ref[1, idx]
                sb[r][q * jb + jr] = cur
                st_ref[1, idx] = cur * a_ref[1, 0] + pltpu.roll(cur, half, 1) * a_ref[1, 1] + xb[r][q * jb + jr]
        return carry

    lax.fori_loop(0, jb, body, 0)

    @pl.when(t == pl.num_programs(0) - 1)
    def _():
        fin_ref[...] = st_ref[...]


def _s5_scan(x, a_chunk, h0, l, row0, n_in, nsb, nj, jb):
    g = x.shape[2]
    w = x.shape[3]
    nblk = nj // jb
    nseq = n_in * nsb
    rows = nsb * jb
    base = row0 // rows

    def xspec(d, r):
        if d == 0:
            return pl.BlockSpec((None, rows, g, w), lambda t: (0, base + r * nblk + t, 0, 0))
        return pl.BlockSpec((None, rows, g, w), lambda t: (1, base + r * nblk + nblk - 1 - t, 0, 0))

    def sspec(d):
        if d == 0:
            return pl.BlockSpec((rows, g, w), lambda t: (t, 0, 0))
        return pl.BlockSpec((rows, g, w), lambda t: (nblk - 1 - t, 0, 0))

    st_spec = pl.BlockSpec((2, nseq, g, w), lambda t: (0, 0, 0, 0))
    res = pl.pallas_call(
        functools.partial(_s5_scan_kernel, n_in=n_in, nsb=nsb, jb=jb),
        grid=(nblk,),
        in_specs=[pl.BlockSpec((None, 2, 2, g, w), lambda t: (l, 0, 0, 0, 0)), st_spec]
                 + [xspec(0, r) for r in range(n_in)] + [xspec(1, r) for r in range(n_in)],
        out_specs=[sspec(0)] * n_in + [sspec(1)] * n_in + [st_spec],
        out_shape=[jax.ShapeDtypeStruct((nsb * nj, g, w), F32)] * (2 * n_in)
                  + [jax.ShapeDtypeStruct((2, nseq, g, w), F32)],
        scratch_shapes=[pltpu.VMEM((2, nseq, g, w), F32)],
        compiler_params=_cparams(1, 4 * n_in * _nbytes((rows, g, w), F32), 3 * _nbytes((2, nseq, g, w), F32)),
        name="s5_scan",
    )(a_chunk, h0, *([x] * (2 * n_in)))
    return res[:n_in], res[n_in:2 * n_in], res[2 * n_in]


def _s5_out_kernel(z_ref, pf_ref, cm_ref, qf_ref, d_ref, *rest, starts):
    n = len(starts)
    sf_refs, sb_refs = rest[:n], rest[n:2 * n]
    y_ref, m_scr, q_scr = rest[2 * n:]
    gpo = S5_GROUPS // S5_OCTETS

    @pl.when(pl.program_id(1) == 0)
    def _():
        same_group = _octet_group((LANES, LANES), 0) == _octet_group((LANES, LANES), 1)

        def lag_op(d, k):
            ab = pf_ref[d, S5_CHUNK - 1 - k] if d == 0 else pf_ref[d, k]
            ah, al = _split(ab)
            ch, cl = _split(cm_ref[d])
            return jnp.where(same_group, _dot_nt(ah, ch) + (_dot_nt(ah, cl) + _dot_nt(al, ch)), 0.0)

        fwd = [lag_op(0, k) for k in range(S5_CHUNK)]
        bwd = [lag_op(1, k) for k in range(S5_CHUNK)]
        blocks = {0: (fwd[0] + bwd[0]).astype(BF16)}
        for k in range(1, S5_CHUNK):
            blocks[k] = fwd[k].astype(BF16)
            blocks[-k] = bwd[k].astype(BF16)
        for t in range(S5_CHUNK):
            for t2 in range(S5_CHUNK):
                m_scr[t * LANES:(t + 1) * LANES, t2 * LANES:(t2 + 1) * LANES] = blocks[t2 - t]
        lane_group = _octet_group((LANES, LANES), 1)
        for d in range(2):
            for t in range(S5_CHUNK):
                blk = qf_ref[d, t]
                for g in range(gpo):
                    q_scr[d, g * LANES:(g + 1) * LANES, t * LANES:(t + 1) * LANES] = (
                        jnp.where(lane_group == g, blk, 0.0).astype(BF16))

    u = _s5_ucat(z_ref)
    rows = u.shape[0]
    width = gpo * 2 * S5_STATE

    def entering_state(refs):
        s = refs[0][...]
        for ref, start in zip(refs[1:], starts[1:]):
            s = jnp.where(pl.program_id(1) >= start, ref[...], s)
        return s.reshape(rows, width).astype(BF16)

    y = _dot(u, m_scr[...])
    y += _dot(entering_state(sf_refs), q_scr[0])
    y += _dot(entering_state(sb_refs), q_scr[1])
    for t in range(S5_CHUNK):
        y_ref[:, t, :] = y[:, t * LANES:(t + 1) * LANES] + z_ref[:, t, :] * d_ref[...]


def _s5_out(z3, pf, cmat, qf, d_oct, sin_f, sin_b, l, tr):
    r16 = z3.shape[0]
    gpo = S5_GROUPS // S5_OCTETS
    w = 2 * S5_STATE
    tiles = [s.shape[0] // tr for s in sin_f]
    starts = tuple(sum(tiles[:n]) for n in range(len(tiles)))
    sspecs = [pl.BlockSpec((tr, gpo, w), functools.partial(
        lambda o, i, start, count: (jnp.clip(i - start, 0, count - 1), o, 0), start=start, count=count))
        for start, count in zip(starts, tiles)]
    return pl.pallas_call(
        functools.partial(_s5_out_kernel, starts=starts),
        grid=(S5_OCTETS, r16 // tr),
        in_specs=[pl.BlockSpec((tr, S5_CHUNK, LANES), lambda o, i: (i, 0, S5_COL0 // LANES + o)),
                  pl.BlockSpec((None, 2, S5_CHUNK, LANES, w), lambda o, i: (l, 0, 0, o, 0)),
                  pl.BlockSpec((None, 2, LANES, w), lambda o, i: (l, 0, o, 0)),
                  pl.BlockSpec((None, 2, S5_CHUNK, w, LANES), lambda o, i: (l, 0, 0, 0, o)),
                  pl.BlockSpec((None, None, 1, LANES), lambda o, i: (l, o, 0, 0)),
                  *sspecs, *sspecs],
        out_specs=pl.BlockSpec((tr, S5_CHUNK, LANES), lambda o, i: (i, 0, o)),
        out_shape=jax.ShapeDtypeStruct((r16, S5_CHUNK, D_BRANCH), F32),
        scratch_shapes=[pltpu.VMEM((S5_OCT_K, S5_OCT_K), BF16), pltpu.VMEM((2, gpo * w, S5_OCT_K), BF16)],
        compiler_params=_cparams(2, 2 * _nbytes((tr, S5_CHUNK, LANES), F32), 2 * _nbytes((2, S5_CHUNK, LANES, w), F32),
                                 2 * len(tiles) * _nbytes((tr, gpo, w), F32),
                                 extra=_nbytes((S5_OCT_K, S5_OCT_K), BF16) + _nbytes((2, gpo * w, S5_OCT_K), BF16)
                                 + 4 * _nbytes((tr, S5_OCT_K), F32)),
        name="s5_out",
    )(z3, pf, cmat, qf, d_oct, *sin_f, *sin_b)


def _s5_glu_kernel(y_ref, w_ref, b_ref, o_ref, wb_ref):
    @pl.when(pl.program_id(0) == 0)
    def _():
        wb_ref[...] = w_ref[...].astype(BF16)

    y = _gelu(y_ref[...])
    o_ref[...] = (y * _sigmoid(_dot(y.astype(BF16), wb_ref[...]) + b_ref[...])).astype(BF16)


def _s5_glu(cfg, y, w, b, l):
    tm = min(512, cfg.ll)
    return pl.pallas_call(
        _s5_glu_kernel,
        grid=(cfg.r // tm,),
        in_specs=[pl.BlockSpec((tm, D_BRANCH), lambda i: (i, 0)),
                  pl.BlockSpec((None, D_BRANCH, D_BRANCH), lambda i: (l, 0, 0)),
                  pl.BlockSpec((None, 1, D_BRANCH), lambda i: (l, 0, 0))],
        out_specs=pl.BlockSpec((tm, D_BRANCH), lambda i: (i, 0)),
        out_shape=jax.ShapeDtypeStruct((cfg.r, D_BRANCH), BF16),
        scratch_shapes=[pltpu.VMEM((D_BRANCH, D_BRANCH), BF16)],
        compiler_params=_cparams(1, 2 * _nbytes((tm, D_BRANCH), F32), _nbytes((D_BRANCH, D_BRANCH), F32),
                                 extra=_nbytes((D_BRANCH, D_BRANCH), BF16) + 4 * _nbytes((tm, D_BRANCH), F32)),
        name="s5_glu",
    )(y, w, b.reshape(DEPTH, 1, D_BRANCH))


def _s5(cfg, s5m, z, p, h0_lat, l):
    pf, qf, cmat, a_chunk = s5m
    r16 = cfg.r // S5_CHUNK
    tr = min(256, cfg.ll // S5_CHUNK)
    z3 = z.reshape(r16, S5_CHUNK, N_IN)
    x = _s5_in(z3, pf, l, tr)
    njc, njl = cfg.lc // S5_CHUNK, cfg.ll // S5_CHUNK
    rc16 = cfg.rc // S5_CHUNK
    zero_h0 = jnp.zeros((2, cfg.nc, S5_GROUPS, 2 * S5_STATE), F32)
    sf_c, sb_c, fin_ctx = _s5_scan(x, a_chunk, zero_h0, l, 0, 1, cfg.nc, njc, njc)
    sf_l, sb_l, _ = _s5_scan(x, a_chunk, h0_lat, l, rc16, cfg.nl, 1, njl, min(32, njl))
    sin_f = list(sf_c) + list(sf_l)
    sin_b = list(sb_c) + list(sb_l)
    d_oct = p["s5_d"].reshape(DEPTH, S5_OCTETS, 1, LANES)
    y = _s5_out(z3, pf, cmat, qf, d_oct, sin_f, sin_b, l, tr).reshape(cfg.r, D_BRANCH)
    return _s5_glu(cfg, y, p["s5_glu_w"], p["s5_glu_b"], l), fin_ctx


def _forward(cfg, x_prompt, x_sample, c, cache_k, cache_v, state_ssm_re, state_ssm_im, c_ctx, p):
    d = D_MODEL
    x = (x_prompt.reshape(cfg.rc, d), x_sample.reshape(cfg.rl, d))
    cvec = jnp.concatenate([c_ctx[None], c, jnp.zeros((MOD_ROWS - cfg.nseg, d), F32)], axis=0)
    mod = _mod(cvec, p["w_mod"], p["b_mod"])[:, :cfg.nseg].reshape(DEPTH, cfg.nseg, 6, 1, d)
    mod = [[mod[l, :, i] for i in range(6)] for l in range(DEPTH)]
    norm_g = p["norm_g"]

    consts = {
        "dft": {},
        "feats": {n: _hy_feats(n) for n in {cfg.lc, cfg.ll}},
        "rope": _rope_tables(cfg.ll),
    }
    for n in {cfg.lc, cfg.ll}:
        consts["dft"][n] = {name: t.astype(BF16) for name, t in _dft_tables(n).items()}
    s5m = _s5_operands(p)

    kv_shape = (cfg.nc, cfg.lc, N_KV_HEADS, HEAD_DIM)
    ks, vs, srs, sis = [], [], [], []
    h = _resid_norm(cfg, x, gpre=norm_g[0, 0], sc=mod[0][1], sh=mod[0][0])
    for l in range(DEPTH):
        z = _mm(h, p["w_in"], l, tm=cfg.tm, tn=512)
        ks.append(z[:cfg.rc, K_COL0:K_COL0 + KV_COLS].reshape(kv_shape))
        vs.append(z[:cfg.rc, V_COL0:V_COL0 + KV_COLS].reshape(kv_shape))

        y_hy = _hyena(cfg, consts, z, p, l)
        sink = jnp.broadcast_to(p["attn_sink"][l][:, None], (N_Q_HEADS, LANES))
        past = cache_k.shape[2]
        y_at = jnp.concatenate([
            _attn_ctx(cfg, z, sink),
            _attn_lat(cfg, consts, z, cache_k[:, l].reshape(cfg.nl, past, KV_COLS),
                      cache_v[:, l].reshape(cfg.nl, past, KV_COLS), sink)], axis=0)
        h0 = jnp.concatenate([state_ssm_re[:, l], state_ssm_im[:, l]], axis=-1).transpose(1, 0, 2, 3)
        y_s5, fin = _s5(cfg, s5m, z, p, h0, l)
        srs.append(fin[..., :S5_STATE].transpose(1, 0, 2, 3))
        sis.append(fin[..., S5_STATE:].transpose(1, 0, 2, 3))

        merged = _merge(cfg, z, (y_hy, y_at, y_s5),
                        (p["w_branch_hy"], p["w_branch_attn"], p["w_branch_s5"]), l)
        y = _mm(merged, p["w_out"], l, tm=cfg.tm, tn=512)
        x, h = _resid_norm(cfg, x, y, gate=mod[l][2], gpost=norm_g[l, 1],
                           gpre=norm_g[l, 2], sc=mod[l][4], sh=mod[l][3])
        u = _mm(h, p["ffn_w_up"], l, tm=cfg.tm, tn=512)
        act = _ffn_act(cfg, u, p["ffn_conv_w"], p["ffn_conv_b"], l)
        f = _mm(act, p["ffn_w_down"], l, tm=min(512, cfg.tm), tn=512, w_buffers=1)
        if l + 1 < DEPTH:
            x, h = _resid_norm(cfg, x, f, gate=mod[l][5], gpost=norm_g[l, 3],
                               gpre=norm_g[l + 1, 0], sc=mod[l + 1][1], sh=mod[l + 1][0])
        else:
            x = _resid_norm(cfg, x, f, gate=mod[l][5], gpost=norm_g[l, 3], split_out=True)

    return (x[0].reshape(cfg.nc, cfg.lc, d), x[1].reshape(cfg.nl, cfg.ll, d),
            jnp.stack(ks, axis=1), jnp.stack(vs, axis=1), jnp.stack(srs, axis=1), jnp.stack(sis, axis=1))


def kernel(x_prompt, x_sample, c, cache_k, cache_v, state_ssm_re, state_ssm_im, c_ctx, w_mod, b_mod, norm_g, w_in, hy_conv_w, hy_conv_b, hy_w1, hy_b1, hy_w2, hy_b2, hy_w3, hy_freq, hy_decay, hy_bias, attn_sink, s5_lam_re, s5_lam_im, s5_log_dt, s5_b_re, s5_b_im, s5_c_re, s5_c_im, s5_d, s5_glu_w, s5_glu_b, w_branch_hy, w_branch_attn, w_branch_s5, w_out, ffn_w_up, ffn_conv_w, ffn_conv_b, ffn_w_down):
    p = dict(w_mod=w_mod, b_mod=b_mod, norm_g=norm_g, w_in=w_in, hy_conv_w=hy_conv_w, hy_conv_b=hy_conv_b,
             hy_w1=hy_w1, hy_b1=hy_b1, hy_w2=hy_w2, hy_b2=hy_b2, hy_w3=hy_w3, hy_freq=hy_freq,
             hy_decay=hy_decay, hy_bias=hy_bias, attn_sink=attn_sink, s5_lam_re=s5_lam_re,
             s5_lam_im=s5_lam_im, s5_log_dt=s5_log_dt, s5_b_re=s5_b_re, s5_b_im=s5_b_im, s5_c_re=s5_c_re,
             s5_c_im=s5_c_im, s5_d=s5_d, s5_glu_w=s5_glu_w, s5_glu_b=s5_glu_b, w_branch_hy=w_branch_hy,
             w_branch_attn=w_branch_attn, w_branch_s5=w_branch_s5, w_out=w_out, ffn_w_up=ffn_w_up,
             ffn_conv_w=ffn_conv_w, ffn_conv_b=ffn_conv_b, ffn_w_down=ffn_w_down)
    cfg = _Cfg(x_prompt.shape[0], x_prompt.shape[1], x_sample.shape[0], x_sample.shape[1])
    return _forward(cfg, x_prompt, x_sample, c, cache_k, cache_v, state_ssm_re, state_ssm_im, c_ctx, p)
```

```python
import functools
import math

import numpy as np
import jax
import jax.numpy as jnp
from jax import lax
from jax.experimental import pallas as pl
from jax.experimental.pallas import tpu as pltpu

F32 = jnp.float32
BF16 = jnp.bfloat16

D_MODEL = 4096
DEPTH = 2
GRID_W = 64
D_BRANCH = D_MODEL // 4
HY_WIDTH = D_BRANCH
HY_BANDS = 16
HY_FEAT = 1 + 2 * HY_BANDS
HY_HID = 64
HEAD_DIM = 128
N_Q_HEADS = D_BRANCH // HEAD_DIM
N_KV_HEADS = 2
GQA_GROUP = N_Q_HEADS // N_KV_HEADS
WINDOW = 128
BLOCK = 128
ROPE_BASE = 10000.0
S5_CH = 16
S5_GROUPS = D_BRANCH // S5_CH
S5_STATE = 64
S5_CHUNK = 16
S5_OCTETS = 8
S5_OCT_K = S5_CHUNK * 128
D_FF = 2 * D_MODEL
EPS = 1e-6
NEG_INF = -1e30

HY_COLS = 3 * HY_WIDTH
Q_COL0 = HY_COLS
K_COL0 = Q_COL0 + N_Q_HEADS * HEAD_DIM
KV_COLS = N_KV_HEADS * HEAD_DIM
V_COL0 = K_COL0 + KV_COLS
S5_COL0 = V_COL0 + KV_COLS
GATE_COL0 = S5_COL0 + D_BRANCH
N_IN = GATE_COL0 + 3 * D_MODEL

V7X_VMEM_REQUEST_MAX = 60 * 1024 * 1024
LANES = 128
SUBLANES = 8
MOD_ROWS = 8


def _nbytes(shape, dtype):
    return math.prod(shape) * jnp.dtype(dtype).itemsize


def _cparams(n_grid, *block_bytes, extra=0):
    est = 2 * sum(block_bytes) + extra + (4 << 20)
    return pltpu.CompilerParams(
        dimension_semantics=("arbitrary",) * n_grid,
        vmem_limit_bytes=int(min(max(est, 16 << 20), V7X_VMEM_REQUEST_MAX)))


def _dot(a, b):
    return jnp.dot(a, b, preferred_element_type=F32)


def _dot_nt(a, b):
    return lax.dot_general(a, b, (((1,), (1,)), ((), ())), preferred_element_type=F32)


def _split(a):
    hi = a.astype(BF16)
    return hi, (a - hi.astype(F32)).astype(BF16)


def _dot3(a, b):
    ah, al = _split(a)
    bh, bl = _split(b)
    return _dot(ah, bh) + (_dot(ah, bl) + _dot(al, bh))


def _sigmoid(x):
    return 1.0 / (1.0 + jnp.exp(-x))


def _gelu(x):
    c = math.sqrt(2.0 / math.pi)
    return x * (0.5 + 0.5 * jnp.tanh(x * (c + (c * 0.044715) * (x * x))))


def _rms(x, g):
    return x * lax.rsqrt(jnp.mean(x * x, axis=-1, keepdims=True) + EPS) * g


class _Cfg:
    def __init__(self, nc, lc, nl, ll):
        self.nc, self.lc, self.nl, self.ll = nc, lc, nl, ll
        self.rc = nc * lc
        self.rl = nl * ll
        self.r = self.rc + self.rl
        self.nseg = 1 + nl
        assert self.rc % ll == 0 and ll % lc == 0 and lc % BLOCK == 0
        self.tm = min(1024, ll)
        self.rb = ll

    def seg_of_row(self, row0):
        return jnp.where(row0 >= self.rc, (row0 - self.rc) // self.ll + 1, 0)


def _mod_kernel(c_ref, w_ref, b_ref, o_ref):
    c = c_ref[...]
    s = (c * _sigmoid(c)).astype(BF16)
    o_ref[...] = _dot(s, w_ref[...].astype(BF16)) + b_ref[...]


def _mod(cvec, w_mod, b_mod):
    n = w_mod.shape[-1]
    tn = 512
    return pl.pallas_call(
        _mod_kernel,
        grid=(DEPTH, n // tn),
        in_specs=[pl.BlockSpec((MOD_ROWS, D_MODEL), lambda l, j: (0, 0)),
                  pl.BlockSpec((None, D_MODEL, tn), lambda l, j: (l, 0, j)),
                  pl.BlockSpec((None, 1, tn), lambda l, j: (l, 0, j))],
        out_specs=pl.BlockSpec((None, MOD_ROWS, tn), lambda l, j: (l, 0, j)),
        out_shape=jax.ShapeDtypeStruct((DEPTH, MOD_ROWS, n), F32),
        compiler_params=_cparams(2, _nbytes((D_MODEL, tn), F32), extra=_nbytes((D_MODEL, tn), BF16)),
        name="mod",
    )(cvec, w_mod, b_mod.reshape(DEPTH, 1, n))


def _resid_norm_kernel(*refs, has_y, has_h, split_in, split_out, n_ctx_tiles):
    it = iter(refs)
    in_ctx = pl.program_id(0) < n_ctx_tiles
    if split_in:
        xa_ref, xb_ref = next(it), next(it)
        x = jnp.where(in_ctx, xa_ref[...], xb_ref[...])
    else:
        x = next(it)[...]
    if has_y:
        y_ref, gate_ref, gpost_ref = next(it), next(it), next(it)
    if has_h:
        gpre_ref, sc_ref, sh_ref = next(it), next(it), next(it)
    if has_y:
        x = x + gate_ref[...] * _rms(y_ref[...], gpost_ref[...])
        if split_out:
            xa_out, xb_out = next(it), next(it)

            @pl.when(in_ctx)
            def _():
                xa_out[...] = x

            @pl.when(jnp.logical_not(in_ctx))
            def _():
                xb_out[...] = x
        else:
            xo_ref = next(it)
            xo_ref[...] = x
    if has_h:
        ho_ref = next(it)
        ho_ref[...] = (_rms(x, gpre_ref[...]) * (1.0 + sc_ref[...]) + sh_ref[...]).astype(BF16)


def _resid_norm(cfg, x, y=None, gate=None, gpost=None, gpre=None, sc=None, sh=None, split_out=False):
    has_y, has_h, split_in = y is not None, gpre is not None, isinstance(x, tuple)
    tm = 256
    nca = cfg.rc // tm
    row = pl.BlockSpec((tm, D_MODEL), lambda i: (i, 0))
    row_a = pl.BlockSpec((tm, D_MODEL), lambda i: (jnp.minimum(i, nca - 1), 0))
    row_b = pl.BlockSpec((tm, D_MODEL), lambda i: (jnp.maximum(i - nca, 0), 0))
    vec = pl.BlockSpec((1, D_MODEL), lambda i: (0, 0))
    seg = pl.BlockSpec((None, 1, D_MODEL), lambda i: (cfg.seg_of_row(i * tm), 0, 0))
    args, specs = (list(x), [row_a, row_b]) if split_in else ([x], [row])
    outs, ospecs = [], []
    if has_y:
        args += [y, gate, gpost.reshape(1, D_MODEL)]
        specs += [row, seg, vec]
        if split_out:
            outs += [jax.ShapeDtypeStruct((cfg.rc, D_MODEL), F32), jax.ShapeDtypeStruct((cfg.rl, D_MODEL), F32)]
            ospecs += [row_a, row_b]
        else:
            outs.append(jax.ShapeDtypeStruct((cfg.r, D_MODEL), F32))
            ospecs.append(row)
    if has_h:
        args += [gpre.reshape(1, D_MODEL), sc, sh]
        specs += [vec, seg, seg]
        outs.append(jax.ShapeDtypeStruct((cfg.r, D_MODEL), BF16))
        ospecs.append(row)
    res = pl.pallas_call(
        functools.partial(_resid_norm_kernel, has_y=has_y, has_h=has_h, split_in=split_in, split_out=split_out,
                          n_ctx_tiles=nca),
        grid=(cfg.r // tm,),
        in_specs=specs, out_specs=ospecs, out_shape=outs,
        compiler_params=_cparams(1, 5 * _nbytes((tm, D_MODEL), F32)),
        name="resid_norm",
    )(*args)
    return res if len(res) > 1 else res[0]


def _mm_kernel(x_ref, w_ref, o_ref, wb_ref):
    @pl.when(pl.program_id(1) == 0)
    def _():
        wb_ref[...] = w_ref[...].astype(BF16)

    o_ref[...] = _dot(x_ref[...], wb_ref[...]).astype(o_ref.dtype)


def _mm(x, w, l, *, tm, tn, out_dtype=F32, w_buffers=2):
    m, k = x.shape
    n = w.shape[-1]
    w_mode = {} if w_buffers == 2 else {"pipeline_mode": pl.Buffered(w_buffers)}
    return pl.pallas_call(
        _mm_kernel,
        grid=(n // tn, m // tm),
        in_specs=[pl.BlockSpec((tm, k), lambda j, i: (i, 0)),
                  pl.BlockSpec((None, k, tn), lambda j, i: (l, 0, j), **w_mode)],
        out_specs=pl.BlockSpec((tm, tn), lambda j, i: (i, j)),
        out_shape=jax.ShapeDtypeStruct((m, n), out_dtype),
        scratch_shapes=[pltpu.VMEM((k, tn), BF16)],
        compiler_params=_cparams(2, _nbytes((tm, k), BF16), _nbytes((tm, tn), out_dtype),
                                 extra=w_buffers * _nbytes((k, tn), F32) + _nbytes((k, tn), BF16)
                                 + _nbytes((tm, tn), F32)),
        name="mm",
    )(x, w)


def _merge_kernel(g0, g1, g2, y0, y1, y2, w0, w1, w2, o_ref, wb_ref):
    @pl.when(pl.program_id(1) == 0)
    def _():
        wb_ref[0] = w0[...].astype(BF16)
        wb_ref[1] = w1[...].astype(BF16)
        wb_ref[2] = w2[...].astype(BF16)

    acc = _sigmoid(g0[...]) * _dot(y0[...], wb_ref[0])
    acc += _sigmoid(g1[...]) * _dot(y1[...], wb_ref[1])
    acc += _sigmoid(g2[...]) * _dot(y2[...], wb_ref[2])
    o_ref[...] = acc.astype(o_ref.dtype)


def _merge(cfg, z, ys, ws, l):
    tm, tn = cfg.tm, 512
    gate_specs = [pl.BlockSpec((tm, tn), functools.partial(
        lambda j, i, b: (i, (GATE_COL0 + b * D_MODEL) // tn + j), b=b)) for b in range(3)]
    y_spec = pl.BlockSpec((tm, D_BRANCH), lambda j, i: (i, 0))
    w_spec = pl.BlockSpec((None, D_BRANCH, tn), lambda j, i: (l, 0, j))
    return pl.pallas_call(
        _merge_kernel,
        grid=(D_MODEL // tn, cfg.r // tm),
        in_specs=gate_specs + [y_spec] * 3 + [w_spec] * 3,
        out_specs=pl.BlockSpec((tm, tn), lambda j, i: (i, j)),
        out_shape=jax.ShapeDtypeStruct((cfg.r, D_MODEL), BF16),
        scratch_shapes=[pltpu.VMEM((3, D_BRANCH, tn), BF16)],
        compiler_params=_cparams(2, 3 * _nbytes((tm, tn), F32), 3 * _nbytes((tm, D_BRANCH), BF16),
                                 3 * _nbytes((D_BRANCH, tn), F32), _nbytes((tm, tn), BF16),
                                 extra=3 * _nbytes((D_BRANCH, tn), BF16) + 2 * _nbytes((tm, tn), F32)),
        name="merge",
    )(z, z, z, *ys, *ws)


def _block_seq_len(cfg):
    return jnp.where(pl.program_id(0) < cfg.rc // cfg.rb, cfg.lc, cfg.ll)


def _dwconv3_rows(x_ref, w_ref, b_ref, r0, sl):
    total = x_ref.shape[0]
    cur = x_ref[r0:r0 + BLOCK, :]
    sub = lax.broadcasted_iota(jnp.int32, (SUBLANES, cur.shape[1]), 0)
    xm = x_ref[r0 - 1:r0 - 1 + BLOCK, :] if r0 > 0 else pltpu.roll(cur, 1, 0)
    first = jnp.where(((r0 + sub) & (sl - 1)) == 0, 0.0, xm[:SUBLANES])
    xm = jnp.concatenate([first, xm[SUBLANES:]], axis=0)
    xp = x_ref[r0 + 1:r0 + 1 + BLOCK, :] if r0 + BLOCK < total else pltpu.roll(cur, BLOCK - 1, 0)
    last = jnp.where(((r0 + BLOCK - SUBLANES + sub) & (sl - 1)) == sl - 1, 0.0, xp[BLOCK - SUBLANES:])
    xp = jnp.concatenate([xp[:BLOCK - SUBLANES], last], axis=0)
    return xm * w_ref[0:1, :] + cur * w_ref[1:2, :] + xp * w_ref[2:3, :] + b_ref[...]


def _hy_dwconv_kernel(x_ref, w_ref, b_ref, oe_ref, oo_ref, *, cfg):
    sl = _block_seq_len(cfg)
    half = x_ref.shape[0] // 2
    xe = x_ref[pl.ds(0, half, stride=2), :]
    xo = x_ref[pl.ds(1, half, stride=2), :]
    m = lax.broadcasted_iota(jnp.int32, xe.shape, 0)
    before_even = jnp.where(((2 * m) & (sl - 1)) == 0, 0.0, pltpu.roll(xo, 1, 0))
    after_odd = jnp.where(((2 * m + 1) & (sl - 1)) == sl - 1, 0.0, pltpu.roll(xe, half - 1, 0))
    w0, w1, w2 = w_ref[0:1, :], w_ref[1:2, :], w_ref[2:3, :]
    oe_ref[...] = before_even * w0 + xe * w1 + xo * w2 + b_ref[...]
    oo_ref[...] = xe * w0 + xo * w1 + after_odd * w2 + b_ref[...]


def _hy_dwconv(cfg, z, w, b, l):
    tn = LANES
    blk = _nbytes((cfg.rb, tn), F32)
    half = pl.BlockSpec((cfg.rb // 2, tn), lambda r, j: (r, j))
    return pl.pallas_call(
        functools.partial(_hy_dwconv_kernel, cfg=cfg),
        grid=(cfg.r // cfg.rb, HY_COLS // tn),
        in_specs=[pl.BlockSpec((cfg.rb, tn), lambda r, j: (r, j)),
                  pl.BlockSpec((None, 3, tn), lambda r, j: (l, 0, j)),
                  pl.BlockSpec((None, 1, tn), lambda r, j: (l, 0, j))],
        out_specs=[half, half],
        out_shape=[jax.ShapeDtypeStruct((cfg.r // 2, HY_COLS), F32)] * 2,
        compiler_params=_cparams(2, 2 * blk, extra=4 * blk),
        name="hy_dwconv",
    )(z, w, b.reshape(DEPTH, 1, HY_COLS))


def _ffn_act_kernel(a_ref, b_ref, wa_ref, wb_ref, ba_ref, bb_ref, o_ref, *, cfg):
    sl = _block_seq_len(cfg)
    for r0 in range(0, a_ref.shape[0], BLOCK):
        a = _dwconv3_rows(a_ref, wa_ref, ba_ref, r0, sl)
        b = _dwconv3_rows(b_ref, wb_ref, bb_ref, r0, sl)
        o_ref[r0:r0 + BLOCK, :] = (_gelu(a) * b).astype(BF16)


def _ffn_act(cfg, u, w, b, l):
    tn = 256
    nb = D_FF // tn
    blk = _nbytes((cfg.rb, tn), F32)
    b3 = b.reshape(DEPTH, 1, 2 * D_FF)
    return pl.pallas_call(
        functools.partial(_ffn_act_kernel, cfg=cfg),
        grid=(cfg.r // cfg.rb, nb),
        in_specs=[pl.BlockSpec((cfg.rb, tn), lambda r, j: (r, j)),
                  pl.BlockSpec((cfg.rb, tn), lambda r, j: (r, j + nb)),
                  pl.BlockSpec((None, 3, tn), lambda r, j: (l, 0, j)),
                  pl.BlockSpec((None, 3, tn), lambda r, j: (l, 0, j + nb)),
                  pl.BlockSpec((None, 1, tn), lambda r, j: (l, 0, j)),
                  pl.BlockSpec((None, 1, tn), lambda r, j: (l, 0, j + nb))],
        out_specs=pl.BlockSpec((cfg.rb, tn), lambda r, j: (r, j)),
        out_shape=jax.ShapeDtypeStruct((cfg.r, D_FF), BF16),
        compiler_params=_cparams(2, 3 * blk, extra=6 * blk),
        name="ffn_act",
    )(u, u, w, w, b3, b3)


def _dft_tables(length):
    half = length // 2
    period = 4 * length
    r = 1 << (half.bit_length() // 2)
    idx = np.arange(half, dtype=np.int64)
    hi = np.arange(half // r, dtype=np.int64) * r
    lo = np.arange(r, dtype=np.int64)

    def cs(m):
        ang = 2.0 * np.pi * (m % period).astype(np.float64) / period
        return jnp.asarray(np.cos(ang), F32), jnp.asarray(np.sin(ang), F32)

    def combine(phase_hi, phase_lo):
        ca, sa = cs(phase_hi)
        cb, sb = cs(phase_lo)
        c = ca[:, :, None] * cb[:, None, :] - sa[:, :, None] * sb[:, None, :]
        s = sa[:, :, None] * cb[:, None, :] + ca[:, :, None] * sb[:, None, :]
        return c.reshape(half, half), s.reshape(half, half)

    odd = (2 * idx + 1)[:, None]
    row = idx[:, None]
    out = {}
    out["ce"], out["se"] = combine(odd * (2 * hi)[None, :], odd * (2 * lo)[None, :])
    out["co"], out["so"] = combine(odd * (2 * hi)[None, :], odd * (2 * lo + 1)[None, :])
    out["cet"], out["set"] = combine(2 * row * (2 * hi)[None, :], 2 * row * (2 * lo + 1)[None, :])
    out["cot"], out["sot"] = combine((2 * row + 1) * (2 * hi)[None, :], (2 * row + 1) * (2 * lo + 1)[None, :])
    return out


def _hy_feats(length):
    pos = jnp.arange(length, dtype=F32)
    t = (pos / length)[:, None]
    bands = jnp.linspace(1e-4, HY_BANDS - 1, HY_BANDS, dtype=F32)
    wpos = 2.0 * math.pi * t * bands
    feats = jnp.concatenate([t, jnp.cos(wpos), -jnp.sin(wpos)], axis=-1)
    feats = jnp.concatenate([feats[0::2], feats[1::2]], axis=0)
    return jnp.pad(feats, ((0, 0), (0, LANES - HY_FEAT)))


def _hy_filter_kernel(f_ref, w1_ref, b1_ref, w2_ref, b2_ref, w3_ref, fr_ref, dec_ref, hs_ref, hd_ref, hb0_ref):
    feats = f_ref[...]
    fr = fr_ref[...]
    h = jnp.sin(fr * (_dot3(feats, w1_ref[...]) + b1_ref[...]))
    h = jnp.sin(fr * (_dot3(h, w2_ref[...]) + b2_ref[...]))
    h = _dot3(h, w3_ref[...]) * jnp.exp(-feats[:, 0:1] * jnp.abs(dec_ref[...]))
    half = 2 * HY_WIDTH
    hf, hb = h[:, :half], h[:, half:]
    hs_ref[...] = (hf + hb).astype(BF16)
    hd_ref[...] = (hb - hf).astype(BF16)

    @pl.when(pl.program_id(0) == 0)
    def _():
        hb0_ref[...] = jnp.broadcast_to(hb[0:1, :], hb0_ref.shape)


def _hy_filter(length, feats, w1p, b1, w2, b2, w3, freq, decay, l):
    tl = 256
    half = 2 * HY_WIDTH
    lsel = lambda i: (l, 0, 0)
    return pl.pallas_call(
        _hy_filter_kernel,
        grid=(length // tl,),
        in_specs=[pl.BlockSpec((tl, LANES), lambda i: (i, 0)),
                  pl.BlockSpec((None, LANES, HY_HID), lsel),
                  pl.BlockSpec((None, 1, HY_HID), lsel),
                  pl.BlockSpec((None, HY_HID, HY_HID), lsel),
                  pl.BlockSpec((None, 1, HY_HID), lsel),
                  pl.BlockSpec((None, HY_HID, 2 * half), lsel),
                  pl.BlockSpec((None, 1, HY_HID), lsel),
                  pl.BlockSpec((None, 1, 2 * half), lsel)],
        out_specs=[pl.BlockSpec((tl, half), lambda i: (i, 0)),
                   pl.BlockSpec((tl, half), lambda i: (i, 0)),
                   pl.BlockSpec((8, half), lambda i: (0, 0))],
        out_shape=[jax.ShapeDtypeStruct((length, half), BF16),
                   jax.ShapeDtypeStruct((length, half), BF16),
                   jax.ShapeDtypeStruct((8, half), F32)],
        compiler_params=_cparams(1, 2 * _nbytes((tl, half), F32), extra=6 * _nbytes((tl, 2 * half), F32)),
        name="hy_filter",
    )(feats, w1p, b1, w2, b2, w3, freq, decay)


def _hy_spec_kernel(ce, se, co, so, hse, hso, hde, hdo, hb0_ref, kar, kai, kbr, kbi):
    a = _dot(ce[...], hse[...])
    b = _dot(co[...], hso[...])
    c = _dot(se[...], hde[...])
    d = _dot(so[...], hdo[...])
    kar[...] = a + b - hb0_ref[0:1, :]
    kai[...] = c + d
    kbr[...] = a - b - hb0_ref[0:1, :]
    kbi[...] = d - c


def _hy_spec(length, mats, hs, hd, hb0):
    half_len = length // 2
    tk = min(512, half_len)
    tn = 512
    width = 2 * HY_WIDTH
    ncb = width // tn
    fspec = pl.BlockSpec((tk, half_len), lambda j, i: (i, 0))
    espec = pl.BlockSpec((half_len, tn), lambda j, i: (0, j))
    ospec = pl.BlockSpec((half_len, tn), lambda j, i: (1, j))
    kspec = pl.BlockSpec((tk, tn), lambda j, i: (i, j))
    return pl.pallas_call(
        _hy_spec_kernel,
        grid=(ncb, half_len // tk),
        in_specs=[fspec] * 4 + [espec, ospec, espec, ospec, pl.BlockSpec((8, tn), lambda j, i: (0, j))],
        out_specs=[kspec] * 4,
        out_shape=[jax.ShapeDtypeStruct((half_len, width), F32)] * 4,
        compiler_params=_cparams(2, 4 * _nbytes((tk, half_len), BF16), 4 * _nbytes((half_len, tn), BF16),
                                 4 * _nbytes((tk, tn), F32), extra=4 * _nbytes((tk, tn), F32)),
        name="hy_spec",
    )(mats["ce"], mats["se"], mats["co"], mats["so"], hs, hs, hd, hd, hb0)


def _hy_fwd_kernel(ce, se, co, so, ue_ref, uo_ref, kar, kai, kbr, kbi, pr_ref, pi_ref, qr_ref, qi_ref, ub_ref):
    @pl.when(pl.program_id(2) == 0)
    def _():
        ub_ref[0] = ue_ref[...].astype(BF16)
        ub_ref[1] = uo_ref[...].astype(BF16)

    e_re, e_im = _dot(ce[...], ub_ref[0]), -_dot(se[...], ub_ref[0])
    o_re, o_im = _dot(co[...], ub_ref[1]), -_dot(so[...], ub_ref[1])
    ua_re, ua_im = e_re + o_re, e_im + o_im
    ub_re, ub_im = e_re - o_re, o_im - e_im
    ya_re = ua_re * kar[...] - ua_im * kai[...]
    ya_im = ua_re * kai[...] + ua_im * kar[...]
    yb_re = ub_re * kbr[...] - ub_im * kbi[...]
    yb_im = ub_re * kbi[...] + ub_im * kbr[...]
    pr_ref[...] = (ya_re + yb_re).astype(BF16)
    pi_ref[...] = (ya_im - yb_im).astype(BF16)
    qr_ref[...] = (ya_re - yb_re).astype(BF16)
    qi_ref[...] = (ya_im + yb_im).astype(BF16)


def _hy_inv_kernel(cet, set_, cot, sot, pr, pi_, qr, qi, ue_ref, uo_ref, ge_ref, go_ref, bias_ref, ye_ref, yo_ref,
                   *, inv_len):
    even = _dot(cet[...], pr[...]) - _dot(set_[...], pi_[...])
    odd = _dot(cot[...], qr[...]) - _dot(sot[...], qi[...])
    ye_ref[...] = (ge_ref[...] * (even * inv_len + ue_ref[...] * bias_ref[...])).astype(ye_ref.dtype)
    yo_ref[...] = (go_ref[...] * (odd * inv_len + uo_ref[...] * bias_ref[...])).astype(yo_ref.dtype)


def _hy_conv(length, nseq, mats, spec, order, data, gate, bias, l, out_dtype):
    half_len = length // 2
    tn = min(512, HY_WIDTH)
    tk = min(512, half_len)
    ncb = HY_WIDTH // tn
    ntk = half_len // tk
    fspec = pl.BlockSpec((tk, half_len), lambda s, j, i: (i, 0))
    kspec = pl.BlockSpec((tk, tn), lambda s, j, i: (i, order * ncb + j))
    tile = pl.BlockSpec((tk, tn), lambda s, j, i: (s * ntk + i, j))

    def full(loc):
        _, rb0, cb0 = loc
        return pl.BlockSpec((half_len, tn), lambda s, j, i: (rb0 + s, cb0 + j))

    def rows(loc):
        _, rb0, cb0 = loc
        return pl.BlockSpec((tk, tn), lambda s, j, i: ((rb0 + s) * ntk + i, cb0 + j))

    pq = pl.pallas_call(
        _hy_fwd_kernel,
        grid=(nseq, ncb, ntk),
        in_specs=[fspec] * 4 + [full(data[0]), full(data[1])] + [kspec] * 4,
        out_specs=[tile] * 4,
        out_shape=[jax.ShapeDtypeStruct((nseq * half_len, HY_WIDTH), BF16)] * 4,
        scratch_shapes=[pltpu.VMEM((2, half_len, tn), BF16)],
        compiler_params=_cparams(3, 4 * _nbytes((tk, half_len), BF16), 2 * _nbytes((half_len, tn), F32),
                                 4 * _nbytes((tk, tn), F32), 4 * _nbytes((tk, tn), BF16),
                                 extra=2 * _nbytes((half_len, tn), BF16) + 12 * _nbytes((tk, tn), F32)),
        name="hy_fwd",
    )(mats["ce"], mats["se"], mats["co"], mats["so"], data[0][0], data[1][0], *spec)

    resident = pl.BlockSpec((half_len, tn), lambda s, j, i: (s, j))
    return pl.pallas_call(
        functools.partial(_hy_inv_kernel, inv_len=1.0 / length),
        grid=(nseq, ncb, ntk),
        in_specs=[fspec] * 4 + [resident] * 4 + [rows(data[0]), rows(data[1]), rows(gate[0]), rows(gate[1]),
                                                 pl.BlockSpec((None, None, 1, tn), lambda s, j, i: (l, order, 0, j))],
        out_specs=[tile, tile],
        out_shape=[jax.ShapeDtypeStruct((nseq * half_len, HY_WIDTH), out_dtype)] * 2,
        compiler_params=_cparams(3, 4 * _nbytes((tk, half_len), BF16), 4 * _nbytes((half_len, tn), BF16),
                                 6 * _nbytes((tk, tn), F32), extra=6 * _nbytes((tk, tn), F32)),
        name="hy_inv",
    )(mats["cet"], mats["set"], mats["cot"], mats["sot"], *pq, data[0][0], data[1][0], gate[0][0], gate[1][0], bias)


def _hyena(cfg, consts, z, p, l):
    zce, zco = _hy_dwconv(cfg, z, p["hy_conv_w"], p["hy_conv_b"], l)
    bias = p["hy_bias"].reshape(DEPTH, 2, 1, HY_WIDTH)
    w1p = jnp.pad(p["hy_w1"], ((0, 0), (0, LANES - HY_FEAT), (0, 0)))
    tn = min(512, HY_WIDTH)
    outs = []
    for length, nseq, row0 in ((cfg.lc, cfg.nc, 0), (cfg.ll, cfg.nl, cfg.rc)):
        mats = consts["dft"][length]
        hs, hd, hb0 = _hy_filter(length, consts["feats"][length], w1p, p["hy_b1"].reshape(DEPTH, 1, HY_HID),
                                 p["hy_w2"], p["hy_b2"].reshape(DEPTH, 1, HY_HID), p["hy_w3"],
                                 p["hy_freq"].reshape(DEPTH, 1, HY_HID),
                                 p["hy_decay"].reshape(DEPTH, 1, 4 * HY_WIDTH), l)
        spec = _hy_spec(length, mats, hs, hd, hb0)
        rb0 = row0 // length

        def zc_cols(col0):
            return (zce, rb0, col0 // tn), (zco, rb0, col0 // tn)

        y1 = _hy_conv(length, nseq, mats, spec, 0, zc_cols(0), zc_cols(HY_WIDTH), bias, l, F32)
        y2 = _hy_conv(length, nseq, mats, spec, 1, ((y1[0], 0, 0), (y1[1], 0, 0)), zc_cols(2 * HY_WIDTH), bias, l,
                      BF16)
        outs.append(jnp.stack(y2, axis=1).reshape(nseq * length, HY_WIDTH))
    return jnp.concatenate(outs, axis=0)


def _softmax_sink_pv(parts, sink):
    m = sink
    for s, _ in parts:
        m = jnp.maximum(m, jnp.max(s, axis=-1, keepdims=True))
    den = jnp.exp(sink - m)
    acc = None
    for s, v in parts:
        e = jnp.exp(s - m)
        den = den + jnp.sum(e, axis=-1, keepdims=True)
        pv = _dot(e.astype(BF16), v)
        acc = pv if acc is None else acc + pv
    return acc / den


def _group_sink(sink_ref, h):
    return jnp.concatenate([jnp.broadcast_to(sink_ref[hq:hq + 1, 0:1], (BLOCK, 1))
                            for hq in range(h * GQA_GROUP, (h + 1) * GQA_GROUP)], axis=0)


def _store_group(o_ref, h, o):
    for g in range(GQA_GROUP):
        hq = h * GQA_GROUP + g
        o_ref[:, hq * HEAD_DIM:(hq + 1) * HEAD_DIM] = o[g * BLOCK:(g + 1) * BLOCK].astype(BF16)


def _attn_ctx_kernel(q_ref, k_ref, v_ref, sink_ref, o_ref):
    scale = HEAD_DIM ** -0.5
    for h in range(N_KV_HEADS):
        hs = slice(h * HEAD_DIM, (h + 1) * HEAD_DIM)
        k = k_ref[:, hs].astype(BF16)
        v = v_ref[:, hs].astype(BF16)
        q = jnp.concatenate([q_ref[:, hq * HEAD_DIM:(hq + 1) * HEAD_DIM]
                             for hq in range(h * GQA_GROUP, (h + 1) * GQA_GROUP)], axis=0).astype(BF16)
        s = _dot_nt(q, k) * scale
        _store_group(o_ref, h, _softmax_sink_pv([(s, v)], _group_sink(sink_ref, h)))


def _attn_ctx(cfg, z, sink):
    nb = cfg.lc // BLOCK
    qw = N_Q_HEADS * HEAD_DIM
    return pl.pallas_call(
        _attn_ctx_kernel,
        grid=(cfg.nc, nb),
        in_specs=[pl.BlockSpec((BLOCK, qw), lambda s, i: (s * nb + i, Q_COL0 // qw)),
                  pl.BlockSpec((cfg.lc, KV_COLS), lambda s, i: (s, K_COL0 // KV_COLS)),
                  pl.BlockSpec((cfg.lc, KV_COLS), lambda s, i: (s, V_COL0 // KV_COLS)),
                  pl.BlockSpec((N_Q_HEADS, LANES), lambda s, i: (0, 0))],
        out_specs=pl.BlockSpec((BLOCK, qw), lambda s, i: (s * nb + i, 0)),
        out_shape=jax.ShapeDtypeStruct((cfg.rc, qw), BF16),
        compiler_params=_cparams(2, _nbytes((BLOCK, qw), F32), 2 * _nbytes((cfg.lc, KV_COLS), F32)),
        name="attn_ctx",
    )(z, z, z, sink)


def _rope(x, c, s):
    return x * c + pltpu.roll(x, HEAD_DIM // 2, 1) * s


def _attn_lat_kernel(q_ref, kp_ref, kc_ref, kn_ref, vp_ref, vc_ref, vn_ref, ck_ref, cv_ref,
                     cq, sq, cp, sp, cn, sn, sink_ref, o_ref, *, nb):
    i = pl.program_id(1)
    scale = HEAD_DIM ** -0.5
    shape = (GQA_GROUP * BLOCK, 3 * BLOCK)
    qi = lax.broadcasted_iota(jnp.int32, shape, 0) & (BLOCK - 1)
    kj = lax.broadcasted_iota(jnp.int32, shape, 1)
    ok = jnp.abs(kj - BLOCK - qi) <= WINDOW
    ok = ok & ((kj >= BLOCK) | (i > 0)) & ((kj < 2 * BLOCK) | (i < nb - 1))
    for h in range(N_KV_HEADS):
        hs = slice(h * HEAD_DIM, (h + 1) * HEAD_DIM)
        kw = jnp.concatenate([_rope(kp_ref[:, hs], cp[...], sp[...]),
                              _rope(kc_ref[:, hs], cq[...], sq[...]),
                              _rope(kn_ref[:, hs], cn[...], sn[...])], axis=0).astype(BF16)
        vw = jnp.concatenate([vp_ref[:, hs], vc_ref[:, hs], vn_ref[:, hs]], axis=0).astype(BF16)
        kctx = ck_ref[:, hs].astype(BF16)
        vctx = cv_ref[:, hs].astype(BF16)
        q = jnp.concatenate([_rope(q_ref[:, hq * HEAD_DIM:(hq + 1) * HEAD_DIM], cq[...], sq[...])
                             for hq in range(h * GQA_GROUP, (h + 1) * GQA_GROUP)], axis=0).astype(BF16)
        s1 = _dot_nt(q, kctx) * scale
        s2 = jnp.where(ok, _dot_nt(q, kw) * scale, NEG_INF)
        _store_group(o_ref, h, _softmax_sink_pv([(s1, vctx), (s2, vw)], _group_sink(sink_ref, h)))


def _attn_lat(cfg, consts, z, ck, cv, sink):
    nb = cfg.ll // BLOCK
    rb0 = cfg.rc // BLOCK
    qw = N_Q_HEADS * HEAD_DIM
    past = ck.shape[1]
    cos2, sin2 = consts["rope"]

    def blk(delta):
        return lambda b, i: rb0 + b * nb + jnp.clip(i + delta, 0, nb - 1)

    def zspec(width, col0, delta):
        rowf = blk(delta)
        return pl.BlockSpec((BLOCK, width), lambda b, i: (rowf(b, i), col0 // width))

    def tspec(delta):
        return pl.BlockSpec((BLOCK, HEAD_DIM), lambda b, i: (jnp.clip(i + delta, 0, nb - 1), 0))

    cspec = pl.BlockSpec((None, past, KV_COLS), lambda b, i: (b, 0, 0))
    return pl.pallas_call(
        functools.partial(_attn_lat_kernel, nb=nb),
        grid=(cfg.nl, nb),
        in_specs=[zspec(qw, Q_COL0, 0),
                  zspec(KV_COLS, K_COL0, -1), zspec(KV_COLS, K_COL0, 0), zspec(KV_COLS, K_COL0, 1),
                  zspec(KV_COLS, V_COL0, -1), zspec(KV_COLS, V_COL0, 0), zspec(KV_COLS, V_COL0, 1),
                  cspec, cspec,
                  tspec(0), tspec(0), tspec(-1), tspec(-1), tspec(1), tspec(1),
                  pl.BlockSpec((N_Q_HEADS, LANES), lambda b, i: (0, 0))],
        out_specs=pl.BlockSpec((BLOCK, qw), lambda b, i: (b * nb + i, 0)),
        out_shape=jax.ShapeDtypeStruct((cfg.rl, qw), BF16),
        compiler_params=_cparams(2, _nbytes((BLOCK, qw), F32), 6 * _nbytes((BLOCK, KV_COLS), F32),
                                 2 * _nbytes((past, KV_COLS), F32)),
        name="attn_lat",
    )(z, z, z, z, z, z, z, ck, cv, cos2, sin2, cos2, sin2, cos2, sin2, sink)


def _rope_tables(length):
    rows = length // GRID_W
    row = jnp.repeat(jnp.arange(rows, dtype=F32), GRID_W)
    col = jnp.tile(jnp.arange(GRID_W, dtype=F32), rows)
    n_freq = HEAD_DIM // 4
    inv = ROPE_BASE ** (-jnp.arange(n_freq, dtype=F32) / n_freq)
    ang = jnp.concatenate([row[:, None] * inv, col[:, None] * inv], axis=-1)
    cos, sin = jnp.cos(ang), jnp.sin(ang)
    return jnp.concatenate([cos, cos], axis=-1), jnp.concatenate([-sin, sin], axis=-1)


def _s5_operands(p):
    t_len = S5_CHUNK
    lr, li = p["s5_lam_re"].astype(F32), p["s5_lam_im"].astype(F32)
    n_layers = lr.shape[0]
    dt = jnp.exp(p["s5_log_dt"].astype(F32))[..., None]
    zr, zi = lr * dt, li * dt
    mag = jnp.exp(zr)
    ar, ai = mag * jnp.cos(zi), mag * jnp.sin(zi)
    den = lr * lr + li * li
    cr = ((ar - 1.0) * lr + ai * li) / den
    ci = (ai * lr - (ar - 1.0) * li) / den
    k = jnp.arange(t_len + 1, dtype=F32)[:, None, None, None, None]
    pm = jnp.exp(k * zr)
    pr, pi = pm * jnp.cos(k * zi), pm * jnp.sin(k * zi)

    bt_re = p["s5_b_re"].astype(F32).transpose(0, 1, 2, 4, 3)
    bt_im = p["s5_b_im"].astype(F32).transpose(0, 1, 2, 4, 3)
    bbr = cr[:, :, :, None, :] * bt_re - ci[:, :, :, None, :] * bt_im
    bbi = cr[:, :, :, None, :] * bt_im + ci[:, :, :, None, :] * bt_re
    pk_r, pk_i = pr[:t_len, :, :, :, None, :], pi[:t_len, :, :, :, None, :]
    ab = jnp.concatenate([pk_r * bbr - pk_i * bbi, pk_r * bbi + pk_i * bbr], axis=-1)
    ab = ab.reshape(t_len, n_layers, 2, D_BRANCH, 2 * S5_STATE).transpose(1, 2, 0, 3, 4)
    pf = jnp.concatenate([jnp.flip(ab[:, 0:1], axis=2), ab[:, 1:2]], axis=1)

    c_re, c_im = p["s5_c_re"].astype(F32), p["s5_c_im"].astype(F32)
    cmat = jnp.concatenate([c_re, -c_im], axis=-1).reshape(n_layers, 2, D_BRANCH, 2 * S5_STATE)
    ct_re = c_re.reshape(n_layers, 2, D_BRANCH, S5_STATE).transpose(0, 1, 3, 2)
    ct_im = c_im.reshape(n_layers, 2, D_BRANCH, S5_STATE).transpose(0, 1, 3, 2)
    pt_r = jnp.repeat(pr[1:].transpose(0, 1, 2, 4, 3), S5_CH, axis=-1)
    pt_i = jnp.repeat(pi[1:].transpose(0, 1, 2, 4, 3), S5_CH, axis=-1)
    q = jnp.concatenate([ct_re * pt_r - ct_im * pt_i, -(ct_re * pt_i + ct_im * pt_r)], axis=3)
    q = q.transpose(1, 2, 0, 3, 4)
    qf = jnp.concatenate([q[:, 0:1], jnp.flip(q[:, 1:2], axis=2)], axis=1)

    a_chunk = jnp.stack([jnp.concatenate([pr[t_len], pr[t_len]], axis=-1),
                         jnp.concatenate([-pi[t_len], pi[t_len]], axis=-1)], axis=2)
    return pf, qf, cmat, a_chunk


def _s5_ucat(z_ref):
    return jnp.concatenate([z_ref[:, t, :].astype(BF16) for t in range(S5_CHUNK)], axis=1)


def _octet_group(shape, axis):
    return lax.broadcasted_iota(jnp.int32, shape, axis) // S5_CH


def _s5_in_kernel(z_ref, pf_ref, x_ref, p_scr):
    gpo = S5_GROUPS // S5_OCTETS

    @pl.when(pl.program_id(1) == 0)
    def _():
        row_group = _octet_group((LANES, LANES), 0)
        for d in range(2):
            for t in range(S5_CHUNK):
                blk = pf_ref[d, t]
                for g in range(gpo):
                    p_scr[d, t * LANES:(t + 1) * LANES, g * LANES:(g + 1) * LANES] = (
                        jnp.where(row_group == g, blk, 0.0).astype(BF16))

    u = _s5_ucat(z_ref)
    for d in range(2):
        x_ref[d] = _dot(u, p_scr[d]).reshape(u.shape[0], gpo, 2 * S5_STATE)


def _s5_in(z3, pf, l, tr):
    r16 = z3.shape[0]
    gpo = S5_GROUPS // S5_OCTETS
    w = 2 * S5_STATE
    return pl.pallas_call(
        _s5_in_kernel,
        grid=(S5_OCTETS, r16 // tr),
        in_specs=[pl.BlockSpec((tr, S5_CHUNK, LANES), lambda o, i: (i, 0, S5_COL0 // LANES + o)),
                  pl.BlockSpec((None, 2, S5_CHUNK, LANES, w), lambda o, i: (l, 0, 0, o, 0))],
        out_specs=pl.BlockSpec((2, tr, gpo, w), lambda o, i: (0, i, o, 0)),
        out_shape=jax.ShapeDtypeStruct((2, r16, S5_GROUPS, w), F32),
        scratch_shapes=[pltpu.VMEM((2, S5_OCT_K, gpo * w), BF16)],
        compiler_params=_cparams(2, _nbytes((tr, S5_CHUNK, LANES), F32), _nbytes((2, S5_CHUNK, LANES, w), F32),
                                 _nbytes((2, tr, gpo, w), F32),
                                 extra=_nbytes((2, S5_OCT_K, gpo * w), BF16) + 4 * _nbytes((tr, S5_OCT_K), F32)),
        name="s5_in",
    )(z3, pf)


def _s5_scan_kernel(*refs, n_in, nsb, jb):
    a_ref, h0_ref = refs[0], refs[1]
    xf = refs[2:2 + n_in]
    xb = refs[2 + n_in:2 + 2 * n_in]
    sf = refs[2 + 2 * n_in:2 + 3 * n_in]
    sb = refs[2 + 3 * n_in:2 + 4 * n_in]
    fin_ref, st_ref = refs[2 + 4 * n_in], refs[3 + 4 * n_in]
    t = pl.program_id(0)

    @pl.when(t == 0)
    def _():
        st_ref[...] = h0_ref[...]

    half = S5_STATE

    def body(s, carry):
        jf = s
        jr = jb - 1 - s
        for r in range(n_in):
            for q in range(nsb):
                idx = r * nsb + q
                cur = st_ref[0, idx]
                sf[r][q * jb + jf] = cur
                st_ref[0, idx] = cur * a_ref[0, 0] + pltpu.roll(cur, half, 1) * a_ref[0, 1] + xf[r][q * jb + jf]
                cur = st_ref[1, idx]
                sb[r][q * jb + jr] = cur
                st_ref[1, idx] = cur * a_ref[1, 0] + pltpu.roll(cur, half, 1) * a_ref[1, 1] + xb[r][q * jb + jr]
        return carry

    lax.fori_loop(0, jb, body, 0)

    @pl.when(t == pl.num_programs(0) - 1)
    def _():
        fin_ref[...] = st_ref[...]


def _s5_scan(x, a_chunk, h0, l, row0, n_in, nsb, nj, jb):
    g = x.shape[2]
    w = x.shape[3]
    nblk = nj // jb
    nseq = n_in * nsb
    rows = nsb * jb
    base = row0 // rows

    def xspec(d, r):
        if d == 0:
            return pl.BlockSpec((None, rows, g, w), lambda t: (0, base + r * nblk + t, 0, 0))
        return pl.BlockSpec((None, rows, g, w), lambda t: (1, base + r * nblk + nblk - 1 - t, 0, 0))

    def sspec(d):
        if d == 0:
            return pl.BlockSpec((rows, g, w), lambda t: (t, 0, 0))
        return pl.BlockSpec((rows, g, w), lambda t: (nblk - 1 - t, 0, 0))

    st_spec = pl.BlockSpec((2, nseq, g, w), lambda t: (0, 0, 0, 0))
    res = pl.pallas_call(
        functools.partial(_s5_scan_kernel, n_in=n_in, nsb=nsb, jb=jb),
        grid=(nblk,),
        in_specs=[pl.BlockSpec((None, 2, 2, g, w), lambda t: (l, 0, 0, 0, 0)), st_spec]
                 + [xspec(0, r) for r in range(n_in)] + [xspec(1, r) for r in range(n_in)],
        out_specs=[sspec(0)] * n_in + [sspec(1)] * n_in + [st_spec],
        out_shape=[jax.ShapeDtypeStruct((nsb * nj, g, w), F32)] * (2 * n_in)
                  + [jax.ShapeDtypeStruct((2, nseq, g, w), F32)],
        scratch_shapes=[pltpu.VMEM((2, nseq, g, w), F32)],
        compiler_params=_cparams(1, 4 * n_in * _nbytes((rows, g, w), F32), 3 * _nbytes((2, nseq, g, w), F32)),
        name="s5_scan",
    )(a_chunk, h0, *([x] * (2 * n_in)))
    return res[:n_in], res[n_in:2 * n_in], res[2 * n_in]


def _s5_out_kernel(z_ref, pf_ref, cm_ref, qf_ref, d_ref, *rest, starts):
    n = len(starts)
    sf_refs, sb_refs = rest[:n], rest[n:2 * n]
    y_ref, m_scr, q_scr = rest[2 * n:]
    gpo = S5_GROUPS // S5_OCTETS

    @pl.when(pl.program_id(1) == 0)
    def _():
        same_group = _octet_group((LANES, LANES), 0) == _octet_group((LANES, LANES), 1)

        def lag_op(d, k):
            ab = pf_ref[d, S5_CHUNK - 1 - k] if d == 0 else pf_ref[d, k]
            ah, al = _split(ab)
            ch, cl = _split(cm_ref[d])
            return jnp.where(same_group, _dot_nt(ah, ch) + (_dot_nt(ah, cl) + _dot_nt(al, ch)), 0.0)

        fwd = [lag_op(0, k) for k in range(S5_CHUNK)]
        bwd = [lag_op(1, k) for k in range(S5_CHUNK)]
        blocks = {0: (fwd[0] + bwd[0]).astype(BF16)}
        for k in range(1, S5_CHUNK):
            blocks[k] = fwd[k].astype(BF16)
            blocks[-k] = bwd[k].astype(BF16)
        for t in range(S5_CHUNK):
            for t2 in range(S5_CHUNK):
                m_scr[t * LANES:(t + 1) * LANES, t2 * LANES:(t2 + 1) * LANES] = blocks[t2 - t]
        lane_group = _octet_group((LANES, LANES), 1)
        for d in range(2):
            for t in range(S5_CHUNK):
                blk = qf_ref[d, t]
                for g in range(gpo):
                    q_scr[d, g * LANES:(g + 1) * LANES, t * LANES:(t + 1) * LANES] = (
                        jnp.where(lane_group == g, blk, 0.0).astype(BF16))

    u = _s5_ucat(z_ref)
    rows = u.shape[0]
    width = gpo * 2 * S5_STATE

    def entering_state(refs):
        s = refs[0][...]
        for ref, start in zip(refs[1:], starts[1:]):
            s = jnp.where(pl.program_id(1) >= start, ref[...], s)
        return s.reshape(rows, width).astype(BF16)

    y = _dot(u, m_scr[...])
    y += _dot(entering_state(sf_refs), q_scr[0])
    y += _dot(entering_state(sb_refs), q_scr[1])
    for t in range(S5_CHUNK):
        y_ref[:, t, :] = y[:, t * LANES:(t + 1) * LANES] + z_ref[:, t, :] * d_ref[...]


def _s5_out(z3, pf, cmat, qf, d_oct, sin_f, sin_b, l, tr):
    r16 = z3.shape[0]
    gpo = S5_GROUPS // S5_OCTETS
    w = 2 * S5_STATE
    tiles = [s.shape[0] // tr for s in sin_f]
    starts = tuple(sum(tiles[:n]) for n in range(len(tiles)))
    sspecs = [pl.BlockSpec((tr, gpo, w), functools.partial(
        lambda o, i, start, count: (jnp.clip(i - start, 0, count - 1), o, 0), start=start, count=count))
        for start, count in zip(starts, tiles)]
    return pl.pallas_call(
        functools.partial(_s5_out_kernel, starts=starts),
        grid=(S5_OCTETS, r16 // tr),
        in_specs=[pl.BlockSpec((tr, S5_CHUNK, LANES), lambda o, i: (i, 0, S5_COL0 // LANES + o)),
                  pl.BlockSpec((None, 2, S5_CHUNK, LANES, w), lambda o, i: (l, 0, 0, o, 0)),
                  pl.BlockSpec((None, 2, LANES, w), lambda o, i: (l, 0, o, 0)),
                  pl.BlockSpec((None, 2, S5_CHUNK, w, LANES), lambda o, i: (l, 0, 0, 0, o)),
                  pl.BlockSpec((None, None, 1, LANES), lambda o, i: (l, o, 0, 0)),
                  *sspecs, *sspecs],
        out_specs=pl.BlockSpec((tr, S5_CHUNK, LANES), lambda o, i: (i, 0, o)),
        out_shape=jax.ShapeDtypeStruct((r16, S5_CHUNK, D_BRANCH), F32),
        scratch_shapes=[pltpu.VMEM((S5_OCT_K, S5_OCT_K), BF16), pltpu.VMEM((2, gpo * w, S5_OCT_K), BF16)],
        compiler_params=_cparams(2, 2 * _nbytes((tr, S5_CHUNK, LANES), F32), 2 * _nbytes((2, S5_CHUNK, LANES, w), F32),
                                 2 * len(tiles) * _nbytes((tr, gpo, w), F32),
                                 extra=_nbytes((S5_OCT_K, S5_OCT_K), BF16) + _nbytes((2, gpo * w, S5_OCT_K), BF16)
                                 + 4 * _nbytes((tr, S5_OCT_K), F32)),
        name="s5_out",
    )(z3, pf, cmat, qf, d_oct, *sin_f, *sin_b)


def _s5_glu_kernel(y_ref, w_ref, b_ref, o_ref, wb_ref):
    @pl.when(pl.program_id(0) == 0)
    def _():
        wb_ref[...] = w_ref[...].astype(BF16)

    y = _gelu(y_ref[...])
    o_ref[...] = (y * _sigmoid(_dot(y.astype(BF16), wb_ref[...]) + b_ref[...])).astype(BF16)


def _s5_glu(cfg, y, w, b, l):
    tm = min(512, cfg.ll)
    return pl.pallas_call(
        _s5_glu_kernel,
        grid=(cfg.r // tm,),
        in_specs=[pl.BlockSpec((tm, D_BRANCH), lambda i: (i, 0)),
                  pl.BlockSpec((None, D_BRANCH, D_BRANCH), lambda i: (l, 0, 0)),
                  pl.BlockSpec((None, 1, D_BRANCH), lambda i: (l, 0, 0))],
        out_specs=pl.BlockSpec((tm, D_BRANCH), lambda i: (i, 0)),
        out_shape=jax.ShapeDtypeStruct((cfg.r, D_BRANCH), BF16),
        scratch_shapes=[pltpu.VMEM((D_BRANCH, D_BRANCH), BF16)],
        compiler_params=_cparams(1, 2 * _nbytes((tm, D_BRANCH), F32), _nbytes((D_BRANCH, D_BRANCH), F32),
                                 extra=_nbytes((D_BRANCH, D_BRANCH), BF16) + 4 * _nbytes((tm, D_BRANCH), F32)),
        name="s5_glu",
    )(y, w, b.reshape(DEPTH, 1, D_BRANCH))


def _s5(cfg, s5m, z, p, h0_lat, l):
    pf, qf, cmat, a_chunk = s5m
    r16 = cfg.r // S5_CHUNK
    tr = min(256, cfg.ll // S5_CHUNK)
    z3 = z.reshape(r16, S5_CHUNK, N_IN)
    x = _s5_in(z3, pf, l, tr)
    njc, njl = cfg.lc // S5_CHUNK, cfg.ll // S5_CHUNK
    rc16 = cfg.rc // S5_CHUNK
    zero_h0 = jnp.zeros((2, cfg.nc, S5_GROUPS, 2 * S5_STATE), F32)
    sf_c, sb_c, fin_ctx = _s5_scan(x, a_chunk, zero_h0, l, 0, 1, cfg.nc, njc, njc)
    sf_l, sb_l, _ = _s5_scan(x, a_chunk, h0_lat, l, rc16, cfg.nl, 1, njl, min(32, njl))
    sin_f = list(sf_c) + list(sf_l)
    sin_b = list(sb_c) + list(sb_l)
    d_oct = p["s5_d"].reshape(DEPTH, S5_OCTETS, 1, LANES)
    y = _s5_out(z3, pf, cmat, qf, d_oct, sin_f, sin_b, l, tr).reshape(cfg.r, D_BRANCH)
    return _s5_glu(cfg, y, p["s5_glu_w"], p["s5_glu_b"], l), fin_ctx


def _forward(cfg, x_prompt, x_sample, c, cache_k, cache_v, state_ssm_re, state_ssm_im, c_ctx, p):
    d = D_MODEL
    x = (x_prompt.reshape(cfg.rc, d), x_sample.reshape(cfg.rl, d))
    cvec = jnp.concatenate([c_ctx[None], c, jnp.zeros((MOD_ROWS - cfg.nseg, d), F32)], axis=0)
    mod = _mod(cvec, p["w_mod"], p["b_mod"])[:, :cfg.nseg].reshape(DEPTH, cfg.nseg, 6, 1, d)
    mod = [[mod[l, :, i] for i in range(6)] for l in range(DEPTH)]
    norm_g = p["norm_g"]

    consts = {
        "dft": {},
        "feats": {n: _hy_feats(n) for n in {cfg.lc, cfg.ll}},
        "rope": _rope_tables(cfg.ll),
    }
    for n in {cfg.lc, cfg.ll}:
        consts["dft"][n] = {name: t.astype(BF16) for name, t in _dft_tables(n).items()}
    s5m = _s5_operands(p)

    kv_shape = (cfg.nc, cfg.lc, N_KV_HEADS, HEAD_DIM)
    ks, vs, srs, sis = [], [], [], []
    h = _resid_norm(cfg, x, gpre=norm_g[0, 0], sc=mod[0][1], sh=mod[0][0])
    for l in range(DEPTH):
        z = _mm(h, p["w_in"], l, tm=cfg.tm, tn=512)
        ks.append(z[:cfg.rc, K_COL0:K_COL0 + KV_COLS].reshape(kv_shape))
        vs.append(z[:cfg.rc, V_COL0:V_COL0 + KV_COLS].reshape(kv_shape))

        y_hy = _hyena(cfg, consts, z, p, l)
        sink = jnp.broadcast_to(p["attn_sink"][l][:, None], (N_Q_HEADS, LANES))
        past = cache_k.shape[2]
        y_at = jnp.concatenate([
            _attn_ctx(cfg, z, sink),
            _attn_lat(cfg, consts, z, cache_k[:, l].reshape(cfg.nl, past, KV_COLS),
                      cache_v[:, l].reshape(cfg.nl, past, KV_COLS), sink)], axis=0)
        h0 = jnp.concatenate([state_ssm_re[:, l], state_ssm_im[:, l]], axis=-1).transpose(1, 0, 2, 3)
        y_s5, fin = _s5(cfg, s5m, z, p, h0, l)
        srs.append(fin[..., :S5_STATE].transpose(1, 0, 2, 3))
        sis.append(fin[..., S5_STATE:].transpose(1, 0, 2, 3))

        merged = _merge(cfg, z, (y_hy, y_at, y_s5),
                        (p["w_branch_hy"], p["w_branch_attn"], p["w_branch_s5"]), l)
        y = _mm(merged, p["w_out"], l, tm=cfg.tm, tn=512)
        x, h = _resid_norm(cfg, x, y, gate=mod[l][2], gpost=norm_g[l, 1],
                           gpre=norm_g[l, 2], sc=mod[l][4], sh=mod[l][3])
        u = _mm(h, p["ffn_w_up"], l, tm=cfg.tm, tn=512)
        act = _ffn_act(cfg, u, p["ffn_conv_w"], p["ffn_conv_b"], l)
        f = _mm(act, p["ffn_w_down"], l, tm=min(512, cfg.tm), tn=512, w_buffers=1)
        if l + 1 < DEPTH:
            x, h = _resid_norm(cfg, x, f, gate=mod[l][5], gpost=norm_g[l, 3],
                               gpre=norm_g[l + 1, 0], sc=mod[l + 1][1], sh=mod[l + 1][0])
        else:
            x = _resid_norm(cfg, x, f, gate=mod[l][5], gpost=norm_g[l, 3], split_out=True)

    return (x[0].reshape(cfg.nc, cfg.lc, d), x[1].reshape(cfg.nl, cfg.ll, d),
            jnp.stack(ks, axis=1), jnp.stack(vs, axis=1), jnp.stack(srs, axis=1), jnp.stack(sis, axis=1))


def kernel(x_prompt, x_sample, c, cache_k, cache_v, state_ssm_re, state_ssm_im, c_ctx, w_mod, b_mod, norm_g, w_in, hy_conv_w, hy_conv_b, hy_w1, hy_b1, hy_w2, hy_b2, hy_w3, hy_freq, hy_decay, hy_bias, attn_sink, s5_lam_re, s5_lam_im, s5_log_dt, s5_b_re, s5_b_im, s5_c_re, s5_c_im, s5_d, s5_glu_w, s5_glu_b, w_branch_hy, w_branch_attn, w_branch_s5, w_out, ffn_w_up, ffn_conv_w, ffn_conv_b, ffn_w_down):
    p = dict(w_mod=w_mod, b_mod=b_mod, norm_g=norm_g, w_in=w_in, hy_conv_w=hy_conv_w, hy_conv_b=hy_conv_b,
             hy_w1=hy_w1, hy_b1=hy_b1, hy_w2=hy_w2, hy_b2=hy_b2, hy_w3=hy_w3, hy_freq=hy_freq,
             hy_decay=hy_decay, hy_bias=hy_bias, attn_sink=attn_sink, s5_lam_re=s5_lam_re,
             s5_lam_im=s5_lam_im, s5_log_dt=s5_log_dt, s5_b_re=s5_b_re, s5_b_im=s5_b_im, s5_c_re=s5_c_re,
             s5_c_im=s5_c_im, s5_d=s5_d, s5_glu_w=s5_glu_w, s5_glu_b=s5_glu_b, w_branch_hy=w_branch_hy,
             w_branch_attn=w_branch_attn, w_branch_s5=w_branch_s5, w_out=w_out, ffn_w_up=ffn_w_up,
             ffn_conv_w=ffn_conv_w, ffn_conv_b=ffn_conv_b, ffn_w_down=ffn_w_down)
    cfg = _Cfg(x_prompt.shape[0], x_prompt.shape[1], x_sample.shape[0], x_sample.shape[1])
    return _forward(cfg, x_prompt, x_sample, c, cache_k, cache_v, state_ssm_re, state_ssm_im, c_ctx, p)
```

```python
import functools
import math

import numpy as np
import jax
import jax.numpy as jnp
from jax import lax
from jax.experimental import pallas as pl
from jax.experimental.pallas import tpu as pltpu

F32 = jnp.float32
BF16 = jnp.bfloat16

D_MODEL = 4096
DEPTH = 2
GRID_W = 64
D_BRANCH = D_MODEL // 4
HY_WIDTH = D_BRANCH
HY_BANDS = 16
HY_FEAT = 1 + 2 * HY_BANDS
HY_HID = 64
HEAD_DIM = 128
N_Q_HEADS = D_BRANCH // HEAD_DIM
N_KV_HEADS = 2
GQA_GROUP = N_Q_HEADS // N_KV_HEADS
WINDOW = 128
BLOCK = 128
ROPE_BASE = 10000.0
S5_CH = 16
S5_GROUPS = D_BRANCH // S5_CH
S5_STATE = 64
S5_CHUNK = 16
S5_OCTETS = 8
S5_OCT_K = S5_CHUNK * 128
D_FF = 2 * D_MODEL
EPS = 1e-6
NEG_INF = -1e30

HY_COLS = 3 * HY_WIDTH
Q_COL0 = HY_COLS
K_COL0 = Q_COL0 + N_Q_HEADS * HEAD_DIM
KV_COLS = N_KV_HEADS * HEAD_DIM
V_COL0 = K_COL0 + KV_COLS
S5_COL0 = V_COL0 + KV_COLS
GATE_COL0 = S5_COL0 + D_BRANCH
N_IN = GATE_COL0 + 3 * D_MODEL

V7X_VMEM_REQUEST_MAX = 60 * 1024 * 1024
LANES = 128
SUBLANES = 8
MOD_ROWS = 8


def _nbytes(shape, dtype):
    return math.prod(shape) * jnp.dtype(dtype).itemsize


def _cparams(n_grid, *block_bytes, extra=0):
    est = 2 * sum(block_bytes) + extra + (4 << 20)
    return pltpu.CompilerParams(
        dimension_semantics=("arbitrary",) * n_grid,
        vmem_limit_bytes=int(min(max(est, 16 << 20), V7X_VMEM_REQUEST_MAX)))


def _dot(a, b):
    return jnp.dot(a, b, preferred_element_type=F32)


def _dot_nt(a, b):
    return lax.dot_general(a, b, (((1,), (1,)), ((), ())), preferred_element_type=F32)


def _split(a):
    hi = a.astype(BF16)
    return hi, (a - hi.astype(F32)).astype(BF16)


def _dot3(a, b):
    ah, al = _split(a)
    bh, bl = _split(b)
    return _dot(ah, bh) + (_dot(ah, bl) + _dot(al, bh))


def _sigmoid(x):
    return 1.0 / (1.0 + jnp.exp(-x))


def _gelu(x):
    c = math.sqrt(2.0 / math.pi)
    return x * (0.5 + 0.5 * jnp.tanh(x * (c + (c * 0.044715) * (x * x))))


def _rms(x, g):
    return x * lax.rsqrt(jnp.mean(x * x, axis=-1, keepdims=True) + EPS) * g


class _Cfg:
    def __init__(self, nc, lc, nl, ll):
        self.nc, self.lc, self.nl, self.ll = nc, lc, nl, ll
        self.rc = nc * lc
        self.rl = nl * ll
        self.r = self.rc + self.rl
        self.nseg = 1 + nl
        assert self.rc % ll == 0 and ll % lc == 0 and lc % BLOCK == 0
        self.tm = min(1024, ll)
        self.rb = ll

    def seg_of_row(self, row0):
        return jnp.where(row0 >= self.rc, (row0 - self.rc) // self.ll + 1, 0)


def _mod_kernel(c_ref, w_ref, b_ref, o_ref):
    c = c_ref[...]
    s = (c * _sigmoid(c)).astype(BF16)
    o_ref[...] = _dot(s, w_ref[...].astype(BF16)) + b_ref[...]


def _mod(cvec, w_mod, b_mod):
    n = w_mod.shape[-1]
    tn = 512
    return pl.pallas_call(
        _mod_kernel,
        grid=(DEPTH, n // tn),
        in_specs=[pl.BlockSpec((MOD_ROWS, D_MODEL), lambda l, j: (0, 0)),
                  pl.BlockSpec((None, D_MODEL, tn), lambda l, j: (l, 0, j)),
                  pl.BlockSpec((None, 1, tn), lambda l, j: (l, 0, j))],
        out_specs=pl.BlockSpec((None, MOD_ROWS, tn), lambda l, j: (l, 0, j)),
        out_shape=jax.ShapeDtypeStruct((DEPTH, MOD_ROWS, n), F32),
        compiler_params=_cparams(2, _nbytes((D_MODEL, tn), F32), extra=_nbytes((D_MODEL, tn), BF16)),
        name="mod",
    )(cvec, w_mod, b_mod.reshape(DEPTH, 1, n))


def _resid_norm_kernel(*refs, has_y, has_h, split_in, split_out, n_ctx_tiles):
    it = iter(refs)
    in_ctx = pl.program_id(0) < n_ctx_tiles
    if split_in:
        xa_ref, xb_ref = next(it), next(it)
        x = jnp.where(in_ctx, xa_ref[...], xb_ref[...])
    else:
        x = next(it)[...]
    if has_y:
        y_ref, gate_ref, gpost_ref = next(it), next(it), next(it)
    if has_h:
        gpre_ref, sc_ref, sh_ref = next(it), next(it), next(it)
    if has_y:
        x = x + gate_ref[...] * _rms(y_ref[...], gpost_ref[...])
        if split_out:
            xa_out, xb_out = next(it), next(it)

            @pl.when(in_ctx)
            def _():
                xa_out[...] = x

            @pl.when(jnp.logical_not(in_ctx))
            def _():
                xb_out[...] = x
        else:
            xo_ref = next(it)
            xo_ref[...] = x
    if has_h:
        ho_ref = next(it)
        ho_ref[...] = (_rms(x, gpre_ref[...]) * (1.0 + sc_ref[...]) + sh_ref[...]).astype(BF16)


def _resid_norm(cfg, x, y=None, gate=None, gpost=None, gpre=None, sc=None, sh=None, split_out=False):
    has_y, has_h, split_in = y is not None, gpre is not None, isinstance(x, tuple)
    tm = 256
    nca = cfg.rc // tm
    row = pl.BlockSpec((tm, D_MODEL), lambda i: (i, 0))
    row_a = pl.BlockSpec((tm, D_MODEL), lambda i: (jnp.minimum(i, nca - 1), 0))
    row_b = pl.BlockSpec((tm, D_MODEL), lambda i: (jnp.maximum(i - nca, 0), 0))
    vec = pl.BlockSpec((1, D_MODEL), lambda i: (0, 0))
    seg = pl.BlockSpec((None, 1, D_MODEL), lambda i: (cfg.seg_of_row(i * tm), 0, 0))
    args, specs = (list(x), [row_a, row_b]) if split_in else ([x], [row])
    outs, ospecs = [], []
    if has_y:
        args += [y, gate, gpost.reshape(1, D_MODEL)]
        specs += [row, seg, vec]
        if split_out:
            outs += [jax.ShapeDtypeStruct((cfg.rc, D_MODEL), F32), jax.ShapeDtypeStruct((cfg.rl, D_MODEL), F32)]
            ospecs += [row_a, row_b]
        else:
            outs.append(jax.ShapeDtypeStruct((cfg.r, D_MODEL), F32))
            ospecs.append(row)
    if has_h:
        args += [gpre.reshape(1, D_MODEL), sc, sh]
        specs += [vec, seg, seg]
        outs.append(jax.ShapeDtypeStruct((cfg.r, D_MODEL), BF16))
        ospecs.append(row)
    res = pl.pallas_call(
        functools.partial(_resid_norm_kernel, has_y=has_y, has_h=has_h, split_in=split_in, split_out=split_out,
                          n_ctx_tiles=nca),
        grid=(cfg.r // tm,),
        in_specs=specs, out_specs=ospecs, out_shape=outs,
        compiler_params=_cparams(1, 5 * _nbytes((tm, D_MODEL), F32)),
        name="resid_norm",
    )(*args)
    return res if len(res) > 1 else res[0]


def _mm_kernel(x_ref, w_ref, o_ref, wb_ref):
    @pl.when(pl.program_id(1) == 0)
    def _():
        wb_ref[...] = w_ref[...].astype(BF16)

    o_ref[...] = _dot(x_ref[...], wb_ref[...]).astype(o_ref.dtype)


def _mm(x, w, l, *, tm, tn, out_dtype=F32, w_buffers=2):
    m, k = x.shape
    n = w.shape[-1]
    w_mode = {} if w_buffers == 2 else {"pipeline_mode": pl.Buffered(w_buffers)}
    return pl.pallas_call(
        _mm_kernel,
        grid=(n // tn, m // tm),
        in_specs=[pl.BlockSpec((tm, k), lambda j, i: (i, 0)),
                  pl.BlockSpec((None, k, tn), lambda j, i: (l, 0, j), **w_mode)],
        out_specs=pl.BlockSpec((tm, tn), lambda j, i: (i, j)),
        out_shape=jax.ShapeDtypeStruct((m, n), out_dtype),
        scratch_shapes=[pltpu.VMEM((k, tn), BF16)],
        compiler_params=_cparams(2, _nbytes((tm, k), BF16), _nbytes((tm, tn), out_dtype),
                                 extra=w_buffers * _nbytes((k, tn), F32) + _nbytes((k, tn), BF16)
                                 + _nbytes((tm, tn), F32)),
        name="mm",
    )(x, w)


def _merge_kernel(g0, g1, g2, y0, y1, y2, w0, w1, w2, o_ref, wb_ref):
    @pl.when(pl.program_id(1) == 0)
    def _():
        wb_ref[0] = w0[...].astype(BF16)
        wb_ref[1] = w1[...].astype(BF16)
        wb_ref[2] = w2[...].astype(BF16)

    acc = _sigmoid(g0[...]) * _dot(y0[...], wb_ref[0])
    acc += _sigmoid(g1[...]) * _dot(y1[...], wb_ref[1])
    acc += _sigmoid(g2[...]) * _dot(y2[...], wb_ref[2])
    o_ref[...] = acc.astype(o_ref.dtype)


def _merge(cfg, z, ys, ws, l):
    tm, tn = cfg.tm, 512
    gate_specs = [pl.BlockSpec((tm, tn), functools.partial(
        lambda j, i, b: (i, (GATE_COL0 + b * D_MODEL) // tn + j), b=b)) for b in range(3)]
    y_spec = pl.BlockSpec((tm, D_BRANCH), lambda j, i: (i, 0))
    w_spec = pl.BlockSpec((None, D_BRANCH, tn), lambda j, i: (l, 0, j))
    return pl.pallas_call(
        _merge_kernel,
        grid=(D_MODEL // tn, cfg.r // tm),
        in_specs=gate_specs + [y_spec] * 3 + [w_spec] * 3,
        out_specs=pl.BlockSpec((tm, tn), lambda j, i: (i, j)),
        out_shape=jax.ShapeDtypeStruct((cfg.r, D_MODEL), BF16),
        scratch_shapes=[pltpu.VMEM((3, D_BRANCH, tn), BF16)],
        compiler_params=_cparams(2, 3 * _nbytes((tm, tn), F32), 3 * _nbytes((tm, D_BRANCH), BF16),
                                 3 * _nbytes((D_BRANCH, tn), F32), _nbytes((tm, tn), BF16),
                                 extra=3 * _nbytes((D_BRANCH, tn), BF16) + 2 * _nbytes((tm, tn), F32)),
        name="merge",
    )(z, z, z, *ys, *ws)


def _block_seq_len(cfg):
    return jnp.where(pl.program_id(0) < cfg.rc // cfg.rb, cfg.lc, cfg.ll)


def _dwconv3_rows(x_ref, w_ref, b_ref, r0, sl):
    total = x_ref.shape[0]
    cur = x_ref[r0:r0 + BLOCK, :]
    sub = lax.broadcasted_iota(jnp.int32, (SUBLANES, cur.shape[1]), 0)
    xm = x_ref[r0 - 1:r0 - 1 + BLOCK, :] if r0 > 0 else pltpu.roll(cur, 1, 0)
    first = jnp.where(((r0 + sub) & (sl - 1)) == 0, 0.0, xm[:SUBLANES])
    xm = jnp.concatenate([first, xm[SUBLANES:]], axis=0)
    xp = x_ref[r0 + 1:r0 + 1 + BLOCK, :] if r0 + BLOCK < total else pltpu.roll(cur, BLOCK - 1, 0)
    last = jnp.where(((r0 + BLOCK - SUBLANES + sub) & (sl - 1)) == sl - 1, 0.0, xp[BLOCK - SUBLANES:])
    xp = jnp.concatenate([xp[:BLOCK - SUBLANES], last], axis=0)
    return xm * w_ref[0:1, :] + cur * w_ref[1:2, :] + xp * w_ref[2:3, :] + b_ref[...]


def _hy_dwconv_kernel(x_ref, w_ref, b_ref, oe_ref, oo_ref, *, cfg):
    sl = _block_seq_len(cfg)
    half = x_ref.shape[0] // 2
    xe = x_ref[pl.ds(0, half, stride=2), :]
    xo = x_ref[pl.ds(1, half, stride=2), :]
    m = lax.broadcasted_iota(jnp.int32, xe.shape, 0)
    before_even = jnp.where(((2 * m) & (sl - 1)) == 0, 0.0, pltpu.roll(xo, 1, 0))
    after_odd = jnp.where(((2 * m + 1) & (sl - 1)) == sl - 1, 0.0, pltpu.roll(xe, half - 1, 0))
    w0, w1, w2 = w_ref[0:1, :], w_ref[1:2, :], w_ref[2:3, :]
    oe_ref[...] = before_even * w0 + xe * w1 + xo * w2 + b_ref[...]
    oo_ref[...] = xe * w0 + xo * w1 + after_odd * w2 + b_ref[...]


def _hy_dwconv(cfg, z, w, b, l):
    tn = LANES
    blk = _nbytes((cfg.rb, tn), F32)
    half = pl.BlockSpec((cfg.rb // 2, tn), lambda r, j: (r, j))
    return pl.pallas_call(
        functools.partial(_hy_dwconv_kernel, cfg=cfg),
        grid=(cfg.r // cfg.rb, HY_COLS // tn),
        in_specs=[pl.BlockSpec((cfg.rb, tn), lambda r, j: (r, j)),
                  pl.BlockSpec((None, 3, tn), lambda r, j: (l, 0, j)),
                  pl.BlockSpec((None, 1, tn), lambda r, j: (l, 0, j))],
        out_specs=[half, half],
        out_shape=[jax.ShapeDtypeStruct((cfg.r // 2, HY_COLS), F32)] * 2,
        compiler_params=_cparams(2, 2 * blk, extra=4 * blk),
        name="hy_dwconv",
    )(z, w, b.reshape(DEPTH, 1, HY_COLS))


def _ffn_act_kernel(a_ref, b_ref, wa_ref, wb_ref, ba_ref, bb_ref, o_ref, *, cfg):
    sl = _block_seq_len(cfg)
    for r0 in range(0, a_ref.shape[0], BLOCK):
        a = _dwconv3_rows(a_ref, wa_ref, ba_ref, r0, sl)
        b = _dwconv3_rows(b_ref, wb_ref, bb_ref, r0, sl)
        o_ref[r0:r0 + BLOCK, :] = (_gelu(a) * b).astype(BF16)


def _ffn_act(cfg, u, w, b, l):
    tn = 256
    nb = D_FF // tn
    blk = _nbytes((cfg.rb, tn), F32)
    b3 = b.reshape(DEPTH, 1, 2 * D_FF)
    return pl.pallas_call(
        functools.partial(_ffn_act_kernel, cfg=cfg),
        grid=(cfg.r // cfg.rb, nb),
        in_specs=[pl.BlockSpec((cfg.rb, tn), lambda r, j: (r, j)),
                  pl.BlockSpec((cfg.rb, tn), lambda r, j: (r, j + nb)),
                  pl.BlockSpec((None, 3, tn), lambda r, j: (l, 0, j)),
                  pl.BlockSpec((None, 3, tn), lambda r, j: (l, 0, j + nb)),
                  pl.BlockSpec((None, 1, tn), lambda r, j: (l, 0, j)),
                  pl.BlockSpec((None, 1, tn), lambda r, j: (l, 0, j + nb))],
        out_specs=pl.BlockSpec((cfg.rb, tn), lambda r, j: (r, j)),
        out_shape=jax.ShapeDtypeStruct((cfg.r, D_FF), BF16),
        compiler_params=_cparams(2, 3 * blk, extra=6 * blk),
        name="ffn_act",
    )(u, u, w, w, b3, b3)


def _dft_tables_kernel(ac_ref, as_ref, bc_ref, bs_ref, c_ref, s_ref):
    nb = bc_ref.shape[1]
    for a in range(ac_ref.shape[1]):
        ca, sa = ac_ref[:, a:a + 1], as_ref[:, a:a + 1]
        c_ref[:, a * nb:(a + 1) * nb] = (ca * bc_ref[...] - sa * bs_ref[...]).astype(BF16)
        s_ref[:, a * nb:(a + 1) * nb] = (sa * bc_ref[...] + ca * bs_ref[...]).astype(BF16)


def _dft_tables(length):
    half = length // 2
    period = 4 * length
    r = min(LANES, half)
    tk = min(256, half)
    idx = np.arange(half, dtype=np.int64)
    hi = np.arange(half // r, dtype=np.int64) * r
    lo = np.arange(r, dtype=np.int64)

    def cs(m):
        ang = 2.0 * np.pi * (m % period).astype(np.float64) / period
        return jnp.asarray(np.cos(ang), F32), jnp.asarray(np.sin(ang), F32)

    def expand(phase_hi, phase_lo):
        spec_a = pl.BlockSpec((tk, half // r), lambda i: (i, 0))
        spec_b = pl.BlockSpec((tk, r), lambda i: (i, 0))
        spec_o = pl.BlockSpec((tk, half), lambda i: (i, 0))
        return pl.pallas_call(
            _dft_tables_kernel,
            grid=(half // tk,),
            in_specs=[spec_a, spec_a, spec_b, spec_b],
            out_specs=[spec_o, spec_o],
            out_shape=[jax.ShapeDtypeStruct((half, half), BF16)] * 2,
            compiler_params=_cparams(1, 4 * _nbytes((tk, LANES), F32), 2 * _nbytes((tk, half), BF16)),
            name="dft_tables",
        )(*cs(phase_hi), *cs(phase_lo))

    odd = (2 * idx + 1)[:, None]
    row = idx[:, None]
    out = {}
    out["ce"], out["se"] = expand(odd * (2 * hi)[None, :], odd * (2 * lo)[None, :])
    out["co"], out["so"] = expand(odd * (2 * hi)[None, :], odd * (2 * lo + 1)[None, :])
    out["cet"], out["set"] = expand(2 * row * (2 * hi)[None, :], 2 * row * (2 * lo + 1)[None, :])
    out["cot"], out["sot"] = expand((2 * row + 1) * (2 * hi)[None, :], (2 * row + 1) * (2 * lo + 1)[None, :])
    return out


def _hy_feats(length):
    pos = jnp.arange(length, dtype=F32)
    t = (pos / length)[:, None]
    bands = jnp.linspace(1e-4, HY_BANDS - 1, HY_BANDS, dtype=F32)
    wpos = 2.0 * math.pi * t * bands
    feats = jnp.concatenate([t, jnp.cos(wpos), -jnp.sin(wpos)], axis=-1)
    feats = jnp.concatenate([feats[0::2], feats[1::2]], axis=0)
    return jnp.pad(feats, ((0, 0), (0, LANES - HY_FEAT)))


def _hy_filter_kernel(f_ref, w1_ref, b1_ref, w2_ref, b2_ref, w3_ref, fr_ref, dec_ref, hs_ref, hd_ref, hb0_ref):
    feats = f_ref[...]
    fr = fr_ref[...]
    h = jnp.sin(fr * (_dot3(feats, w1_ref[...]) + b1_ref[...]))
    h = jnp.sin(fr * (_dot3(h, w2_ref[...]) + b2_ref[...]))
    h = _dot3(h, w3_ref[...]) * jnp.exp(-feats[:, 0:1] * jnp.abs(dec_ref[...]))
    half = 2 * HY_WIDTH
    hf, hb = h[:, :half], h[:, half:]
    hs_ref[...] = (hf + hb).astype(BF16)
    hd_ref[...] = (hb - hf).astype(BF16)

    @pl.when(pl.program_id(0) == 0)
    def _():
        hb0_ref[...] = jnp.broadcast_to(hb[0:1, :], hb0_ref.shape)


def _hy_filter(length, feats, w1p, b1, w2, b2, w3, freq, decay, l):
    tl = 256
    half = 2 * HY_WIDTH
    lsel = lambda i: (l, 0, 0)
    return pl.pallas_call(
        _hy_filter_kernel,
        grid=(length // tl,),
        in_specs=[pl.BlockSpec((tl, LANES), lambda i: (i, 0)),
                  pl.BlockSpec((None, LANES, HY_HID), lsel),
                  pl.BlockSpec((None, 1, HY_HID), lsel),
                  pl.BlockSpec((None, HY_HID, HY_HID), lsel),
                  pl.BlockSpec((None, 1, HY_HID), lsel),
                  pl.BlockSpec((None, HY_HID, 2 * half), lsel),
                  pl.BlockSpec((None, 1, HY_HID), lsel),
                  pl.BlockSpec((None, 1, 2 * half), lsel)],
        out_specs=[pl.BlockSpec((tl, half), lambda i: (i, 0)),
                   pl.BlockSpec((tl, half), lambda i: (i, 0)),
                   pl.BlockSpec((8, half), lambda i: (0, 0))],
        out_shape=[jax.ShapeDtypeStruct((length, half), BF16),
                   jax.ShapeDtypeStruct((length, half), BF16),
                   jax.ShapeDtypeStruct((8, half), F32)],
        compiler_params=_cparams(1, 2 * _nbytes((tl, half), F32), extra=6 * _nbytes((tl, 2 * half), F32)),
        name="hy_filter",
    )(feats, w1p, b1, w2, b2, w3, freq, decay)


def _hy_spec_kernel(ce, se, co, so, hse, hso, hde, hdo, hb0_ref, kar, kai, kbr, kbi):
    a = _dot(ce[...], hse[...])
    b = _dot(co[...], hso[...])
    c = _dot(se[...], hde[...])
    d = _dot(so[...], hdo[...])
    kar[...] = a + b - hb0_ref[0:1, :]
    kai[...] = c + d
    kbr[...] = a - b - hb0_ref[0:1, :]
    kbi[...] = d - c


def _hy_spec(length, mats, hs, hd, hb0):
    half_len = length // 2
    tk = min(512, half_len)
    tn = 512
    width = 2 * HY_WIDTH
    ncb = width // tn
    fspec = pl.BlockSpec((tk, half_len), lambda j, i: (i, 0))
    espec = pl.BlockSpec((half_len, tn), lambda j, i: (0, j))
    ospec = pl.BlockSpec((half_len, tn), lambda j, i: (1, j))
    kspec = pl.BlockSpec((tk, tn), lambda j, i: (i, j))
    return pl.pallas_call(
        _hy_spec_kernel,
        grid=(ncb, half_len // tk),
        in_specs=[fspec] * 4 + [espec, ospec, espec, ospec, pl.BlockSpec((8, tn), lambda j, i: (0, j))],
        out_specs=[kspec] * 4,
        out_shape=[jax.ShapeDtypeStruct((half_len, width), F32)] * 4,
        compiler_params=_cparams(2, 4 * _nbytes((tk, half_len), BF16), 4 * _nbytes((half_len, tn), BF16),
                                 4 * _nbytes((tk, tn), F32), extra=4 * _nbytes((tk, tn), F32)),
        name="hy_spec",
    )(mats["ce"], mats["se"], mats["co"], mats["so"], hs, hs, hd, hd, hb0)


def _hy_fwd_kernel(ce, se, co, so, ue_ref, uo_ref, kar, kai, kbr, kbi, pr_ref, pi_ref, qr_ref, qi_ref, ub_ref):
    @pl.when(pl.program_id(2) == 0)
    def _():
        ub_ref[0] = ue_ref[...].astype(BF16)
        ub_ref[1] = uo_ref[...].astype(BF16)

    e_re, e_im = _dot(ce[...], ub_ref[0]), -_dot(se[...], ub_ref[0])
    o_re, o_im = _dot(co[...], ub_ref[1]), -_dot(so[...], ub_ref[1])
    ua_re, ua_im = e_re + o_re, e_im + o_im
    ub_re, ub_im = e_re - o_re, o_im - e_im
    ya_re = ua_re * kar[...] - ua_im * kai[...]
    ya_im = ua_re * kai[...] + ua_im * kar[...]
    yb_re = ub_re * kbr[...] - ub_im * kbi[...]
    yb_im = ub_re * kbi[...] + ub_im * kbr[...]
    pr_ref[...] = (ya_re + yb_re).astype(BF16)
    pi_ref[...] = (ya_im - yb_im).astype(BF16)
    qr_ref[...] = (ya_re - yb_re).astype(BF16)
    qi_ref[...] = (ya_im + yb_im).astype(BF16)


def _hy_inv_kernel(cet, set_, cot, sot, pr, pi_, qr, qi, ue_ref, uo_ref, ge_ref, go_ref, bias_ref, ye_ref, yo_ref,
                   *, inv_len):
    even = _dot(cet[...], pr[...]) - _dot(set_[...], pi_[...])
    odd = _dot(cot[...], qr[...]) - _dot(sot[...], qi[...])
    ye_ref[...] = (ge_ref[...] * (even * inv_len + ue_ref[...] * bias_ref[...])).astype(ye_ref.dtype)
    yo_ref[...] = (go_ref[...] * (odd * inv_len + uo_ref[...] * bias_ref[...])).astype(yo_ref.dtype)


def _hy_conv(length, nseq, mats, spec, order, data, gate, bias, l, out_dtype):
    half_len = length // 2
    tn = min(512, HY_WIDTH)
    tk = min(512, half_len)
    ncb = HY_WIDTH // tn
    ntk = half_len // tk
    fspec = pl.BlockSpec((tk, half_len), lambda s, j, i: (i, 0))
    kspec = pl.BlockSpec((tk, tn), lambda s, j, i: (i, order * ncb + j))
    tile = pl.BlockSpec((tk, tn), lambda s, j, i: (s * ntk + i, j))

    def full(loc):
        _, rb0, cb0 = loc
        return pl.BlockSpec((half_len, tn), lambda s, j, i: (rb0 + s, cb0 + j))

    def rows(loc):
        _, rb0, cb0 = loc
        return pl.BlockSpec((tk, tn), lambda s, j, i: ((rb0 + s) * ntk + i, cb0 + j))

    pq = pl.pallas_call(
        _hy_fwd_kernel,
        grid=(nseq, ncb, ntk),
        in_specs=[fspec] * 4 + [full(data[0]), full(data[1])] + [kspec] * 4,
        out_specs=[tile] * 4,
        out_shape=[jax.ShapeDtypeStruct((nseq * half_len, HY_WIDTH), BF16)] * 4,
        scratch_shapes=[pltpu.VMEM((2, half_len, tn), BF16)],
        compiler_params=_cparams(3, 4 * _nbytes((tk, half_len), BF16), 2 * _nbytes((half_len, tn), F32),
                                 4 * _nbytes((tk, tn), F32), 4 * _nbytes((tk, tn), BF16),
                                 extra=2 * _nbytes((half_len, tn), BF16) + 12 * _nbytes((tk, tn), F32)),
        name="hy_fwd",
    )(mats["ce"], mats["se"], mats["co"], mats["so"], data[0][0], data[1][0], *spec)

    resident = pl.BlockSpec((half_len, tn), lambda s, j, i: (s, j))
    return pl.pallas_call(
        functools.partial(_hy_inv_kernel, inv_len=1.0 / length),
        grid=(nseq, ncb, ntk),
        in_specs=[fspec] * 4 + [resident] * 4 + [rows(data[0]), rows(data[1]), rows(gate[0]), rows(gate[1]),
                                                 pl.BlockSpec((None, None, 1, tn), lambda s, j, i: (l, order, 0, j))],
        out_specs=[tile, tile],
        out_shape=[jax.ShapeDtypeStruct((nseq * half_len, HY_WIDTH), out_dtype)] * 2,
        compiler_params=_cparams(3, 4 * _nbytes((tk, half_len), BF16), 4 * _nbytes((half_len, tn), BF16),
                                 6 * _nbytes((tk, tn), F32), extra=6 * _nbytes((tk, tn), F32)),
        name="hy_inv",
    )(mats["cet"], mats["set"], mats["cot"], mats["sot"], *pq, data[0][0], data[1][0], gate[0][0], gate[1][0], bias)


def _hyena(cfg, consts, z, p, l):
    zce, zco = _hy_dwconv(cfg, z, p["hy_conv_w"], p["hy_conv_b"], l)
    bias = p["hy_bias"].reshape(DEPTH, 2, 1, HY_WIDTH)
    w1p = jnp.pad(p["hy_w1"], ((0, 0), (0, LANES - HY_FEAT), (0, 0)))
    tn = min(512, HY_WIDTH)
    outs = []
    for length, nseq, row0 in ((cfg.lc, cfg.nc, 0), (cfg.ll, cfg.nl, cfg.rc)):
        mats = consts["dft"][length]
        hs, hd, hb0 = _hy_filter(length, consts["feats"][length], w1p, p["hy_b1"].reshape(DEPTH, 1, HY_HID),
                                 p["hy_w2"], p["hy_b2"].reshape(DEPTH, 1, HY_HID), p["hy_w3"],
                                 p["hy_freq"].reshape(DEPTH, 1, HY_HID),
                                 p["hy_decay"].reshape(DEPTH, 1, 4 * HY_WIDTH), l)
        spec = _hy_spec(length, mats, hs, hd, hb0)
        rb0 = row0 // length

        def zc_cols(col0):
            return (zce, rb0, col0 // tn), (zco, rb0, col0 // tn)

        y1 = _hy_conv(length, nseq, mats, spec, 0, zc_cols(0), zc_cols(HY_WIDTH), bias, l, F32)
        y2 = _hy_conv(length, nseq, mats, spec, 1, ((y1[0], 0, 0), (y1[1], 0, 0)), zc_cols(2 * HY_WIDTH), bias, l,
                      BF16)
        outs.append(jnp.stack(y2, axis=1).reshape(nseq * length, HY_WIDTH))
    return jnp.concatenate(outs, axis=0)


def _softmax_sink_pv(parts, sink):
    m = sink
    for s, _ in parts:
        m = jnp.maximum(m, jnp.max(s, axis=-1, keepdims=True))
    den = jnp.exp(sink - m)
    acc = None
    for s, v in parts:
        e = jnp.exp(s - m)
        den = den + jnp.sum(e, axis=-1, keepdims=True)
        pv = _dot(e.astype(BF16), v)
        acc = pv if acc is None else acc + pv
    return acc / den


def _group_sink(sink_ref, h):
    return jnp.concatenate([jnp.broadcast_to(sink_ref[hq:hq + 1, 0:1], (BLOCK, 1))
                            for hq in range(h * GQA_GROUP, (h + 1) * GQA_GROUP)], axis=0)


def _store_group(o_ref, h, o):
    for g in range(GQA_GROUP):
        hq = h * GQA_GROUP + g
        o_ref[:, hq * HEAD_DIM:(hq + 1) * HEAD_DIM] = o[g * BLOCK:(g + 1) * BLOCK].astype(BF16)


def _attn_ctx_kernel(q_ref, k_ref, v_ref, sink_ref, o_ref):
    scale = HEAD_DIM ** -0.5
    for h in range(N_KV_HEADS):
        hs = slice(h * HEAD_DIM, (h + 1) * HEAD_DIM)
        k = k_ref[:, hs].astype(BF16)
        v = v_ref[:, hs].astype(BF16)
        q = jnp.concatenate([q_ref[:, hq * HEAD_DIM:(hq + 1) * HEAD_DIM]
                             for hq in range(h * GQA_GROUP, (h + 1) * GQA_GROUP)], axis=0).astype(BF16)
        s = _dot_nt(q, k) * scale
        _store_group(o_ref, h, _softmax_sink_pv([(s, v)], _group_sink(sink_ref, h)))


def _attn_ctx(cfg, z, sink):
    nb = cfg.lc // BLOCK
    qw = N_Q_HEADS * HEAD_DIM
    return pl.pallas_call(
        _attn_ctx_kernel,
        grid=(cfg.nc, nb),
        in_specs=[pl.BlockSpec((BLOCK, qw), lambda s, i: (s * nb + i, Q_COL0 // qw)),
                  pl.BlockSpec((cfg.lc, KV_COLS), lambda s, i: (s, K_COL0 // KV_COLS)),
                  pl.BlockSpec((cfg.lc, KV_COLS), lambda s, i: (s, V_COL0 // KV_COLS)),
                  pl.BlockSpec((N_Q_HEADS, LANES), lambda s, i: (0, 0))],
        out_specs=pl.BlockSpec((BLOCK, qw), lambda s, i: (s * nb + i, 0)),
        out_shape=jax.ShapeDtypeStruct((cfg.rc, qw), BF16),
        compiler_params=_cparams(2, _nbytes((BLOCK, qw), F32), 2 * _nbytes((cfg.lc, KV_COLS), F32)),
        name="attn_ctx",
    )(z, z, z, sink)


def _rope(x, c, s):
    return x * c + pltpu.roll(x, HEAD_DIM // 2, 1) * s


def _attn_lat_kernel(q_ref, kp_ref, kc_ref, kn_ref, vp_ref, vc_ref, vn_ref, ck_ref, cv_ref,
                     cq, sq, cp, sp, cn, sn, sink_ref, o_ref, *, nb):
    i = pl.program_id(1)
    scale = HEAD_DIM ** -0.5
    shape = (GQA_GROUP * BLOCK, 3 * BLOCK)
    qi = lax.broadcasted_iota(jnp.int32, shape, 0) & (BLOCK - 1)
    kj = lax.broadcasted_iota(jnp.int32, shape, 1)
    ok = jnp.abs(kj - BLOCK - qi) <= WINDOW
    ok = ok & ((kj >= BLOCK) | (i > 0)) & ((kj < 2 * BLOCK) | (i < nb - 1))
    for h in range(N_KV_HEADS):
        hs = slice(h * HEAD_DIM, (h + 1) * HEAD_DIM)
        kw = jnp.concatenate([_rope(kp_ref[:, hs], cp[...], sp[...]),
                              _rope(kc_ref[:, hs], cq[...], sq[...]),
                              _rope(kn_ref[:, hs], cn[...], sn[...])], axis=0).astype(BF16)
        vw = jnp.concatenate([vp_ref[:, hs], vc_ref[:, hs], vn_ref[:, hs]], axis=0).astype(BF16)
        kctx = ck_ref[:, hs].astype(BF16)
        vctx = cv_ref[:, hs].astype(BF16)
        q = jnp.concatenate([_rope(q_ref[:, hq * HEAD_DIM:(hq + 1) * HEAD_DIM], cq[...], sq[...])
                             for hq in range(h * GQA_GROUP, (h + 1) * GQA_GROUP)], axis=0).astype(BF16)
        s1 = _dot_nt(q, kctx) * scale
        s2 = jnp.where(ok, _dot_nt(q, kw) * scale, NEG_INF)
        _store_group(o_ref, h, _softmax_sink_pv([(s1, vctx), (s2, vw)], _group_sink(sink_ref, h)))


def _attn_lat(cfg, consts, z, ck, cv, sink):
    nb = cfg.ll // BLOCK
    rb0 = cfg.rc // BLOCK
    qw = N_Q_HEADS * HEAD_DIM
    past = ck.shape[1]
    cos2, sin2 = consts["rope"]

    def blk(delta):
        return lambda b, i: rb0 + b * nb + jnp.clip(i + delta, 0, nb - 1)

    def zspec(width, col0, delta):
        rowf = blk(delta)
        return pl.BlockSpec((BLOCK, width), lambda b, i: (rowf(b, i), col0 // width))

    def tspec(delta):
        return pl.BlockSpec((BLOCK, HEAD_DIM), lambda b, i: (jnp.clip(i + delta, 0, nb - 1), 0))

    cspec = pl.BlockSpec((None, past, KV_COLS), lambda b, i: (b, 0, 0))
    return pl.pallas_call(
        functools.partial(_attn_lat_kernel, nb=nb),
        grid=(cfg.nl, nb),
        in_specs=[zspec(qw, Q_COL0, 0),
                  zspec(KV_COLS, K_COL0, -1), zspec(KV_COLS, K_COL0, 0), zspec(KV_COLS, K_COL0, 1),
                  zspec(KV_COLS, V_COL0, -1), zspec(KV_COLS, V_COL0, 0), zspec(KV_COLS, V_COL0, 1),
                  cspec, cspec,
                  tspec(0), tspec(0), tspec(-1), tspec(-1), tspec(1), tspec(1),
                  pl.BlockSpec((N_Q_HEADS, LANES), lambda b, i: (0, 0))],
        out_specs=pl.BlockSpec((BLOCK, qw), lambda b, i: (b * nb + i, 0)),
        out_shape=jax.ShapeDtypeStruct((cfg.rl, qw), BF16),
        compiler_params=_cparams(2, _nbytes((BLOCK, qw), F32), 6 * _nbytes((BLOCK, KV_COLS), F32),
                                 2 * _nbytes((past, KV_COLS), F32)),
        name="attn_lat",
    )(z, z, z, z, z, z, z, ck, cv, cos2, sin2, cos2, sin2, cos2, sin2, sink)


def _rope_tables(length):
    rows = length // GRID_W
    row = jnp.repeat(jnp.arange(rows, dtype=F32), GRID_W)
    col = jnp.tile(jnp.arange(GRID_W, dtype=F32), rows)
    n_freq = HEAD_DIM // 4
    inv = ROPE_BASE ** (-jnp.arange(n_freq, dtype=F32) / n_freq)
    ang = jnp.concatenate([row[:, None] * inv, col[:, None] * inv], axis=-1)
    cos, sin = jnp.cos(ang), jnp.sin(ang)
    return jnp.concatenate([cos, cos], axis=-1), jnp.concatenate([-sin, sin], axis=-1)


def _s5_operands(p):
    t_len = S5_CHUNK
    lr, li = p["s5_lam_re"].astype(F32), p["s5_lam_im"].astype(F32)
    n_layers = lr.shape[0]
    dt = jnp.exp(p["s5_log_dt"].astype(F32))[..., None]
    zr, zi = lr * dt, li * dt
    mag = jnp.exp(zr)
    ar, ai = mag * jnp.cos(zi), mag * jnp.sin(zi)
    den = lr * lr + li * li
    cr = ((ar - 1.0) * lr + ai * li) / den
    ci = (ai * lr - (ar - 1.0) * li) / den
    k = jnp.arange(t_len + 1, dtype=F32)[:, None, None, None, None]
    pm = jnp.exp(k * zr)
    pr, pi = pm * jnp.cos(k * zi), pm * jnp.sin(k * zi)

    bt_re = p["s5_b_re"].astype(F32).transpose(0, 1, 2, 4, 3)
    bt_im = p["s5_b_im"].astype(F32).transpose(0, 1, 2, 4, 3)
    bbr = cr[:, :, :, None, :] * bt_re - ci[:, :, :, None, :] * bt_im
    bbi = cr[:, :, :, None, :] * bt_im + ci[:, :, :, None, :] * bt_re
    pk_r, pk_i = pr[:t_len, :, :, :, None, :], pi[:t_len, :, :, :, None, :]
    ab = jnp.concatenate([pk_r * bbr - pk_i * bbi, pk_r * bbi + pk_i * bbr], axis=-1)
    ab = ab.reshape(t_len, n_layers, 2, D_BRANCH, 2 * S5_STATE).transpose(1, 2, 0, 3, 4)
    pf = jnp.concatenate([jnp.flip(ab[:, 0:1], axis=2), ab[:, 1:2]], axis=1)

    c_re, c_im = p["s5_c_re"].astype(F32), p["s5_c_im"].astype(F32)
    cmat = jnp.concatenate([c_re, -c_im], axis=-1).reshape(n_layers, 2, D_BRANCH, 2 * S5_STATE)
    ct_re = c_re.reshape(n_layers, 2, D_BRANCH, S5_STATE).transpose(0, 1, 3, 2)
    ct_im = c_im.reshape(n_layers, 2, D_BRANCH, S5_STATE).transpose(0, 1, 3, 2)
    pt_r = jnp.repeat(pr[1:].transpose(0, 1, 2, 4, 3), S5_CH, axis=-1)
    pt_i = jnp.repeat(pi[1:].transpose(0, 1, 2, 4, 3), S5_CH, axis=-1)
    q = jnp.concatenate([ct_re * pt_r - ct_im * pt_i, -(ct_re * pt_i + ct_im * pt_r)], axis=3)
    q = q.transpose(1, 2, 0, 3, 4)
    qf = jnp.concatenate([q[:, 0:1], jnp.flip(q[:, 1:2], axis=2)], axis=1)

    a_chunk = jnp.stack([jnp.concatenate([pr[t_len], pr[t_len]], axis=-1),
                         jnp.concatenate([-pi[t_len], pi[t_len]], axis=-1)], axis=2)
    return pf, qf, cmat, a_chunk


def _s5_ucat(z_ref):
    return jnp.concatenate([z_ref[:, t, :].astype(BF16) for t in range(S5_CHUNK)], axis=1)


def _octet_group(shape, axis):
    return lax.broadcasted_iota(jnp.int32, shape, axis) // S5_CH


def _s5_in_kernel(z_ref, pf_ref, x_ref, p_scr):
    gpo = S5_GROUPS // S5_OCTETS

    @pl.when(pl.program_id(1) == 0)
    def _():
        row_group = _octet_group((LANES, LANES), 0)
        for d in range(2):
            for t in range(S5_CHUNK):
                blk = pf_ref[d, t]
                for g in range(gpo):
                    p_scr[d, t * LANES:(t + 1) * LANES, g * LANES:(g + 1) * LANES] = (
                        jnp.where(row_group == g, blk, 0.0).astype(BF16))

    u = _s5_ucat(z_ref)
    for d in range(2):
        x_ref[d] = _dot(u, p_scr[d]).reshape(u.shape[0], gpo, 2 * S5_STATE)


def _s5_in(z3, pf, l, tr):
    r16 = z3.shape[0]
    gpo = S5_GROUPS // S5_OCTETS
    w = 2 * S5_STATE
    return pl.pallas_call(
        _s5_in_kernel,
        grid=(S5_OCTETS, r16 // tr),
        in_specs=[pl.BlockSpec((tr, S5_CHUNK, LANES), lambda o, i: (i, 0, S5_COL0 // LANES + o)),
                  pl.BlockSpec((None, 2, S5_CHUNK, LANES, w), lambda o, i: (l, 0, 0, o, 0))],
        out_specs=pl.BlockSpec((2, tr, gpo, w), lambda o, i: (0, i, o, 0)),
        out_shape=jax.ShapeDtypeStruct((2, r16, S5_GROUPS, w), F32),
        scratch_shapes=[pltpu.VMEM((2, S5_OCT_K, gpo * w), BF16)],
        compiler_params=_cparams(2, _nbytes((tr, S5_CHUNK, LANES), F32), _nbytes((2, S5_CHUNK, LANES, w), F32),
                                 _nbytes((2, tr, gpo, w), F32),
                                 extra=_nbytes((2, S5_OCT_K, gpo * w), BF16) + 4 * _nbytes((tr, S5_OCT_K), F32)),
        name="s5_in",
    )(z3, pf)


def _s5_scan_kernel(*refs, n_in, nsb, jb):
    a_ref, h0_ref = refs[0], refs[1]
    xf = refs[2:2 + n_in]
    xb = refs[2 + n_in:2 + 2 * n_in]
    sf = refs[2 + 2 * n_in:2 + 3 * n_in]
    sb = refs[2 + 3 * n_in:2 + 4 * n_in]
    fin_ref, st_ref = refs[2 + 4 * n_in], refs[3 + 4 * n_in]
    t = pl.program_id(0)

    @pl.when(t == 0)
    def _():
        st_ref[...] = h0_ref[...]

    half = S5_STATE

    def body(s, carry):
        jf = s
        jr = jb - 1 - s
        for r in range(n_in):
            for q in range(nsb):
                idx = r * nsb + q
                cur = st_ref[0, idx]
                sf[r][q * jb + jf] = cur
                st_ref[0, idx] = cur * a_ref[0, 0] + pltpu.roll(cur, half, 1) * a_ref[0, 1] + xf[r][q * jb + jf]
                cur = st_ref[1, idx]
                sb[r][q * jb + jr] = cur
                st_ref[1, idx] = cur * a_ref[1, 0] + pltpu.roll(cur, half, 1) * a_ref[1, 1] + xb[r][q * jb + jr]
        return carry

    lax.fori_loop(0, jb, body, 0)

    @pl.when(t == pl.num_programs(0) - 1)
    def _():
        fin_ref[...] = st_ref[...]


def _s5_scan(x, a_chunk, h0, l, row0, n_in, nsb, nj, jb):
    g = x.shape[2]
    w = x.shape[3]
    nblk = nj // jb
    nseq = n_in * nsb
    rows = nsb * jb
    base = row0 // rows

    def xspec(d, r):
        if d == 0:
            return pl.BlockSpec((None, rows, g, w), lambda t: (0, base + r * nblk + t, 0, 0))
        return pl.BlockSpec((None, rows, g, w), lambda t: (1, base + r * nblk + nblk - 1 - t, 0, 0))

    def sspec(d):
        if d == 0:
            return pl.BlockSpec((rows, g, w), lambda t: (t, 0, 0))
        return pl.BlockSpec((rows, g, w), lambda t: (nblk - 1 - t, 0, 0))

    st_spec = pl.BlockSpec((2, nseq, g, w), lambda t: (0, 0, 0, 0))
    res = pl.pallas_call(
        functools.partial(_s5_scan_kernel, n_in=n_in, nsb=nsb, jb=jb),
        grid=(nblk,),
        in_specs=[pl.BlockSpec((None, 2, 2, g, w), lambda t: (l, 0, 0, 0, 0)), st_spec]
                 + [xspec(0, r) for r in range(n_in)] + [xspec(1, r) for r in range(n_in)],
        out_specs=[sspec(0)] * n_in + [sspec(1)] * n_in + [st_spec],
        out_shape=[jax.ShapeDtypeStruct((nsb * nj, g, w), F32)] * (2 * n_in)
                  + [jax.ShapeDtypeStruct((2, nseq, g, w), F32)],
        scratch_shapes=[pltpu.VMEM((2, nseq, g, w), F32)],
        compiler_params=_cparams(1, 4 * n_in * _nbytes((rows, g, w), F32), 3 * _nbytes((2, nseq, g, w), F32)),
        name="s5_scan",
    )(a_chunk, h0, *([x] * (2 * n_in)))
    return res[:n_in], res[n_in:2 * n_in], res[2 * n_in]


def _s5_out_kernel(z_ref, pf_ref, cm_ref, qf_ref, d_ref, *rest, starts):
    n = len(starts)
    sf_refs, sb_refs = rest[:n], rest[n:2 * n]
    y_ref, m_scr, q_scr = rest[2 * n:]
    gpo = S5_GROUPS // S5_OCTETS

    @pl.when(pl.program_id(1) == 0)
    def _():
        same_group = _octet_group((LANES, LANES), 0) == _octet_group((LANES, LANES), 1)

        def lag_op(d, k):
            ab = pf_ref[d, S5_CHUNK - 1 - k] if d == 0 else pf_ref[d, k]
            ah, al = _split(ab)
            ch, cl = _split(cm_ref[d])
            return jnp.where(same_group, _dot_nt(ah, ch) + (_dot_nt(ah, cl) + _dot_nt(al, ch)), 0.0)

        fwd = [lag_op(0, k) for k in range(S5_CHUNK)]
        bwd = [lag_op(1, k) for k in range(S5_CHUNK)]
        blocks = {0: (fwd[0] + bwd[0]).astype(BF16)}
        for k in range(1, S5_CHUNK):
            blocks[k] = fwd[k].astype(BF16)
            blocks[-k] = bwd[k].astype(BF16)
        for t in range(S5_CHUNK):
            for t2 in range(S5_CHUNK):
                m_scr[t * LANES:(t + 1) * LANES, t2 * LANES:(t2 + 1) * LANES] = blocks[t2 - t]
        lane_group = _octet_group((LANES, LANES), 1)
        for d in range(2):
            for t in range(S5_CHUNK):
                blk = qf_ref[d, t]
                for g in range(gpo):
                    q_scr[d, g * LANES:(g + 1) * LANES, t * LANES:(t + 1) * LANES] = (
                        jnp.where(lane_group == g, blk, 0.0).astype(BF16))

    u = _s5_ucat(z_ref)
    rows = u.shape[0]
    width = gpo * 2 * S5_STATE

    def entering_state(refs):
        s = refs[0][...]
        for ref, start in zip(refs[1:], starts[1:]):
            s = jnp.where(pl.program_id(1) >= start, ref[...], s)
        return s.reshape(rows, width).astype(BF16)

    y = _dot(u, m_scr[...])
    y += _dot(entering_state(sf_refs), q_scr[0])
    y += _dot(entering_state(sb_refs), q_scr[1])
    for t in range(S5_CHUNK):
        y_ref[:, t, :] = y[:, t * LANES:(t + 1) * LANES] + z_ref[:, t, :] * d_ref[...]


def _s5_out(z3, pf, cmat, qf, d_oct, sin_f, sin_b, l, tr):
    r16 = z3.shape[0]
    gpo = S5_GROUPS // S5_OCTETS
    w = 2 * S5_STATE
    tiles = [s.shape[0] // tr for s in sin_f]
    starts = tuple(sum(tiles[:n]) for n in range(len(tiles)))
    sspecs = [pl.BlockSpec((tr, gpo, w), functools.partial(
        lambda o, i, start, count: (jnp.clip(i - start, 0, count - 1), o, 0), start=start, count=count))
        for start, count in zip(starts, tiles)]
    return pl.pallas_call(
        functools.partial(_s5_out_kernel, starts=starts),
        grid=(S5_OCTETS, r16 // tr),
        in_specs=[pl.BlockSpec((tr, S5_CHUNK, LANES), lambda o, i: (i, 0, S5_COL0 // LANES + o)),
                  pl.BlockSpec((None, 2, S5_CHUNK, LANES, w), lambda o, i: (l, 0, 0, o, 0)),
                  pl.BlockSpec((None, 2, LANES, w), lambda o, i: (l, 0, o, 0)),
                  pl.BlockSpec((None, 2, S5_CHUNK, w, LANES), lambda o, i: (l, 0, 0, 0, o)),
                  pl.BlockSpec((None, None, 1, LANES), lambda o, i: (l, o, 0, 0)),
                  *sspecs, *sspecs],
        out_specs=pl.BlockSpec((tr, S5_CHUNK, LANES), lambda o, i: (i, 0, o)),
        out_shape=jax.ShapeDtypeStruct((r16, S5_CHUNK, D_BRANCH), F32),
        scratch_shapes=[pltpu.VMEM((S5_OCT_K, S5_OCT_K), BF16), pltpu.VMEM((2, gpo * w, S5_OCT_K), BF16)],
        compiler_params=_cparams(2, 2 * _nbytes((tr, S5_CHUNK, LANES), F32), 2 * _nbytes((2, S5_CHUNK, LANES, w), F32),
                                 2 * len(tiles) * _nbytes((tr, gpo, w), F32),
                                 extra=_nbytes((S5_OCT_K, S5_OCT_K), BF16) + _nbytes((2, gpo * w, S5_OCT_K), BF16)
                                 + 4 * _nbytes((tr, S5_OCT_K), F32)),
        name="s5_out",
    )(z3, pf, cmat, qf, d_oct, *sin_f, *sin_b)


def _s5_glu_kernel(y_ref, w_ref, b_ref, o_ref, wb_ref):
    @pl.when(pl.program_id(0) == 0)
    def _():
        wb_ref[...] = w_ref[...].astype(BF16)

    y = _gelu(y_ref[...])
    o_ref[...] = (y * _sigmoid(_dot(y.astype(BF16), wb_ref[...]) + b_ref[...])).astype(BF16)


def _s5_glu(cfg, y, w, b, l):
    tm = min(512, cfg.ll)
    return pl.pallas_call(
        _s5_glu_kernel,
        grid=(cfg.r // tm,),
        in_specs=[pl.BlockSpec((tm, D_BRANCH), lambda i: (i, 0)),
                  pl.BlockSpec((None, D_BRANCH, D_BRANCH), lambda i: (l, 0, 0)),
                  pl.BlockSpec((None, 1, D_BRANCH), lambda i: (l, 0, 0))],
        out_specs=pl.BlockSpec((tm, D_BRANCH), lambda i: (i, 0)),
        out_shape=jax.ShapeDtypeStruct((cfg.r, D_BRANCH), BF16),
        scratch_shapes=[pltpu.VMEM((D_BRANCH, D_BRANCH), BF16)],
        compiler_params=_cparams(1, 2 * _nbytes((tm, D_BRANCH), F32), _nbytes((D_BRANCH, D_BRANCH), F32),
                                 extra=_nbytes((D_BRANCH, D_BRANCH), BF16) + 4 * _nbytes((tm, D_BRANCH), F32)),
        name="s5_glu",
    )(y, w, b.reshape(DEPTH, 1, D_BRANCH))


def _s5(cfg, s5m, z, p, h0_lat, l):
    pf, qf, cmat, a_chunk = s5m
    r16 = cfg.r // S5_CHUNK
    tr = min(256, cfg.ll // S5_CHUNK)
    z3 = z.reshape(r16, S5_CHUNK, N_IN)
    x = _s5_in(z3, pf, l, tr)
    njc, njl = cfg.lc // S5_CHUNK, cfg.ll // S5_CHUNK
    rc16 = cfg.rc // S5_CHUNK
    zero_h0 = jnp.zeros((2, cfg.nc, S5_GROUPS, 2 * S5_STATE), F32)
    sf_c, sb_c, fin_ctx = _s5_scan(x, a_chunk, zero_h0, l, 0, 1, cfg.nc, njc, njc)
    sf_l, sb_l, _ = _s5_scan(x, a_chunk, h0_lat, l, rc16, cfg.nl, 1, njl, min(32, njl))
    sin_f = list(sf_c) + list(sf_l)
    sin_b = list(sb_c) + list(sb_l)
    d_oct = p["s5_d"].reshape(DEPTH, S5_OCTETS, 1, LANES)
    y = _s5_out(z3, pf, cmat, qf, d_oct, sin_f, sin_b, l, tr).reshape(cfg.r, D_BRANCH)
    return _s5_glu(cfg, y, p["s5_glu_w"], p["s5_glu_b"], l), fin_ctx


def _forward(cfg, x_prompt, x_sample, c, cache_k, cache_v, state_ssm_re, state_ssm_im, c_ctx, p):
    d = D_MODEL
    x = (x_prompt.reshape(cfg.rc, d), x_sample.reshape(cfg.rl, d))
    cvec = jnp.concatenate([c_ctx[None], c, jnp.zeros((MOD_ROWS - cfg.nseg, d), F32)], axis=0)
    mod = _mod(cvec, p["w_mod"], p["b_mod"])[:, :cfg.nseg].reshape(DEPTH, cfg.nseg, 6, 1, d)
    mod = [[mod[l, :, i] for i in range(6)] for l in range(DEPTH)]
    norm_g = p["norm_g"]

    consts = {
        "dft": {},
        "feats": {n: _hy_feats(n) for n in {cfg.lc, cfg.ll}},
        "rope": _rope_tables(cfg.ll),
    }
    for n in {cfg.lc, cfg.ll}:
        consts["dft"][n] = _dft_tables(n)
    s5m = _s5_operands(p)

    kv_shape = (cfg.nc, cfg.lc, N_KV_HEADS, HEAD_DIM)
    ks, vs, srs, sis = [], [], [], []
    h = _resid_norm(cfg, x, gpre=norm_g[0, 0], sc=mod[0][1], sh=mod[0][0])
    for l in range(DEPTH):
        z = _mm(h, p["w_in"], l, tm=cfg.tm, tn=512)
        ks.append(z[:cfg.rc, K_COL0:K_COL0 + KV_COLS].reshape(kv_shape))
        vs.append(z[:cfg.rc, V_COL0:V_COL0 + KV_COLS].reshape(kv_shape))

        y_hy = _hyena(cfg, consts, z, p, l)
        sink = jnp.broadcast_to(p["attn_sink"][l][:, None], (N_Q_HEADS, LANES))
        past = cache_k.shape[2]
        y_at = jnp.concatenate([
            _attn_ctx(cfg, z, sink),
            _attn_lat(cfg, consts, z, cache_k[:, l].reshape(cfg.nl, past, KV_COLS),
                      cache_v[:, l].reshape(cfg.nl, past, KV_COLS), sink)], axis=0)
        h0 = jnp.concatenate([state_ssm_re[:, l], state_ssm_im[:, l]], axis=-1).transpose(1, 0, 2, 3)
        y_s5, fin = _s5(cfg, s5m, z, p, h0, l)
        srs.append(fin[..., :S5_STATE].transpose(1, 0, 2, 3))
        sis.append(fin[..., S5_STATE:].transpose(1, 0, 2, 3))

        merged = _merge(cfg, z, (y_hy, y_at, y_s5),
                        (p["w_branch_hy"], p["w_branch_attn"], p["w_branch_s5"]), l)
        y = _mm(merged, p["w_out"], l, tm=cfg.tm, tn=512)
        x, h = _resid_norm(cfg, x, y, gate=mod[l][2], gpost=norm_g[l, 1],
                           gpre=norm_g[l, 2], sc=mod[l][4], sh=mod[l][3])
        u = _mm(h, p["ffn_w_up"], l, tm=cfg.tm, tn=512)
        act = _ffn_act(cfg, u, p["ffn_conv_w"], p["ffn_conv_b"], l)
        f = _mm(act, p["ffn_w_down"], l, tm=min(512, cfg.tm), tn=512, w_buffers=1)
        if l + 1 < DEPTH:
            x, h = _resid_norm(cfg, x, f, gate=mod[l][5], gpost=norm_g[l, 3],
                               gpre=norm_g[l + 1, 0], sc=mod[l + 1][1], sh=mod[l + 1][0])
        else:
            x = _resid_norm(cfg, x, f, gate=mod[l][5], gpost=norm_g[l, 3], split_out=True)

    return (x[0].reshape(cfg.nc, cfg.lc, d), x[1].reshape(cfg.nl, cfg.ll, d),
            jnp.stack(ks, axis=1), jnp.stack(vs, axis=1), jnp.stack(srs, axis=1), jnp.stack(sis, axis=1))


def kernel(x_prompt, x_sample, c, cache_k, cache_v, state_ssm_re, state_ssm_im, c_ctx, w_mod, b_mod, norm_g, w_in, hy_conv_w, hy_conv_b, hy_w1, hy_b1, hy_w2, hy_b2, hy_w3, hy_freq, hy_decay, hy_bias, attn_sink, s5_lam_re, s5_lam_im, s5_log_dt, s5_b_re, s5_b_im, s5_c_re, s5_c_im, s5_d, s5_glu_w, s5_glu_b, w_branch_hy, w_branch_attn, w_branch_s5, w_out, ffn_w_up, ffn_conv_w, ffn_conv_b, ffn_w_down):
    p = dict(w_mod=w_mod, b_mod=b_mod, norm_g=norm_g, w_in=w_in, hy_conv_w=hy_conv_w, hy_conv_b=hy_conv_b,
             hy_w1=hy_w1, hy_b1=hy_b1, hy_w2=hy_w2, hy_b2=hy_b2, hy_w3=hy_w3, hy_freq=hy_freq,
             hy_decay=hy_decay, hy_bias=hy_bias, attn_sink=attn_sink, s5_lam_re=s5_lam_re,
             s5_lam_im=s5_lam_im, s5_log_dt=s5_log_dt, s5_b_re=s5_b_re, s5_b_im=s5_b_im, s5_c_re=s5_c_re,
             s5_c_im=s5_c_im, s5_d=s5_d, s5_glu_w=s5_glu_w, s5_glu_b=s5_glu_b, w_branch_hy=w_branch_hy,
             w_branch_attn=w_branch_attn, w_branch_s5=w_branch_s5, w_out=w_out, ffn_w_up=ffn_w_up,
             ffn_conv_w=ffn_conv_w, ffn_conv_b=ffn_conv_b, ffn_w_down=ffn_w_down)
    cfg = _Cfg(x_prompt.shape[0], x_prompt.shape[1], x_sample.shape[0], x_sample.shape[1])
    return _forward(cfg, x_prompt, x_sample, c, cache_k, cache_v, state_ssm_re, state_ssm_im, c_ctx, p)
```

```python
import functools
import math

import numpy as np
import jax
import jax.numpy as jnp
from jax import lax
from jax.experimental import pallas as pl
from jax.experimental.pallas import tpu as pltpu

F32 = jnp.float32
BF16 = jnp.bfloat16

D_MODEL = 4096
DEPTH = 2
GRID_W = 64
D_BRANCH = D_MODEL // 4
HY_WIDTH = D_BRANCH
HY_BANDS = 16
HY_FEAT = 1 + 2 * HY_BANDS
HY_HID = 64
HEAD_DIM = 128
N_Q_HEADS = D_BRANCH // HEAD_DIM
N_KV_HEADS = 2
GQA_GROUP = N_Q_HEADS // N_KV_HEADS
WINDOW = 128
BLOCK = 128
ROPE_BASE = 10000.0
S5_CH = 16
S5_GROUPS = D_BRANCH // S5_CH
S5_STATE = 64
S5_CHUNK = 16
S5_OCTETS = 8
S5_OCT_K = S5_CHUNK * 128
D_FF = 2 * D_MODEL
EPS = 1e-6
NEG_INF = -1e30

HY_COLS = 3 * HY_WIDTH
Q_COL0 = HY_COLS
K_COL0 = Q_COL0 + N_Q_HEADS * HEAD_DIM
KV_COLS = N_KV_HEADS * HEAD_DIM
V_COL0 = K_COL0 + KV_COLS
S5_COL0 = V_COL0 + KV_COLS
GATE_COL0 = S5_COL0 + D_BRANCH
N_IN = GATE_COL0 + 3 * D_MODEL

V7X_VMEM_REQUEST_MAX = 60 * 1024 * 1024
LANES = 128
SUBLANES = 8
MOD_ROWS = 8


def _nbytes(shape, dtype):
    return math.prod(shape) * jnp.dtype(dtype).itemsize


def _cparams(n_grid, *block_bytes, extra=0):
    est = 2 * sum(block_bytes) + extra + (4 << 20)
    return pltpu.CompilerParams(
        dimension_semantics=("arbitrary",) * n_grid,
        vmem_limit_bytes=int(min(max(est, 16 << 20), V7X_VMEM_REQUEST_MAX)))


def _dot(a, b):
    return jnp.dot(a, b, preferred_element_type=F32)


def _dot_nt(a, b):
    return lax.dot_general(a, b, (((1,), (1,)), ((), ())), preferred_element_type=F32)


def _split(a):
    hi = a.astype(BF16)
    return hi, (a - hi.astype(F32)).astype(BF16)


def _dot3(a, b):
    ah, al = _split(a)
    bh, bl = _split(b)
    return _dot(ah, bh) + (_dot(ah, bl) + _dot(al, bh))


def _sigmoid(x):
    return 1.0 / (1.0 + jnp.exp(-x))


def _gelu(x):
    c = math.sqrt(2.0 / math.pi)
    return x * (0.5 + 0.5 * jnp.tanh(x * (c + (c * 0.044715) * (x * x))))


def _rms(x, g):
    return x * lax.rsqrt(jnp.mean(x * x, axis=-1, keepdims=True) + EPS) * g


class _Cfg:
    def __init__(self, nc, lc, nl, ll):
        self.nc, self.lc, self.nl, self.ll = nc, lc, nl, ll
        self.rc = nc * lc
        self.rl = nl * ll
        self.r = self.rc + self.rl
        self.nseg = 1 + nl
        assert self.rc % ll == 0 and ll % lc == 0 and lc % BLOCK == 0
        self.tm = min(1024, ll)
        self.rb = ll

    def seg_of_row(self, row0):
        return jnp.where(row0 >= self.rc, (row0 - self.rc) // self.ll + 1, 0)


def _mod_kernel(c_ref, w_ref, b_ref, o_ref):
    c = c_ref[...]
    s = (c * _sigmoid(c)).astype(BF16)
    o_ref[...] = _dot(s, w_ref[...].astype(BF16)) + b_ref[...]


def _mod(cvec, w_mod, b_mod):
    n = w_mod.shape[-1]
    tn = 512
    return pl.pallas_call(
        _mod_kernel,
        grid=(DEPTH, n // tn),
        in_specs=[pl.BlockSpec((MOD_ROWS, D_MODEL), lambda l, j: (0, 0)),
                  pl.BlockSpec((None, D_MODEL, tn), lambda l, j: (l, 0, j)),
                  pl.BlockSpec((None, 1, tn), lambda l, j: (l, 0, j))],
        out_specs=pl.BlockSpec((None, MOD_ROWS, tn), lambda l, j: (l, 0, j)),
        out_shape=jax.ShapeDtypeStruct((DEPTH, MOD_ROWS, n), F32),
        compiler_params=_cparams(2, _nbytes((D_MODEL, tn), F32), extra=_nbytes((D_MODEL, tn), BF16)),
        name="mod",
    )(cvec, w_mod, b_mod.reshape(DEPTH, 1, n))


def _resid_norm_kernel(*refs, has_y, has_h, split_in, split_out, n_ctx_tiles):
    it = iter(refs)
    in_ctx = pl.program_id(0) < n_ctx_tiles
    if split_in:
        xa_ref, xb_ref = next(it), next(it)
        x = jnp.where(in_ctx, xa_ref[...], xb_ref[...])
    else:
        x = next(it)[...]
    if has_y:
        y_ref, gate_ref, gpost_ref = next(it), next(it), next(it)
    if has_h:
        gpre_ref, sc_ref, sh_ref = next(it), next(it), next(it)
    if has_y:
        x = x + gate_ref[...] * _rms(y_ref[...], gpost_ref[...])
        if split_out:
            xa_out, xb_out = next(it), next(it)

            @pl.when(in_ctx)
            def _():
                xa_out[...] = x

            @pl.when(jnp.logical_not(in_ctx))
            def _():
                xb_out[...] = x
        else:
            xo_ref = next(it)
            xo_ref[...] = x
    if has_h:
        ho_ref = next(it)
        ho_ref[...] = (_rms(x, gpre_ref[...]) * (1.0 + sc_ref[...]) + sh_ref[...]).astype(BF16)


def _resid_norm(cfg, x, y=None, gate=None, gpost=None, gpre=None, sc=None, sh=None, split_out=False):
    has_y, has_h, split_in = y is not None, gpre is not None, isinstance(x, tuple)
    tm = 256
    nca = cfg.rc // tm
    row = pl.BlockSpec((tm, D_MODEL), lambda i: (i, 0))
    row_a = pl.BlockSpec((tm, D_MODEL), lambda i: (jnp.minimum(i, nca - 1), 0))
    row_b = pl.BlockSpec((tm, D_MODEL), lambda i: (jnp.maximum(i - nca, 0), 0))
    vec = pl.BlockSpec((1, D_MODEL), lambda i: (0, 0))
    seg = pl.BlockSpec((None, 1, D_MODEL), lambda i: (cfg.seg_of_row(i * tm), 0, 0))
    args, specs = (list(x), [row_a, row_b]) if split_in else ([x], [row])
    outs, ospecs = [], []
    if has_y:
        args += [y, gate, gpost.reshape(1, D_MODEL)]
        specs += [row, seg, vec]
        if split_out:
            outs += [jax.ShapeDtypeStruct((cfg.rc, D_MODEL), F32), jax.ShapeDtypeStruct((cfg.rl, D_MODEL), F32)]
            ospecs += [row_a, row_b]
        else:
            outs.append(jax.ShapeDtypeStruct((cfg.r, D_MODEL), F32))
            ospecs.append(row)
    if has_h:
        args += [gpre.reshape(1, D_MODEL), sc, sh]
        specs += [vec, seg, seg]
        outs.append(jax.ShapeDtypeStruct((cfg.r, D_MODEL), BF16))
        ospecs.append(row)
    res = pl.pallas_call(
        functools.partial(_resid_norm_kernel, has_y=has_y, has_h=has_h, split_in=split_in, split_out=split_out,
                          n_ctx_tiles=nca),
        grid=(cfg.r // tm,),
        in_specs=specs, out_specs=ospecs, out_shape=outs,
        compiler_params=_cparams(1, 5 * _nbytes((tm, D_MODEL), F32)),
        name="resid_norm",
    )(*args)
    return res if len(res) > 1 else res[0]


def _mm_kernel(x_ref, w_ref, o_ref, wb_ref):
    @pl.when(pl.program_id(1) == 0)
    def _():
        wb_ref[...] = w_ref[...].astype(BF16)

    o_ref[...] = _dot(x_ref[...], wb_ref[...]).astype(o_ref.dtype)


def _mm(x, w, l, *, tm, tn, out_dtype=F32, w_buffers=2):
    m, k = x.shape
    n = w.shape[-1]
    w_mode = {} if w_buffers == 2 else {"pipeline_mode": pl.Buffered(w_buffers)}
    return pl.pallas_call(
        _mm_kernel,
        grid=(n // tn, m // tm),
        in_specs=[pl.BlockSpec((tm, k), lambda j, i: (i, 0)),
                  pl.BlockSpec((None, k, tn), lambda j, i: (l, 0, j), **w_mode)],
        out_specs=pl.BlockSpec((tm, tn), lambda j, i: (i, j)),
        out_shape=jax.ShapeDtypeStruct((m, n), out_dtype),
        scratch_shapes=[pltpu.VMEM((k, tn), BF16)],
        compiler_params=_cparams(2, _nbytes((tm, k), BF16), _nbytes((tm, tn), out_dtype),
                                 extra=w_buffers * _nbytes((k, tn), F32) + _nbytes((k, tn), BF16)
                                 + _nbytes((tm, tn), F32)),
        name="mm",
    )(x, w)


def _merge_kernel(g0, g1, g2, y0, y1, y2, w0, w1, w2, o_ref, wb_ref):
    @pl.when(pl.program_id(1) == 0)
    def _():
        wb_ref[0] = w0[...].astype(BF16)
        wb_ref[1] = w1[...].astype(BF16)
        wb_ref[2] = w2[...].astype(BF16)

    acc = _sigmoid(g0[...]) * _dot(y0[...], wb_ref[0])
    acc += _sigmoid(g1[...]) * _dot(y1[...], wb_ref[1])
    acc += _sigmoid(g2[...]) * _dot(y2[...], wb_ref[2])
    o_ref[...] = acc.astype(o_ref.dtype)


def _merge(cfg, z, ys, ws, l):
    tm, tn = cfg.tm, 512
    gate_specs = [pl.BlockSpec((tm, tn), functools.partial(
        lambda j, i, b: (i, (GATE_COL0 + b * D_MODEL) // tn + j), b=b)) for b in range(3)]
    y_spec = pl.BlockSpec((tm, D_BRANCH), lambda j, i: (i, 0))
    w_spec = pl.BlockSpec((None, D_BRANCH, tn), lambda j, i: (l, 0, j))
    return pl.pallas_call(
        _merge_kernel,
        grid=(D_MODEL // tn, cfg.r // tm),
        in_specs=gate_specs + [y_spec] * 3 + [w_spec] * 3,
        out_specs=pl.BlockSpec((tm, tn), lambda j, i: (i, j)),
        out_shape=jax.ShapeDtypeStruct((cfg.r, D_MODEL), BF16),
        scratch_shapes=[pltpu.VMEM((3, D_BRANCH, tn), BF16)],
        compiler_params=_cparams(2, 3 * _nbytes((tm, tn), F32), 3 * _nbytes((tm, D_BRANCH), BF16),
                                 3 * _nbytes((D_BRANCH, tn), F32), _nbytes((tm, tn), BF16),
                                 extra=3 * _nbytes((D_BRANCH, tn), BF16) + 2 * _nbytes((tm, tn), F32)),
        name="merge",
    )(z, z, z, *ys, *ws)


def _block_seq_len(cfg):
    return jnp.where(pl.program_id(0) < cfg.rc // cfg.rb, cfg.lc, cfg.ll)


def _dwconv3_rows(x_ref, w_ref, b_ref, r0, sl):
    total = x_ref.shape[0]
    cur = x_ref[r0:r0 + BLOCK, :]
    sub = lax.broadcasted_iota(jnp.int32, (SUBLANES, cur.shape[1]), 0)
    xm = x_ref[r0 - 1:r0 - 1 + BLOCK, :] if r0 > 0 else pltpu.roll(cur, 1, 0)
    first = jnp.where(((r0 + sub) & (sl - 1)) == 0, 0.0, xm[:SUBLANES])
    xm = jnp.concatenate([first, xm[SUBLANES:]], axis=0)
    xp = x_ref[r0 + 1:r0 + 1 + BLOCK, :] if r0 + BLOCK < total else pltpu.roll(cur, BLOCK - 1, 0)
    last = jnp.where(((r0 + BLOCK - SUBLANES + sub) & (sl - 1)) == sl - 1, 0.0, xp[BLOCK - SUBLANES:])
    xp = jnp.concatenate([xp[:BLOCK - SUBLANES], last], axis=0)
    return xm * w_ref[0:1, :] + cur * w_ref[1:2, :] + xp * w_ref[2:3, :] + b_ref[...]


def _hy_dwconv_kernel(x_ref, w_ref, b_ref, oe_ref, oo_ref, *, cfg):
    sl = _block_seq_len(cfg)
    half = x_ref.shape[0] // 2
    xe = x_ref[pl.ds(0, half, stride=2), :]
    xo = x_ref[pl.ds(1, half, stride=2), :]
    m = lax.broadcasted_iota(jnp.int32, xe.shape, 0)
    before_even = jnp.where(((2 * m) & (sl - 1)) == 0, 0.0, pltpu.roll(xo, 1, 0))
    after_odd = jnp.where(((2 * m + 1) & (sl - 1)) == sl - 1, 0.0, pltpu.roll(xe, half - 1, 0))
    w0, w1, w2 = w_ref[0:1, :], w_ref[1:2, :], w_ref[2:3, :]
    oe_ref[...] = before_even * w0 + xe * w1 + xo * w2 + b_ref[...]
    oo_ref[...] = xe * w0 + xo * w1 + after_odd * w2 + b_ref[...]


def _hy_dwconv(cfg, z, w, b, l):
    tn = LANES
    blk = _nbytes((cfg.rb, tn), F32)
    half = pl.BlockSpec((cfg.rb // 2, tn), lambda r, j: (r, j))
    return pl.pallas_call(
        functools.partial(_hy_dwconv_kernel, cfg=cfg),
        grid=(cfg.r // cfg.rb, HY_COLS // tn),
        in_specs=[pl.BlockSpec((cfg.rb, tn), lambda r, j: (r, j)),
                  pl.BlockSpec((None, 3, tn), lambda r, j: (l, 0, j)),
                  pl.BlockSpec((None, 1, tn), lambda r, j: (l, 0, j))],
        out_specs=[half, half],
        out_shape=[jax.ShapeDtypeStruct((cfg.r // 2, HY_COLS), F32)] * 2,
        compiler_params=_cparams(2, 2 * blk, extra=4 * blk),
        name="hy_dwconv",
    )(z, w, b.reshape(DEPTH, 1, HY_COLS))


def _ffn_act_kernel(a_ref, b_ref, wa_ref, wb_ref, ba_ref, bb_ref, o_ref, *, cfg):
    sl = _block_seq_len(cfg)
    for r0 in range(0, a_ref.shape[0], BLOCK):
        a = _dwconv3_rows(a_ref, wa_ref, ba_ref, r0, sl)
        b = _dwconv3_rows(b_ref, wb_ref, bb_ref, r0, sl)
        o_ref[r0:r0 + BLOCK, :] = (_gelu(a) * b).astype(BF16)


def _ffn_act(cfg, u, w, b, l):
    tn = 256
    nb = D_FF // tn
    blk = _nbytes((cfg.rb, tn), F32)
    b3 = b.reshape(DEPTH, 1, 2 * D_FF)
    return pl.pallas_call(
        functools.partial(_ffn_act_kernel, cfg=cfg),
        grid=(cfg.r // cfg.rb, nb),
        in_specs=[pl.BlockSpec((cfg.rb, tn), lambda r, j: (r, j)),
                  pl.BlockSpec((cfg.rb, tn), lambda r, j: (r, j + nb)),
                  pl.BlockSpec((None, 3, tn), lambda r, j: (l, 0, j)),
                  pl.BlockSpec((None, 3, tn), lambda r, j: (l, 0, j + nb)),
                  pl.BlockSpec((None, 1, tn), lambda r, j: (l, 0, j)),
                  pl.BlockSpec((None, 1, tn), lambda r, j: (l, 0, j + nb))],
        out_specs=pl.BlockSpec((cfg.rb, tn), lambda r, j: (r, j)),
        out_shape=jax.ShapeDtypeStruct((cfg.r, D_FF), BF16),
        compiler_params=_cparams(2, 3 * blk, extra=6 * blk),
        name="ffn_act",
    )(u, u, w, w, b3, b3)


def _dft_tables_kernel(ac_ref, as_ref, bc_ref, bs_ref, c_ref, s_ref):
    nb = bc_ref.shape[1]
    for a in range(ac_ref.shape[1]):
        ca, sa = ac_ref[:, a:a + 1], as_ref[:, a:a + 1]
        c_ref[:, a * nb:(a + 1) * nb] = (ca * bc_ref[...] - sa * bs_ref[...]).astype(BF16)
        s_ref[:, a * nb:(a + 1) * nb] = (sa * bc_ref[...] + ca * bs_ref[...]).astype(BF16)


def _dft_tables(length):
    half = length // 2
    period = 4 * length
    r = min(LANES, half)
    tk = min(256, half)
    idx = np.arange(half, dtype=np.int64)
    hi = np.arange(half // r, dtype=np.int64) * r
    lo = np.arange(r, dtype=np.int64)

    def cs(m):
        ang = 2.0 * np.pi * (m % period).astype(np.float64) / period
        return jnp.asarray(np.cos(ang), F32), jnp.asarray(np.sin(ang), F32)

    def expand(phase_hi, phase_lo):
        spec_a = pl.BlockSpec((tk, half // r), lambda i: (i, 0))
        spec_b = pl.BlockSpec((tk, r), lambda i: (i, 0))
        spec_o = pl.BlockSpec((tk, half), lambda i: (i, 0))
        return pl.pallas_call(
            _dft_tables_kernel,
            grid=(half // tk,),
            in_specs=[spec_a, spec_a, spec_b, spec_b],
            out_specs=[spec_o, spec_o],
            out_shape=[jax.ShapeDtypeStruct((half, half), BF16)] * 2,
            compiler_params=_cparams(1, 4 * _nbytes((tk, LANES), F32), 2 * _nbytes((tk, half), BF16)),
            name="dft_tables",
        )(*cs(phase_hi), *cs(phase_lo))

    odd = (2 * idx + 1)[:, None]
    row = idx[:, None]
    out = {}
    out["ce"], out["se"] = expand(odd * (2 * hi)[None, :], odd * (2 * lo)[None, :])
    out["co"], out["so"] = expand(odd * (2 * hi)[None, :], odd * (2 * lo + 1)[None, :])
    out["cet"], out["set"] = expand(2 * row * (2 * hi)[None, :], 2 * row * (2 * lo + 1)[None, :])
    out["cot"], out["sot"] = expand((2 * row + 1) * (2 * hi)[None, :], (2 * row + 1) * (2 * lo + 1)[None, :])
    return out


def _hy_feats(length):
    pos = jnp.arange(length, dtype=F32)
    t = (pos / length)[:, None]
    bands = jnp.linspace(1e-4, HY_BANDS - 1, HY_BANDS, dtype=F32)
    wpos = 2.0 * math.pi * t * bands
    feats = jnp.concatenate([t, jnp.cos(wpos), -jnp.sin(wpos)], axis=-1)
    feats = jnp.concatenate([feats[0::2], feats[1::2]], axis=0)
    return jnp.pad(feats, ((0, 0), (0, LANES - HY_FEAT)))


def _hy_filter_kernel(f_ref, w1_ref, b1_ref, w2_ref, b2_ref, w3_ref, fr_ref, dec_ref, hs_ref, hd_ref, hb0_ref):
    feats = f_ref[...]
    fr = fr_ref[...]
    h = jnp.sin(fr * (_dot3(feats, w1_ref[...]) + b1_ref[...]))
    h = jnp.sin(fr * (_dot3(h, w2_ref[...]) + b2_ref[...]))
    h = _dot3(h, w3_ref[...]) * jnp.exp(-feats[:, 0:1] * jnp.abs(dec_ref[...]))
    half = 2 * HY_WIDTH
    hf, hb = h[:, :half], h[:, half:]
    hs_ref[...] = (hf + hb).astype(BF16)
    hd_ref[...] = (hb - hf).astype(BF16)

    @pl.when(pl.program_id(0) == 0)
    def _():
        hb0_ref[...] = jnp.broadcast_to(hb[0:1, :], hb0_ref.shape)


def _hy_filter(length, feats, w1p, b1, w2, b2, w3, freq, decay, l):
    tl = 256
    half = 2 * HY_WIDTH
    lsel = lambda i: (l, 0, 0)
    return pl.pallas_call(
        _hy_filter_kernel,
        grid=(length // tl,),
        in_specs=[pl.BlockSpec((tl, LANES), lambda i: (i, 0)),
                  pl.BlockSpec((None, LANES, HY_HID), lsel),
                  pl.BlockSpec((None, 1, HY_HID), lsel),
                  pl.BlockSpec((None, HY_HID, HY_HID), lsel),
                  pl.BlockSpec((None, 1, HY_HID), lsel),
                  pl.BlockSpec((None, HY_HID, 2 * half), lsel),
                  pl.BlockSpec((None, 1, HY_HID), lsel),
                  pl.BlockSpec((None, 1, 2 * half), lsel)],
        out_specs=[pl.BlockSpec((tl, half), lambda i: (i, 0)),
                   pl.BlockSpec((tl, half), lambda i: (i, 0)),
                   pl.BlockSpec((8, half), lambda i: (0, 0))],
        out_shape=[jax.ShapeDtypeStruct((length, half), BF16),
                   jax.ShapeDtypeStruct((length, half), BF16),
                   jax.ShapeDtypeStruct((8, half), F32)],
        compiler_params=_cparams(1, 2 * _nbytes((tl, half), F32), extra=6 * _nbytes((tl, 2 * half), F32)),
        name="hy_filter",
    )(feats, w1p, b1, w2, b2, w3, freq, decay)


def _hy_spec_kernel(ce, se, co, so, hse, hso, hde, hdo, hb0_ref, kar, kai, kbr, kbi):
    a = _dot(ce[...], hse[...])
    b = _dot(co[...], hso[...])
    c = _dot(se[...], hde[...])
    d = _dot(so[...], hdo[...])
    kar[...] = a + b - hb0_ref[0:1, :]
    kai[...] = c + d
    kbr[...] = a - b - hb0_ref[0:1, :]
    kbi[...] = d - c


def _hy_spec(length, mats, hs, hd, hb0):
    half_len = length // 2
    tk = min(512, half_len)
    tn = 512
    width = 2 * HY_WIDTH
    ncb = width // tn
    fspec = pl.BlockSpec((tk, half_len), lambda j, i: (i, 0))
    espec = pl.BlockSpec((half_len, tn), lambda j, i: (0, j))
    ospec = pl.BlockSpec((half_len, tn), lambda j, i: (1, j))
    kspec = pl.BlockSpec((tk, tn), lambda j, i: (i, j))
    return pl.pallas_call(
        _hy_spec_kernel,
        grid=(ncb, half_len // tk),
        in_specs=[fspec] * 4 + [espec, ospec, espec, ospec, pl.BlockSpec((8, tn), lambda j, i: (0, j))],
        out_specs=[kspec] * 4,
        out_shape=[jax.ShapeDtypeStruct((half_len, width), F32)] * 4,
        compiler_params=_cparams(2, 4 * _nbytes((tk, half_len), BF16), 4 * _nbytes((half_len, tn), BF16),
                                 4 * _nbytes((tk, tn), F32), extra=4 * _nbytes((tk, tn), F32)),
        name="hy_spec",
    )(mats["ce"], mats["se"], mats["co"], mats["so"], hs, hs, hd, hd, hb0)


def _hy_fwd_kernel(ce, se, co, so, ue_ref, uo_ref, kar, kai, kbr, kbi, pr_ref, pi_ref, qr_ref, qi_ref, ub_ref):
    @pl.when(pl.program_id(2) == 0)
    def _():
        ub_ref[0] = ue_ref[...].astype(BF16)
        ub_ref[1] = uo_ref[...].astype(BF16)

    e_re, e_im = _dot(ce[...], ub_ref[0]), -_dot(se[...], ub_ref[0])
    o_re, o_im = _dot(co[...], ub_ref[1]), -_dot(so[...], ub_ref[1])
    ua_re, ua_im = e_re + o_re, e_im + o_im
    ub_re, ub_im = e_re - o_re, o_im - e_im
    ya_re = ua_re * kar[...] - ua_im * kai[...]
    ya_im = ua_re * kai[...] + ua_im * kar[...]
    yb_re = ub_re * kbr[...] - ub_im * kbi[...]
    yb_im = ub_re * kbi[...] + ub_im * kbr[...]
    pr_ref[...] = (ya_re + yb_re).astype(BF16)
    pi_ref[...] = (ya_im - yb_im).astype(BF16)
    qr_ref[...] = (ya_re - yb_re).astype(BF16)
    qi_ref[...] = (ya_im + yb_im).astype(BF16)


def _hy_inv_kernel(cet, set_, cot, sot, pr, pi_, qr, qi, ue_ref, uo_ref, ge_ref, go_ref, bias_ref, ye_ref, yo_ref,
                   *, inv_len):
    even = _dot(cet[...], pr[...]) - _dot(set_[...], pi_[...])
    odd = _dot(cot[...], qr[...]) - _dot(sot[...], qi[...])
    ye_ref[...] = (ge_ref[...] * (even * inv_len + ue_ref[...] * bias_ref[...])).astype(ye_ref.dtype)
    yo_ref[...] = (go_ref[...] * (odd * inv_len + uo_ref[...] * bias_ref[...])).astype(yo_ref.dtype)


def _hy_conv(length, nseq, mats, spec, order, data, gate, bias, l, out_dtype):
    half_len = length // 2
    tn = min(512, HY_WIDTH)
    tk = min(512, half_len)
    ncb = HY_WIDTH // tn
    ntk = half_len // tk
    fspec = pl.BlockSpec((tk, half_len), lambda s, j, i: (i, 0))
    kspec = pl.BlockSpec((tk, tn), lambda s, j, i: (i, order * ncb + j))
    tile = pl.BlockSpec((tk, tn), lambda s, j, i: (s * ntk + i, j))

    def full(loc):
        _, rb0, cb0 = loc
        return pl.BlockSpec((half_len, tn), lambda s, j, i: (rb0 + s, cb0 + j))

    def rows(loc):
        _, rb0, cb0 = loc
        return pl.BlockSpec((tk, tn), lambda s, j, i: ((rb0 + s) * ntk + i, cb0 + j))

    pq = pl.pallas_call(
        _hy_fwd_kernel,
        grid=(nseq, ncb, ntk),
        in_specs=[fspec] * 4 + [full(data[0]), full(data[1])] + [kspec] * 4,
        out_specs=[tile] * 4,
        out_shape=[jax.ShapeDtypeStruct((nseq * half_len, HY_WIDTH), BF16)] * 4,
        scratch_shapes=[pltpu.VMEM((2, half_len, tn), BF16)],
        compiler_params=_cparams(3, 4 * _nbytes((tk, half_len), BF16), 2 * _nbytes((half_len, tn), F32),
                                 4 * _nbytes((tk, tn), F32), 4 * _nbytes((tk, tn), BF16),
                                 extra=2 * _nbytes((half_len, tn), BF16) + 12 * _nbytes((tk, tn), F32)),
        name="hy_fwd",
    )(mats["ce"], mats["se"], mats["co"], mats["so"], data[0][0], data[1][0], *spec)

    resident = pl.BlockSpec((half_len, tn), lambda s, j, i: (s, j))
    return pl.pallas_call(
        functools.partial(_hy_inv_kernel, inv_len=1.0 / length),
        grid=(nseq, ncb, ntk),
        in_specs=[fspec] * 4 + [resident] * 4 + [rows(data[0]), rows(data[1]), rows(gate[0]), rows(gate[1]),
                                                 pl.BlockSpec((None, None, 1, tn), lambda s, j, i: (l, order, 0, j))],
        out_specs=[tile, tile],
        out_shape=[jax.ShapeDtypeStruct((nseq * half_len, HY_WIDTH), out_dtype)] * 2,
        compiler_params=_cparams(3, 4 * _nbytes((tk, half_len), BF16), 4 * _nbytes((half_len, tn), BF16),
                                 6 * _nbytes((tk, tn), F32), extra=6 * _nbytes((tk, tn), F32)),
        name="hy_inv",
    )(mats["cet"], mats["set"], mats["cot"], mats["sot"], *pq, data[0][0], data[1][0], gate[0][0], gate[1][0], bias)


def _hyena(cfg, consts, z, p, l):
    zce, zco = _hy_dwconv(cfg, z, p["hy_conv_w"], p["hy_conv_b"], l)
    bias = p["hy_bias"].reshape(DEPTH, 2, 1, HY_WIDTH)
    w1p = jnp.pad(p["hy_w1"], ((0, 0), (0, LANES - HY_FEAT), (0, 0)))
    tn = min(512, HY_WIDTH)
    outs = []
    for length, nseq, row0 in ((cfg.lc, cfg.nc, 0), (cfg.ll, cfg.nl, cfg.rc)):
        mats = consts["dft"][length]
        hs, hd, hb0 = _hy_filter(length, consts["feats"][length], w1p, p["hy_b1"].reshape(DEPTH, 1, HY_HID),
                                 p["hy_w2"], p["hy_b2"].reshape(DEPTH, 1, HY_HID), p["hy_w3"],
                                 p["hy_freq"].reshape(DEPTH, 1, HY_HID),
                                 p["hy_decay"].reshape(DEPTH, 1, 4 * HY_WIDTH), l)
        spec = _hy_spec(length, mats, hs, hd, hb0)
        rb0 = row0 // length

        def zc_cols(col0):
            return (zce, rb0, col0 // tn), (zco, rb0, col0 // tn)

        y1 = _hy_conv(length, nseq, mats, spec, 0, zc_cols(0), zc_cols(HY_WIDTH), bias, l, F32)
        y2 = _hy_conv(length, nseq, mats, spec, 1, ((y1[0], 0, 0), (y1[1], 0, 0)), zc_cols(2 * HY_WIDTH), bias, l,
                      BF16)
        outs.append(jnp.stack(y2, axis=1).reshape(nseq * length, HY_WIDTH))
    return jnp.concatenate(outs, axis=0)


def _softmax_sink_pv(parts, sink):
    m = sink
    for s, _ in parts:
        m = jnp.maximum(m, jnp.max(s, axis=-1, keepdims=True))
    den = jnp.exp(sink - m)
    acc = None
    for s, v in parts:
        e = jnp.exp(s - m)
        den = den + jnp.sum(e, axis=-1, keepdims=True)
        pv = _dot(e.astype(BF16), v)
        acc = pv if acc is None else acc + pv
    return acc / den


def _group_sink(sink_ref, h):
    return jnp.concatenate([jnp.broadcast_to(sink_ref[hq:hq + 1, 0:1], (BLOCK, 1))
                            for hq in range(h * GQA_GROUP, (h + 1) * GQA_GROUP)], axis=0)


def _store_group(o_ref, h, o):
    for g in range(GQA_GROUP):
        hq = h * GQA_GROUP + g
        o_ref[:, hq * HEAD_DIM:(hq + 1) * HEAD_DIM] = o[g * BLOCK:(g + 1) * BLOCK].astype(BF16)


def _attn_ctx_kernel(q_ref, k_ref, v_ref, sink_ref, o_ref):
    scale = HEAD_DIM ** -0.5
    for h in range(N_KV_HEADS):
        hs = slice(h * HEAD_DIM, (h + 1) * HEAD_DIM)
        k = k_ref[:, hs].astype(BF16)
        v = v_ref[:, hs].astype(BF16)
        q = jnp.concatenate([q_ref[:, hq * HEAD_DIM:(hq + 1) * HEAD_DIM]
                             for hq in range(h * GQA_GROUP, (h + 1) * GQA_GROUP)], axis=0).astype(BF16)
        s = _dot_nt(q, k) * scale
        _store_group(o_ref, h, _softmax_sink_pv([(s, v)], _group_sink(sink_ref, h)))


def _attn_ctx(cfg, z, sink):
    nb = cfg.lc // BLOCK
    qw = N_Q_HEADS * HEAD_DIM
    return pl.pallas_call(
        _attn_ctx_kernel,
        grid=(cfg.nc, nb),
        in_specs=[pl.BlockSpec((BLOCK, qw), lambda s, i: (s * nb + i, Q_COL0 // qw)),
                  pl.BlockSpec((cfg.lc, KV_COLS), lambda s, i: (s, K_COL0 // KV_COLS)),
                  pl.BlockSpec((cfg.lc, KV_COLS), lambda s, i: (s, V_COL0 // KV_COLS)),
                  pl.BlockSpec((N_Q_HEADS, LANES), lambda s, i: (0, 0))],
        out_specs=pl.BlockSpec((BLOCK, qw), lambda s, i: (s * nb + i, 0)),
        out_shape=jax.ShapeDtypeStruct((cfg.rc, qw), BF16),
        compiler_params=_cparams(2, _nbytes((BLOCK, qw), F32), 2 * _nbytes((cfg.lc, KV_COLS), F32)),
        name="attn_ctx",
    )(z, z, z, sink)


def _rope(x, c, s):
    return x * c + pltpu.roll(x, HEAD_DIM // 2, 1) * s


def _attn_lat_kernel(q_ref, kp_ref, kc_ref, kn_ref, vp_ref, vc_ref, vn_ref, ck_ref, cv_ref,
                     cq, sq, cp, sp, cn, sn, sink_ref, o_ref, *, nb):
    i = pl.program_id(1)
    scale = HEAD_DIM ** -0.5
    shape = (GQA_GROUP * BLOCK, 3 * BLOCK)
    qi = lax.broadcasted_iota(jnp.int32, shape, 0) & (BLOCK - 1)
    kj = lax.broadcasted_iota(jnp.int32, shape, 1)
    ok = jnp.abs(kj - BLOCK - qi) <= WINDOW
    ok = ok & ((kj >= BLOCK) | (i > 0)) & ((kj < 2 * BLOCK) | (i < nb - 1))
    for h in range(N_KV_HEADS):
        hs = slice(h * HEAD_DIM, (h + 1) * HEAD_DIM)
        kw = jnp.concatenate([_rope(kp_ref[:, hs], cp[...], sp[...]),
                              _rope(kc_ref[:, hs], cq[...], sq[...]),
                              _rope(kn_ref[:, hs], cn[...], sn[...])], axis=0).astype(BF16)
        vw = jnp.concatenate([vp_ref[:, hs], vc_ref[:, hs], vn_ref[:, hs]], axis=0).astype(BF16)
        kctx = ck_ref[:, hs].astype(BF16)
        vctx = cv_ref[:, hs].astype(BF16)
        q = jnp.concatenate([_rope(q_ref[:, hq * HEAD_DIM:(hq + 1) * HEAD_DIM], cq[...], sq[...])
                             for hq in range(h * GQA_GROUP, (h + 1) * GQA_GROUP)], axis=0).astype(BF16)
        s1 = _dot_nt(q, kctx) * scale
        s2 = jnp.where(ok, _dot_nt(q, kw) * scale, NEG_INF)
        _store_group(o_ref, h, _softmax_sink_pv([(s1, vctx), (s2, vw)], _group_sink(sink_ref, h)))


def _attn_lat(cfg, consts, z, ck, cv, sink):
    nb = cfg.ll // BLOCK
    rb0 = cfg.rc // BLOCK
    qw = N_Q_HEADS * HEAD_DIM
    past = ck.shape[1]
    cos2, sin2 = consts["rope"]

    def blk(delta):
        return lambda b, i: rb0 + b * nb + jnp.clip(i + delta, 0, nb - 1)

    def zspec(width, col0, delta):
        rowf = blk(delta)
        return pl.BlockSpec((BLOCK, width), lambda b, i: (rowf(b, i), col0 // width))

    def tspec(delta):
        return pl.BlockSpec((BLOCK, HEAD_DIM), lambda b, i: (jnp.clip(i + delta, 0, nb - 1), 0))

    cspec = pl.BlockSpec((None, past, KV_COLS), lambda b, i: (b, 0, 0))
    return pl.pallas_call(
        functools.partial(_attn_lat_kernel, nb=nb),
        grid=(cfg.nl, nb),
        in_specs=[zspec(qw, Q_COL0, 0),
                  zspec(KV_COLS, K_COL0, -1), zspec(KV_COLS, K_COL0, 0), zspec(KV_COLS, K_COL0, 1),
                  zspec(KV_COLS, V_COL0, -1), zspec(KV_COLS, V_COL0, 0), zspec(KV_COLS, V_COL0, 1),
                  cspec, cspec,
                  tspec(0), tspec(0), tspec(-1), tspec(-1), tspec(1), tspec(1),
                  pl.BlockSpec((N_Q_HEADS, LANES), lambda b, i: (0, 0))],
        out_specs=pl.BlockSpec((BLOCK, qw), lambda b, i: (b * nb + i, 0)),
        out_shape=jax.ShapeDtypeStruct((cfg.rl, qw), BF16),
        compiler_params=_cparams(2, _nbytes((BLOCK, qw), F32), 6 * _nbytes((BLOCK, KV_COLS), F32),
                                 2 * _nbytes((past, KV_COLS), F32)),
        name="attn_lat",
    )(z, z, z, z, z, z, z, ck, cv, cos2, sin2, cos2, sin2, cos2, sin2, sink)


def _rope_tables(length):
    rows = length // GRID_W
    row = jnp.repeat(jnp.arange(rows, dtype=F32), GRID_W)
    col = jnp.tile(jnp.arange(GRID_W, dtype=F32), rows)
    n_freq = HEAD_DIM // 4
    inv = ROPE_BASE ** (-jnp.arange(n_freq, dtype=F32) / n_freq)
    ang = jnp.concatenate([row[:, None] * inv, col[:, None] * inv], axis=-1)
    cos, sin = jnp.cos(ang), jnp.sin(ang)
    return jnp.concatenate([cos, cos], axis=-1), jnp.concatenate([-sin, sin], axis=-1)


def _s5_operands(p):
    t_len = S5_CHUNK
    lr, li = p["s5_lam_re"].astype(F32), p["s5_lam_im"].astype(F32)
    n_layers = lr.shape[0]
    dt = jnp.exp(p["s5_log_dt"].astype(F32))[..., None]
    zr, zi = lr * dt, li * dt
    mag = jnp.exp(zr)
    ar, ai = mag * jnp.cos(zi), mag * jnp.sin(zi)
    den = lr * lr + li * li
    cr = ((ar - 1.0) * lr + ai * li) / den
    ci = (ai * lr - (ar - 1.0) * li) / den
    k = jnp.arange(t_len + 1, dtype=F32)[:, None, None, None, None]
    pm = jnp.exp(k * zr)
    pr, pi = pm * jnp.cos(k * zi), pm * jnp.sin(k * zi)

    prs, pis = pr.transpose(1, 2, 0, 3, 4), pi.transpose(1, 2, 0, 3, 4)

    def per_direction(x, fwd, bwd):
        return jnp.concatenate([fwd(x[:, 0:1]), bwd(x[:, 1:2])], axis=1)

    bt_re = p["s5_b_re"].astype(F32).transpose(0, 1, 2, 4, 3)
    bt_im = p["s5_b_im"].astype(F32).transpose(0, 1, 2, 4, 3)
    bbr = (cr[:, :, :, None, :] * bt_re - ci[:, :, :, None, :] * bt_im)[:, :, None]
    bbi = (cr[:, :, :, None, :] * bt_im + ci[:, :, :, None, :] * bt_re)[:, :, None]
    kp = [per_direction(x, lambda y: jnp.flip(y[:, :, :t_len], axis=2), lambda y: y[:, :, :t_len])
          [:, :, :, :, None, :] for x in (prs, pis)]
    pf = jnp.concatenate([kp[0] * bbr - kp[1] * bbi, kp[0] * bbi + kp[1] * bbr], axis=-1)
    pf = pf.reshape(n_layers, 2, t_len, D_BRANCH, 2 * S5_STATE)

    c_re, c_im = p["s5_c_re"].astype(F32), p["s5_c_im"].astype(F32)
    cmat = jnp.concatenate([c_re, -c_im], axis=-1).reshape(n_layers, 2, D_BRANCH, 2 * S5_STATE)
    ct_re = c_re.reshape(n_layers, 2, D_BRANCH, S5_STATE).transpose(0, 1, 3, 2)[:, :, None]
    ct_im = c_im.reshape(n_layers, 2, D_BRANCH, S5_STATE).transpose(0, 1, 3, 2)[:, :, None]
    to_lanes = jnp.asarray(np.repeat(np.eye(S5_GROUPS, dtype=np.float32), S5_CH, axis=1))
    kq = [jnp.dot(per_direction(x, lambda y: y[:, :, 1:], lambda y: jnp.flip(y[:, :, 1:], axis=2))
                  .transpose(0, 1, 2, 4, 3).reshape(-1, S5_GROUPS), to_lanes, precision=lax.Precision.HIGHEST)
          .reshape(n_layers, 2, t_len, S5_STATE, D_BRANCH) for x in (prs, pis)]
    qf = jnp.concatenate([ct_re * kq[0] - ct_im * kq[1], -(ct_re * kq[1] + ct_im * kq[0])], axis=3)

    a_chunk = jnp.stack([jnp.concatenate([pr[t_len], pr[t_len]], axis=-1),
                         jnp.concatenate([-pi[t_len], pi[t_len]], axis=-1)], axis=2)
    return pf, qf, cmat, a_chunk


def _s5_ucat(z_ref):
    return jnp.concatenate([z_ref[:, t, :].astype(BF16) for t in range(S5_CHUNK)], axis=1)


def _octet_group(shape, axis):
    return lax.broadcasted_iota(jnp.int32, shape, axis) // S5_CH


def _s5_in_kernel(z_ref, pf_ref, x_ref, p_scr):
    gpo = S5_GROUPS // S5_OCTETS

    @pl.when(pl.program_id(1) == 0)
    def _():
        row_group = _octet_group((LANES, LANES), 0)
        for d in range(2):
            for t in range(S5_CHUNK):
                blk = pf_ref[d, t]
                for g in range(gpo):
                    p_scr[d, t * LANES:(t + 1) * LANES, g * LANES:(g + 1) * LANES] = (
                        jnp.where(row_group == g, blk, 0.0).astype(BF16))

    u = _s5_ucat(z_ref)
    for d in range(2):
        x_ref[d] = _dot(u, p_scr[d]).reshape(u.shape[0], gpo, 2 * S5_STATE)


def _s5_in(z3, pf, l, tr):
    r16 = z3.shape[0]
    gpo = S5_GROUPS // S5_OCTETS
    w = 2 * S5_STATE
    return pl.pallas_call(
        _s5_in_kernel,
        grid=(S5_OCTETS, r16 // tr),
        in_specs=[pl.BlockSpec((tr, S5_CHUNK, LANES), lambda o, i: (i, 0, S5_COL0 // LANES + o)),
                  pl.BlockSpec((None, 2, S5_CHUNK, LANES, w), lambda o, i: (l, 0, 0, o, 0))],
        out_specs=pl.BlockSpec((2, tr, gpo, w), lambda o, i: (0, i, o, 0)),
        out_shape=jax.ShapeDtypeStruct((2, r16, S5_GROUPS, w), F32),
        scratch_shapes=[pltpu.VMEM((2, S5_OCT_K, gpo * w), BF16)],
        compiler_params=_cparams(2, _nbytes((tr, S5_CHUNK, LANES), F32), _nbytes((2, S5_CHUNK, LANES, w), F32),
                                 _nbytes((2, tr, gpo, w), F32),
                                 extra=_nbytes((2, S5_OCT_K, gpo * w), BF16) + 4 * _nbytes((tr, S5_OCT_K), F32)),
        name="s5_in",
    )(z3, pf)


def _s5_scan_kernel(*refs, n_in, nsb, jb):
    a_ref, h0_ref = refs[0], refs[1]
    xf = refs[2:2 + n_in]
    xb = refs[2 + n_in:2 + 2 * n_in]
    sf = refs[2 + 2 * n_in:2 + 3 * n_in]
    sb = refs[2 + 3 * n_in:2 + 4 * n_in]
    fin_ref, st_ref = refs[2 + 4 * n_in], refs[3 + 4 * n_in]
    t = pl.program_id(0)

    @pl.when(t == 0)
    def _():
        st_ref[...] = h0_ref[...]

    half = S5_STATE

    def body(s, carry):
        jf = s
        jr = jb - 1 - s
        for r in range(n_in):
            for q in range(nsb):
                idx = r * nsb + q
                cur = st_ref[0, idx]
                sf[r][q * jb + jf] = cur
                st_ref[0, idx] = cur * a_ref[0, 0] + pltpu.roll(cur, half, 1) * a_ref[0, 1] + xf[r][q * jb + jf]
                cur = st_ref[1, idx]
                sb[r][q * jb + jr] = cur
                st_ref[1, idx] = cur * a_ref[1, 0] + pltpu.roll(cur, half, 1) * a_ref[1, 1] + xb[r][q * jb + jr]
        return carry

    lax.fori_loop(0, jb, body, 0)

    @pl.when(t == pl.num_programs(0) - 1)
    def _():
        fin_ref[...] = st_ref[...]


def _s5_scan(x, a_chunk, h0, l, row0, n_in, nsb, nj, jb):
    g = x.shape[2]
    w = x.shape[3]
    nblk = nj // jb
    nseq = n_in * nsb
    rows = nsb * jb
    base = row0 // rows

    def xspec(d, r):
        if d == 0:
            return pl.BlockSpec((None, rows, g, w), lambda t: (0, base + r * nblk + t, 0, 0))
        return pl.BlockSpec((None, rows, g, w), lambda t: (1, base + r * nblk + nblk - 1 - t, 0, 0))

    def sspec(d):
        if d == 0:
            return pl.BlockSpec((rows, g, w), lambda t: (t, 0, 0))
        return pl.BlockSpec((rows, g, w), lambda t: (nblk - 1 - t, 0, 0))

    st_spec = pl.BlockSpec((2, nseq, g, w), lambda t: (0, 0, 0, 0))
    res = pl.pallas_call(
        functools.partial(_s5_scan_kernel, n_in=n_in, nsb=nsb, jb=jb),
        grid=(nblk,),
        in_specs=[pl.BlockSpec((None, 2, 2, g, w), lambda t: (l, 0, 0, 0, 0)), st_spec]
                 + [xspec(0, r) for r in range(n_in)] + [xspec(1, r) for r in range(n_in)],
        out_specs=[sspec(0)] * n_in + [sspec(1)] * n_in + [st_spec],
        out_shape=[jax.ShapeDtypeStruct((nsb * nj, g, w), F32)] * (2 * n_in)
                  + [jax.ShapeDtypeStruct((2, nseq, g, w), F32)],
        scratch_shapes=[pltpu.VMEM((2, nseq, g, w), F32)],
        compiler_params=_cparams(1, 4 * n_in * _nbytes((rows, g, w), F32), 3 * _nbytes((2, nseq, g, w), F32)),
        name="s5_scan",
    )(a_chunk, h0, *([x] * (2 * n_in)))
    return res[:n_in], res[n_in:2 * n_in], res[2 * n_in]


def _s5_out_kernel(z_ref, pf_ref, cm_ref, qf_ref, d_ref, *rest, starts):
    n = len(starts)
    sf_refs, sb_refs = rest[:n], rest[n:2 * n]
    y_ref, m_scr, q_scr = rest[2 * n:]
    gpo = S5_GROUPS // S5_OCTETS

    @pl.when(pl.program_id(1) == 0)
    def _():
        same_group = _octet_group((LANES, LANES), 0) == _octet_group((LANES, LANES), 1)

        def lag_op(d, k):
            ab = pf_ref[d, S5_CHUNK - 1 - k] if d == 0 else pf_ref[d, k]
            ah, al = _split(ab)
            ch, cl = _split(cm_ref[d])
            return jnp.where(same_group, _dot_nt(ah, ch) + (_dot_nt(ah, cl) + _dot_nt(al, ch)), 0.0)

        fwd = [lag_op(0, k) for k in range(S5_CHUNK)]
        bwd = [lag_op(1, k) for k in range(S5_CHUNK)]
        blocks = {0: (fwd[0] + bwd[0]).astype(BF16)}
        for k in range(1, S5_CHUNK):
            blocks[k] = fwd[k].astype(BF16)
            blocks[-k] = bwd[k].astype(BF16)
        for t in range(S5_CHUNK):
            for t2 in range(S5_CHUNK):
                m_scr[t * LANES:(t + 1) * LANES, t2 * LANES:(t2 + 1) * LANES] = blocks[t2 - t]
        lane_group = _octet_group((LANES, LANES), 1)
        for d in range(2):
            for t in range(S5_CHUNK):
                blk = qf_ref[d, t]
                for g in range(gpo):
                    q_scr[d, g * LANES:(g + 1) * LANES, t * LANES:(t + 1) * LANES] = (
                        jnp.where(lane_group == g, blk, 0.0).astype(BF16))

    u = _s5_ucat(z_ref)
    rows = u.shape[0]
    width = gpo * 2 * S5_STATE

    def entering_state(refs):
        s = refs[0][...]
        for ref, start in zip(refs[1:], starts[1:]):
            s = jnp.where(pl.program_id(1) >= start, ref[...], s)
        return s.reshape(rows, width).astype(BF16)

    y = _dot(u, m_scr[...])
    y += _dot(entering_state(sf_refs), q_scr[0])
    y += _dot(entering_state(sb_refs), q_scr[1])
    for t in range(S5_CHUNK):
        y_ref[:, t, :] = y[:, t * LANES:(t + 1) * LANES] + z_ref[:, t, :] * d_ref[...]


def _s5_out(z3, pf, cmat, qf, d_oct, sin_f, sin_b, l, tr):
    r16 = z3.shape[0]
    gpo = S5_GROUPS // S5_OCTETS
    w = 2 * S5_STATE
    tiles = [s.shape[0] // tr for s in sin_f]
    starts = tuple(sum(tiles[:n]) for n in range(len(tiles)))
    sspecs = [pl.BlockSpec((tr, gpo, w), functools.partial(
        lambda o, i, start, count: (jnp.clip(i - start, 0, count - 1), o, 0), start=start, count=count))
        for start, count in zip(starts, tiles)]
    return pl.pallas_call(
        functools.partial(_s5_out_kernel, starts=starts),
        grid=(S5_OCTETS, r16 // tr),
        in_specs=[pl.BlockSpec((tr, S5_CHUNK, LANES), lambda o, i: (i, 0, S5_COL0 // LANES + o)),
                  pl.BlockSpec((None, 2, S5_CHUNK, LANES, w), lambda o, i: (l, 0, 0, o, 0)),
                  pl.BlockSpec((None, 2, LANES, w), lambda o, i: (l, 0, o, 0)),
                  pl.BlockSpec((None, 2, S5_CHUNK, w, LANES), lambda o, i: (l, 0, 0, 0, o)),
                  pl.BlockSpec((None, None, 1, LANES), lambda o, i: (l, o, 0, 0)),
                  *sspecs, *sspecs],
        out_specs=pl.BlockSpec((tr, S5_CHUNK, LANES), lambda o, i: (i, 0, o)),
        out_shape=jax.ShapeDtypeStruct((r16, S5_CHUNK, D_BRANCH), F32),
        scratch_shapes=[pltpu.VMEM((S5_OCT_K, S5_OCT_K), BF16), pltpu.VMEM((2, gpo * w, S5_OCT_K), BF16)],
        compiler_params=_cparams(2, 2 * _nbytes((tr, S5_CHUNK, LANES), F32), 2 * _nbytes((2, S5_CHUNK, LANES, w), F32),
                                 2 * len(tiles) * _nbytes((tr, gpo, w), F32),
                                 extra=_nbytes((S5_OCT_K, S5_OCT_K), BF16) + _nbytes((2, gpo * w, S5_OCT_K), BF16)
                                 + 4 * _nbytes((tr, S5_OCT_K), F32)),
        name="s5_out",
    )(z3, pf, cmat, qf, d_oct, *sin_f, *sin_b)


def _s5_glu_kernel(y_ref, w_ref, b_ref, o_ref, wb_ref):
    @pl.when(pl.program_id(0) == 0)
    def _():
        wb_ref[...] = w_ref[...].astype(BF16)

    y = _gelu(y_ref[...])
    o_ref[...] = (y * _sigmoid(_dot(y.astype(BF16), wb_ref[...]) + b_ref[...])).astype(BF16)


def _s5_glu(cfg, y, w, b, l):
    tm = min(512, cfg.ll)
    return pl.pallas_call(
        _s5_glu_kernel,
        grid=(cfg.r // tm,),
        in_specs=[pl.BlockSpec((tm, D_BRANCH), lambda i: (i, 0)),
                  pl.BlockSpec((None, D_BRANCH, D_BRANCH), lambda i: (l, 0, 0)),
                  pl.BlockSpec((None, 1, D_BRANCH), lambda i: (l, 0, 0))],
        out_specs=pl.BlockSpec((tm, D_BRANCH), lambda i: (i, 0)),
        out_shape=jax.ShapeDtypeStruct((cfg.r, D_BRANCH), BF16),
        scratch_shapes=[pltpu.VMEM((D_BRANCH, D_BRANCH), BF16)],
        compiler_params=_cparams(1, 2 * _nbytes((tm, D_BRANCH), F32), _nbytes((D_BRANCH, D_BRANCH), F32),
                                 extra=_nbytes((D_BRANCH, D_BRANCH), BF16) + 4 * _nbytes((tm, D_BRANCH), F32)),
        name="s5_glu",
    )(y, w, b.reshape(DEPTH, 1, D_BRANCH))


def _s5(cfg, s5m, z, p, h0_lat, l):
    pf, qf, cmat, a_chunk = s5m
    r16 = cfg.r // S5_CHUNK
    tr = min(256, cfg.ll // S5_CHUNK)
    z3 = z.reshape(r16, S5_CHUNK, N_IN)
    x = _s5_in(z3, pf, l, tr)
    njc, njl = cfg.lc // S5_CHUNK, cfg.ll // S5_CHUNK
    rc16 = cfg.rc // S5_CHUNK
    zero_h0 = jnp.zeros((2, cfg.nc, S5_GROUPS, 2 * S5_STATE), F32)
    sf_c, sb_c, fin_ctx = _s5_scan(x, a_chunk, zero_h0, l, 0, 1, cfg.nc, njc, njc)
    sf_l, sb_l, _ = _s5_scan(x, a_chunk, h0_lat, l, rc16, cfg.nl, 1, njl, min(32, njl))
    sin_f = list(sf_c) + list(sf_l)
    sin_b = list(sb_c) + list(sb_l)
    d_oct = p["s5_d"].reshape(DEPTH, S5_OCTETS, 1, LANES)
    y = _s5_out(z3, pf, cmat, qf, d_oct, sin_f, sin_b, l, tr).reshape(cfg.r, D_BRANCH)
    return _s5_glu(cfg, y, p["s5_glu_w"], p["s5_glu_b"], l), fin_ctx


def _forward(cfg, x_prompt, x_sample, c, cache_k, cache_v, state_ssm_re, state_ssm_im, c_ctx, p):
    d = D_MODEL
    x = (x_prompt.reshape(cfg.rc, d), x_sample.reshape(cfg.rl, d))
    cvec = jnp.concatenate([c_ctx[None], c, jnp.zeros((MOD_ROWS - cfg.nseg, d), F32)], axis=0)
    mod = _mod(cvec, p["w_mod"], p["b_mod"])[:, :cfg.nseg].reshape(DEPTH, cfg.nseg, 6, 1, d)
    mod = [[mod[l, :, i] for i in range(6)] for l in range(DEPTH)]
    norm_g = p["norm_g"]

    consts = {
        "dft": {},
        "feats": {n: _hy_feats(n) for n in {cfg.lc, cfg.ll}},
        "rope": _rope_tables(cfg.ll),
    }
    for n in {cfg.lc, cfg.ll}:
        consts["dft"][n] = _dft_tables(n)
    s5m = _s5_operands(p)

    kv_shape = (cfg.nc, cfg.lc, N_KV_HEADS, HEAD_DIM)
    ks, vs, srs, sis = [], [], [], []
    h = _resid_norm(cfg, x, gpre=norm_g[0, 0], sc=mod[0][1], sh=mod[0][0])
    for l in range(DEPTH):
        z = _mm(h, p["w_in"], l, tm=cfg.tm, tn=512)
        ks.append(z[:cfg.rc, K_COL0:K_COL0 + KV_COLS].reshape(kv_shape))
        vs.append(z[:cfg.rc, V_COL0:V_COL0 + KV_COLS].reshape(kv_shape))

        y_hy = _hyena(cfg, consts, z, p, l)
        sink = jnp.broadcast_to(p["attn_sink"][l][:, None], (N_Q_HEADS, LANES))
        past = cache_k.shape[2]
        y_at = jnp.concatenate([
            _attn_ctx(cfg, z, sink),
            _attn_lat(cfg, consts, z, cache_k[:, l].reshape(cfg.nl, past, KV_COLS),
                      cache_v[:, l].reshape(cfg.nl, past, KV_COLS), sink)], axis=0)
        h0 = jnp.concatenate([state_ssm_re[:, l], state_ssm_im[:, l]], axis=-1).transpose(1, 0, 2, 3)
        y_s5, fin = _s5(cfg, s5m, z, p, h0, l)
        srs.append(fin[..., :S5_STATE].transpose(1, 0, 2, 3))
        sis.append(fin[..., S5_STATE:].transpose(1, 0, 2, 3))

        merged = _merge(cfg, z, (y_hy, y_at, y_s5),
                        (p["w_branch_hy"], p["w_branch_attn"], p["w_branch_s5"]), l)
        y = _mm(merged, p["w_out"], l, tm=cfg.tm, tn=512)
        x, h = _resid_norm(cfg, x, y, gate=mod[l][2], gpost=norm_g[l, 1],
                           gpre=norm_g[l, 2], sc=mod[l][4], sh=mod[l][3])
        u = _mm(h, p["ffn_w_up"], l, tm=cfg.tm, tn=512)
        act = _ffn_act(cfg, u, p["ffn_conv_w"], p["ffn_conv_b"], l)
        f = _mm(act, p["ffn_w_down"], l, tm=min(512, cfg.tm), tn=512, w_buffers=1)
        if l + 1 < DEPTH:
            x, h = _resid_norm(cfg, x, f, gate=mod[l][5], gpost=norm_g[l, 3],
                               gpre=norm_g[l + 1, 0], sc=mod[l + 1][1], sh=mod[l + 1][0])
        else:
            x = _resid_norm(cfg, x, f, gate=mod[l][5], gpost=norm_g[l, 3], split_out=True)

    return (x[0].reshape(cfg.nc, cfg.lc, d), x[1].reshape(cfg.nl, cfg.ll, d),
            jnp.stack(ks, axis=1), jnp.stack(vs, axis=1), jnp.stack(srs, axis=1), jnp.stack(sis, axis=1))


def kernel(x_prompt, x_sample, c, cache_k, cache_v, state_ssm_re, state_ssm_im, c_ctx, w_mod, b_mod, norm_g, w_in, hy_conv_w, hy_conv_b, hy_w1, hy_b1, hy_w2, hy_b2, hy_w3, hy_freq, hy_decay, hy_bias, attn_sink, s5_lam_re, s5_lam_im, s5_log_dt, s5_b_re, s5_b_im, s5_c_re, s5_c_im, s5_d, s5_glu_w, s5_glu_b, w_branch_hy, w_branch_attn, w_branch_s5, w_out, ffn_w_up, ffn_conv_w, ffn_conv_b, ffn_w_down):
    p = dict(w_mod=w_mod, b_mod=b_mod, norm_g=norm_g, w_in=w_in, hy_conv_w=hy_conv_w, hy_conv_b=hy_conv_b,
             hy_w1=hy_w1, hy_b1=hy_b1, hy_w2=hy_w2, hy_b2=hy_b2, hy_w3=hy_w3, hy_freq=hy_freq,
             hy_decay=hy_decay, hy_bias=hy_bias, attn_sink=attn_sink, s5_lam_re=s5_lam_re,
             s5_lam_im=s5_lam_im, s5_log_dt=s5_log_dt, s5_b_re=s5_b_re, s5_b_im=s5_b_im, s5_c_re=s5_c_re,
             s5_c_im=s5_c_im, s5_d=s5_d, s5_glu_w=s5_glu_w, s5_glu_b=s5_glu_b, w_branch_hy=w_branch_hy,
             w_branch_attn=w_branch_attn, w_branch_s5=w_branch_s5, w_out=w_out, ffn_w_up=ffn_w_up,
             ffn_conv_w=ffn_conv_w, ffn_conv_b=ffn_conv_b, ffn_w_down=ffn_w_down)
    cfg = _Cfg(x_prompt.shape[0], x_prompt.shape[1], x_sample.shape[0], x_sample.shape[1])
    return _forward(cfg, x_prompt, x_sample, c, cache_k, cache_v, state_ssm_re, state_ssm_im, c_ctx, p)
```

```python
import functools
import math

import numpy as np
import jax
import jax.numpy as jnp
from jax import lax
from jax.experimental import pallas as pl
from jax.experimental.pallas import tpu as pltpu

F32 = jnp.float32
BF16 = jnp.bfloat16

D_MODEL = 4096
DEPTH = 2
GRID_W = 64
D_BRANCH = D_MODEL // 4
HY_WIDTH = D_BRANCH
HY_BANDS = 16
HY_FEAT = 1 + 2 * HY_BANDS
HY_HID = 64
HEAD_DIM = 128
N_Q_HEADS = D_BRANCH // HEAD_DIM
N_KV_HEADS = 2
GQA_GROUP = N_Q_HEADS // N_KV_HEADS
WINDOW = 128
BLOCK = 128
ROPE_BASE = 10000.0
S5_CH = 16
S5_GROUPS = D_BRANCH // S5_CH
S5_STATE = 64
S5_CHUNK = 16
S5_OCTETS = 8
S5_OCT_K = S5_CHUNK * 128
D_FF = 2 * D_MODEL
EPS = 1e-6
NEG_INF = -1e30

HY_COLS = 3 * HY_WIDTH
Q_COL0 = HY_COLS
K_COL0 = Q_COL0 + N_Q_HEADS * HEAD_DIM
KV_COLS = N_KV_HEADS * HEAD_DIM
V_COL0 = K_COL0 + KV_COLS
S5_COL0 = V_COL0 + KV_COLS
GATE_COL0 = S5_COL0 + D_BRANCH
N_IN = GATE_COL0 + 3 * D_MODEL

V7X_VMEM_REQUEST_MAX = 60 * 1024 * 1024
LANES = 128
SUBLANES = 8
MOD_ROWS = 8


def _nbytes(shape, dtype):
    return math.prod(shape) * jnp.dtype(dtype).itemsize


def _cparams(n_grid, *block_bytes, extra=0):
    est = 2 * sum(block_bytes) + extra + (4 << 20)
    return pltpu.CompilerParams(
        dimension_semantics=("arbitrary",) * n_grid,
        vmem_limit_bytes=int(min(max(est, 16 << 20), V7X_VMEM_REQUEST_MAX)))


def _dot(a, b):
    return jnp.dot(a, b, preferred_element_type=F32)


def _dot_nt(a, b):
    return lax.dot_general(a, b, (((1,), (1,)), ((), ())), preferred_element_type=F32)


def _split(a):
    hi = a.astype(BF16)
    return hi, (a - hi.astype(F32)).astype(BF16)


def _dot3(a, b):
    ah, al = _split(a)
    bh, bl = _split(b)
    return _dot(ah, bh) + (_dot(ah, bl) + _dot(al, bh))


def _sigmoid(x):
    return 1.0 / (1.0 + jnp.exp(-x))


def _gelu(x):
    c = math.sqrt(2.0 / math.pi)
    return x * (0.5 + 0.5 * jnp.tanh(x * (c + (c * 0.044715) * (x * x))))


def _rms(x, g):
    return x * lax.rsqrt(jnp.mean(x * x, axis=-1, keepdims=True) + EPS) * g


class _Cfg:
    def __init__(self, nc, lc, nl, ll):
        self.nc, self.lc, self.nl, self.ll = nc, lc, nl, ll
        self.rc = nc * lc
        self.rl = nl * ll
        self.r = self.rc + self.rl
        self.nseg = 1 + nl
        assert self.rc % ll == 0 and ll % lc == 0 and lc % BLOCK == 0
        self.tm = min(1024, ll)
        self.rb = ll

    def seg_of_row(self, row0):
        return jnp.where(row0 >= self.rc, (row0 - self.rc) // self.ll + 1, 0)


def _mod_kernel(c_ref, w_ref, b_ref, o_ref):
    c = c_ref[...]
    s = (c * _sigmoid(c)).astype(BF16)
    o_ref[...] = _dot(s, w_ref[...].astype(BF16)) + b_ref[...]


def _mod(cvec, w_mod, b_mod):
    n = w_mod.shape[-1]
    tn = 512
    return pl.pallas_call(
        _mod_kernel,
        grid=(DEPTH, n // tn),
        in_specs=[pl.BlockSpec((MOD_ROWS, D_MODEL), lambda l, j: (0, 0)),
                  pl.BlockSpec((None, D_MODEL, tn), lambda l, j: (l, 0, j)),
                  pl.BlockSpec((None, 1, tn), lambda l, j: (l, 0, j))],
        out_specs=pl.BlockSpec((None, MOD_ROWS, tn), lambda l, j: (l, 0, j)),
        out_shape=jax.ShapeDtypeStruct((DEPTH, MOD_ROWS, n), F32),
        compiler_params=_cparams(2, _nbytes((D_MODEL, tn), F32), extra=_nbytes((D_MODEL, tn), BF16)),
        name="mod",
    )(cvec, w_mod, b_mod.reshape(DEPTH, 1, n))


def _resid_norm_kernel(*refs, has_y, has_h, split_in, split_out, n_ctx_tiles):
    it = iter(refs)
    in_ctx = pl.program_id(0) < n_ctx_tiles
    if split_in:
        xa_ref, xb_ref = next(it), next(it)
        x = jnp.where(in_ctx, xa_ref[...], xb_ref[...])
    else:
        x = next(it)[...]
    if has_y:
        y_ref, gate_ref, gpost_ref = next(it), next(it), next(it)
    if has_h:
        gpre_ref, sc_ref, sh_ref = next(it), next(it), next(it)
    if has_y:
        x = x + gate_ref[...] * _rms(y_ref[...], gpost_ref[...])
        if split_out:
            xa_out, xb_out = next(it), next(it)

            @pl.when(in_ctx)
            def _():
                xa_out[...] = x

            @pl.when(jnp.logical_not(in_ctx))
            def _():
                xb_out[...] = x
        else:
            xo_ref = next(it)
            xo_ref[...] = x
    if has_h:
        ho_ref = next(it)
        ho_ref[...] = (_rms(x, gpre_ref[...]) * (1.0 + sc_ref[...]) + sh_ref[...]).astype(BF16)


def _resid_norm(cfg, x, y=None, gate=None, gpost=None, gpre=None, sc=None, sh=None, split_out=False):
    has_y, has_h, split_in = y is not None, gpre is not None, isinstance(x, tuple)
    tm = 256
    nca = cfg.rc // tm
    row = pl.BlockSpec((tm, D_MODEL), lambda i: (i, 0))
    row_a = pl.BlockSpec((tm, D_MODEL), lambda i: (jnp.minimum(i, nca - 1), 0))
    row_b = pl.BlockSpec((tm, D_MODEL), lambda i: (jnp.maximum(i - nca, 0), 0))
    vec = pl.BlockSpec((1, D_MODEL), lambda i: (0, 0))
    seg = pl.BlockSpec((None, 1, D_MODEL), lambda i: (cfg.seg_of_row(i * tm), 0, 0))
    args, specs = (list(x), [row_a, row_b]) if split_in else ([x], [row])
    outs, ospecs = [], []
    if has_y:
        args += [y, gate, gpost.reshape(1, D_MODEL)]
        specs += [row, seg, vec]
        if split_out:
            outs += [jax.ShapeDtypeStruct((cfg.rc, D_MODEL), F32), jax.ShapeDtypeStruct((cfg.rl, D_MODEL), F32)]
            ospecs += [row_a, row_b]
        else:
            outs.append(jax.ShapeDtypeStruct((cfg.r, D_MODEL), F32))
            ospecs.append(row)
    if has_h:
        args += [gpre.reshape(1, D_MODEL), sc, sh]
        specs += [vec, seg, seg]
        outs.append(jax.ShapeDtypeStruct((cfg.r, D_MODEL), BF16))
        ospecs.append(row)
    res = pl.pallas_call(
        functools.partial(_resid_norm_kernel, has_y=has_y, has_h=has_h, split_in=split_in, split_out=split_out,
                          n_ctx_tiles=nca),
        grid=(cfg.r // tm,),
        in_specs=specs, out_specs=ospecs, out_shape=outs,
        compiler_params=_cparams(1, 5 * _nbytes((tm, D_MODEL), F32)),
        name="resid_norm",
    )(*args)
    return res if len(res) > 1 else res[0]


def _mm_kernel(x_ref, w_ref, o_ref, wb_ref):
    @pl.when(pl.program_id(1) == 0)
    def _():
        wb_ref[...] = w_ref[...].astype(BF16)

    o_ref[...] = _dot(x_ref[...], wb_ref[...]).astype(o_ref.dtype)


def _mm(x, w, l, *, tm, tn, out_dtype=F32, w_buffers=2):
    m, k = x.shape
    n = w.shape[-1]
    w_mode = {} if w_buffers == 2 else {"pipeline_mode": pl.Buffered(w_buffers)}
    return pl.pallas_call(
        _mm_kernel,
        grid=(n // tn, m // tm),
        in_specs=[pl.BlockSpec((tm, k), lambda j, i: (i, 0)),
                  pl.BlockSpec((None, k, tn), lambda j, i: (l, 0, j), **w_mode)],
        out_specs=pl.BlockSpec((tm, tn), lambda j, i: (i, j)),
        out_shape=jax.ShapeDtypeStruct((m, n), out_dtype),
        scratch_shapes=[pltpu.VMEM((k, tn), BF16)],
        compiler_params=_cparams(2, _nbytes((tm, k), BF16), _nbytes((tm, tn), out_dtype),
                                 extra=w_buffers * _nbytes((k, tn), F32) + _nbytes((k, tn), BF16)
                                 + _nbytes((tm, tn), F32)),
        name="mm",
    )(x, w)


def _merge_kernel(*refs, n_ctx_tiles):
    g0, g1, g2, hy_c, hy_l, at_c, at_l, y_s5, w0, w1, w2, o_ref, wb_ref = refs

    @pl.when(pl.program_id(1) == 0)
    def _():
        wb_ref[0] = w0[...].astype(BF16)
        wb_ref[1] = w1[...].astype(BF16)
        wb_ref[2] = w2[...].astype(BF16)

    in_ctx = pl.program_id(1) < n_ctx_tiles
    y_hy = jnp.where(in_ctx, hy_c[...], hy_l[...])
    y_at = jnp.where(in_ctx, at_c[...], at_l[...])
    acc = _sigmoid(g0[...]) * _dot(y_hy, wb_ref[0])
    acc += _sigmoid(g1[...]) * _dot(y_at, wb_ref[1])
    acc += _sigmoid(g2[...]) * _dot(y_s5[...], wb_ref[2])
    o_ref[...] = acc.astype(o_ref.dtype)


def _merge(cfg, z, y_hy, y_at, y_s5, ws, l):
    tm, tn = cfg.tm, 512
    nca = cfg.rc // tm
    gate_specs = [pl.BlockSpec((tm, tn), functools.partial(
        lambda j, i, b: (i, (GATE_COL0 + b * D_MODEL) // tn + j), b=b)) for b in range(3)]
    y_spec = pl.BlockSpec((tm, D_BRANCH), lambda j, i: (i, 0))
    y_ctx = pl.BlockSpec((tm, D_BRANCH), lambda j, i: (jnp.minimum(i, nca - 1), 0))
    y_lat = pl.BlockSpec((tm, D_BRANCH), lambda j, i: (jnp.maximum(i - nca, 0), 0))
    w_spec = pl.BlockSpec((None, D_BRANCH, tn), lambda j, i: (l, 0, j))
    return pl.pallas_call(
        functools.partial(_merge_kernel, n_ctx_tiles=nca),
        grid=(D_MODEL // tn, cfg.r // tm),
        in_specs=gate_specs + [y_ctx, y_lat, y_ctx, y_lat, y_spec] + [w_spec] * 3,
        out_specs=pl.BlockSpec((tm, tn), lambda j, i: (i, j)),
        out_shape=jax.ShapeDtypeStruct((cfg.r, D_MODEL), BF16),
        scratch_shapes=[pltpu.VMEM((3, D_BRANCH, tn), BF16)],
        compiler_params=_cparams(2, 3 * _nbytes((tm, tn), F32), 5 * _nbytes((tm, D_BRANCH), BF16),
                                 3 * _nbytes((D_BRANCH, tn), F32), _nbytes((tm, tn), BF16),
                                 extra=3 * _nbytes((D_BRANCH, tn), BF16) + 2 * _nbytes((tm, tn), F32)
                                 + 2 * _nbytes((tm, D_BRANCH), BF16)),
        name="merge",
    )(z, z, z, *y_hy, *y_at, y_s5, *ws)


def _block_seq_len(cfg):
    return jnp.where(pl.program_id(0) < cfg.rc // cfg.rb, cfg.lc, cfg.ll)


def _dwconv3_rows(x_ref, w_ref, b_ref, r0, sl):
    total = x_ref.shape[0]
    cur = x_ref[r0:r0 + BLOCK, :]
    sub = lax.broadcasted_iota(jnp.int32, (SUBLANES, cur.shape[1]), 0)
    xm = x_ref[r0 - 1:r0 - 1 + BLOCK, :] if r0 > 0 else pltpu.roll(cur, 1, 0)
    first = jnp.where(((r0 + sub) & (sl - 1)) == 0, 0.0, xm[:SUBLANES])
    xm = jnp.concatenate([first, xm[SUBLANES:]], axis=0)
    xp = x_ref[r0 + 1:r0 + 1 + BLOCK, :] if r0 + BLOCK < total else pltpu.roll(cur, BLOCK - 1, 0)
    last = jnp.where(((r0 + BLOCK - SUBLANES + sub) & (sl - 1)) == sl - 1, 0.0, xp[BLOCK - SUBLANES:])
    xp = jnp.concatenate([xp[:BLOCK - SUBLANES], last], axis=0)
    return xm * w_ref[0:1, :] + cur * w_ref[1:2, :] + xp * w_ref[2:3, :] + b_ref[...]


def _hy_dwconv_kernel(x_ref, w_ref, b_ref, oe_ref, oo_ref, *, cfg):
    sl = _block_seq_len(cfg)
    half = x_ref.shape[0] // 2
    xe = x_ref[pl.ds(0, half, stride=2), :]
    xo = x_ref[pl.ds(1, half, stride=2), :]
    m = lax.broadcasted_iota(jnp.int32, xe.shape, 0)
    before_even = jnp.where(((2 * m) & (sl - 1)) == 0, 0.0, pltpu.roll(xo, 1, 0))
    after_odd = jnp.where(((2 * m + 1) & (sl - 1)) == sl - 1, 0.0, pltpu.roll(xe, half - 1, 0))
    w0, w1, w2 = w_ref[0:1, :], w_ref[1:2, :], w_ref[2:3, :]
    oe_ref[...] = before_even * w0 + xe * w1 + xo * w2 + b_ref[...]
    oo_ref[...] = xe * w0 + xo * w1 + after_odd * w2 + b_ref[...]


def _hy_dwconv(cfg, z, w, b, l):
    tn = LANES
    blk = _nbytes((cfg.rb, tn), F32)
    half = pl.BlockSpec((cfg.rb // 2, tn), lambda r, j: (r, j))
    return pl.pallas_call(
        functools.partial(_hy_dwconv_kernel, cfg=cfg),
        grid=(cfg.r // cfg.rb, HY_COLS // tn),
        in_specs=[pl.BlockSpec((cfg.rb, tn), lambda r, j: (r, j)),
                  pl.BlockSpec((None, 3, tn), lambda r, j: (l, 0, j)),
                  pl.BlockSpec((None, 1, tn), lambda r, j: (l, 0, j))],
        out_specs=[half, half],
        out_shape=[jax.ShapeDtypeStruct((cfg.r // 2, HY_COLS), F32)] * 2,
        compiler_params=_cparams(2, 2 * blk, extra=4 * blk),
        name="hy_dwconv",
    )(z, w, b.reshape(DEPTH, 1, HY_COLS))


def _ffn_act_kernel(a_ref, b_ref, wa_ref, wb_ref, ba_ref, bb_ref, o_ref, *, cfg):
    sl = _block_seq_len(cfg)
    for r0 in range(0, a_ref.shape[0], BLOCK):
        a = _dwconv3_rows(a_ref, wa_ref, ba_ref, r0, sl)
        b = _dwconv3_rows(b_ref, wb_ref, bb_ref, r0, sl)
        o_ref[r0:r0 + BLOCK, :] = (_gelu(a) * b).astype(BF16)


def _ffn_act(cfg, u, w, b, l):
    tn = 256
    nb = D_FF // tn
    blk = _nbytes((cfg.rb, tn), F32)
    b3 = b.reshape(DEPTH, 1, 2 * D_FF)
    return pl.pallas_call(
        functools.partial(_ffn_act_kernel, cfg=cfg),
        grid=(cfg.r // cfg.rb, nb),
        in_specs=[pl.BlockSpec((cfg.rb, tn), lambda r, j: (r, j)),
                  pl.BlockSpec((cfg.rb, tn), lambda r, j: (r, j + nb)),
                  pl.BlockSpec((None, 3, tn), lambda r, j: (l, 0, j)),
                  pl.BlockSpec((None, 3, tn), lambda r, j: (l, 0, j + nb)),
                  pl.BlockSpec((None, 1, tn), lambda r, j: (l, 0, j)),
                  pl.BlockSpec((None, 1, tn), lambda r, j: (l, 0, j + nb))],
        out_specs=pl.BlockSpec((cfg.rb, tn), lambda r, j: (r, j)),
        out_shape=jax.ShapeDtypeStruct((cfg.r, D_FF), BF16),
        compiler_params=_cparams(2, 3 * blk, extra=6 * blk),
        name="ffn_act",
    )(u, u, w, w, b3, b3)


def _dft_tables_kernel(ac_ref, as_ref, bc_ref, bs_ref, c_ref, s_ref):
    nb = bc_ref.shape[1]
    for a in range(ac_ref.shape[1]):
        ca, sa = ac_ref[:, a:a + 1], as_ref[:, a:a + 1]
        c_ref[:, a * nb:(a + 1) * nb] = (ca * bc_ref[...] - sa * bs_ref[...]).astype(BF16)
        s_ref[:, a * nb:(a + 1) * nb] = (sa * bc_ref[...] + ca * bs_ref[...]).astype(BF16)


def _dft_tables(length):
    half = length // 2
    period = 4 * length
    r = min(LANES, half)
    tk = min(256, half)
    idx = np.arange(half, dtype=np.int64)
    hi = np.arange(half // r, dtype=np.int64) * r
    lo = np.arange(r, dtype=np.int64)

    def cs(m):
        ang = 2.0 * np.pi * (m % period).astype(np.float64) / period
        return jnp.asarray(np.cos(ang), F32), jnp.asarray(np.sin(ang), F32)

    def expand(phase_hi, phase_lo):
        spec_a = pl.BlockSpec((tk, half // r), lambda i: (i, 0))
        spec_b = pl.BlockSpec((tk, r), lambda i: (i, 0))
        spec_o = pl.BlockSpec((tk, half), lambda i: (i, 0))
        return pl.pallas_call(
            _dft_tables_kernel,
            grid=(half // tk,),
            in_specs=[spec_a, spec_a, spec_b, spec_b],
            out_specs=[spec_o, spec_o],
            out_shape=[jax.ShapeDtypeStruct((half, half), BF16)] * 2,
            compiler_params=_cparams(1, 4 * _nbytes((tk, LANES), F32), 2 * _nbytes((tk, half), BF16)),
            name="dft_tables",
        )(*cs(phase_hi), *cs(phase_lo))

    odd = (2 * idx + 1)[:, None]
    row = idx[:, None]
    out = {}
    out["ce"], out["se"] = expand(odd * (2 * hi)[None, :], odd * (2 * lo)[None, :])
    out["co"], out["so"] = expand(odd * (2 * hi)[None, :], odd * (2 * lo + 1)[None, :])
    out["cet"], out["set"] = expand(2 * row * (2 * hi)[None, :], 2 * row * (2 * lo + 1)[None, :])
    out["cot"], out["sot"] = expand((2 * row + 1) * (2 * hi)[None, :], (2 * row + 1) * (2 * lo + 1)[None, :])
    return out


def _hy_feats(length):
    pos = jnp.arange(length, dtype=F32)
    t = (pos / length)[:, None]
    bands = jnp.linspace(1e-4, HY_BANDS - 1, HY_BANDS, dtype=F32)
    wpos = 2.0 * math.pi * t * bands
    feats = jnp.concatenate([t, jnp.cos(wpos), -jnp.sin(wpos)], axis=-1)
    feats = jnp.concatenate([feats[0::2], feats[1::2]], axis=0)
    return jnp.pad(feats, ((0, 0), (0, LANES - HY_FEAT)))


def _hy_filter_kernel(f_ref, w1_ref, b1_ref, w2_ref, b2_ref, w3_ref, fr_ref, dec_ref, hs_ref, hd_ref, hb0_ref):
    feats = f_ref[...]
    fr = fr_ref[...]
    h = jnp.sin(fr * (_dot3(feats, w1_ref[...]) + b1_ref[...]))
    h = jnp.sin(fr * (_dot3(h, w2_ref[...]) + b2_ref[...]))
    h = _dot3(h, w3_ref[...]) * jnp.exp(-feats[:, 0:1] * jnp.abs(dec_ref[...]))
    half = 2 * HY_WIDTH
    hf, hb = h[:, :half], h[:, half:]
    hs_ref[...] = (hf + hb).astype(BF16)
    hd_ref[...] = (hb - hf).astype(BF16)

    @pl.when(pl.program_id(0) == 0)
    def _():
        hb0_ref[...] = jnp.broadcast_to(hb[0:1, :], hb0_ref.shape)


def _hy_filter(length, feats, w1p, b1, w2, b2, w3, freq, decay, l):
    tl = 256
    half = 2 * HY_WIDTH
    lsel = lambda i: (l, 0, 0)
    return pl.pallas_call(
        _hy_filter_kernel,
        grid=(length // tl,),
        in_specs=[pl.BlockSpec((tl, LANES), lambda i: (i, 0)),
                  pl.BlockSpec((None, LANES, HY_HID), lsel),
                  pl.BlockSpec((None, 1, HY_HID), lsel),
                  pl.BlockSpec((None, HY_HID, HY_HID), lsel),
                  pl.BlockSpec((None, 1, HY_HID), lsel),
                  pl.BlockSpec((None, HY_HID, 2 * half), lsel),
                  pl.BlockSpec((None, 1, HY_HID), lsel),
                  pl.BlockSpec((None, 1, 2 * half), lsel)],
        out_specs=[pl.BlockSpec((tl, half), lambda i: (i, 0)),
                   pl.BlockSpec((tl, half), lambda i: (i, 0)),
                   pl.BlockSpec((8, half), lambda i: (0, 0))],
        out_shape=[jax.ShapeDtypeStruct((length, half), BF16),
                   jax.ShapeDtypeStruct((length, half), BF16),
                   jax.ShapeDtypeStruct((8, half), F32)],
        compiler_params=_cparams(1, 2 * _nbytes((tl, half), F32), extra=6 * _nbytes((tl, 2 * half), F32)),
        name="hy_filter",
    )(feats, w1p, b1, w2, b2, w3, freq, decay)


def _hy_spec_kernel(ce, se, co, so, hse, hso, hde, hdo, hb0_ref, kar, kai, kbr, kbi):
    a = _dot(ce[...], hse[...])
    b = _dot(co[...], hso[...])
    c = _dot(se[...], hde[...])
    d = _dot(so[...], hdo[...])
    kar[...] = a + b - hb0_ref[0:1, :]
    kai[...] = c + d
    kbr[...] = a - b - hb0_ref[0:1, :]
    kbi[...] = d - c


def _hy_spec(length, mats, hs, hd, hb0):
    half_len = length // 2
    tk = min(512, half_len)
    tn = 512
    width = 2 * HY_WIDTH
    ncb = width // tn
    fspec = pl.BlockSpec((tk, half_len), lambda j, i: (i, 0))
    espec = pl.BlockSpec((half_len, tn), lambda j, i: (0, j))
    ospec = pl.BlockSpec((half_len, tn), lambda j, i: (1, j))
    kspec = pl.BlockSpec((tk, tn), lambda j, i: (i, j))
    return pl.pallas_call(
        _hy_spec_kernel,
        grid=(ncb, half_len // tk),
        in_specs=[fspec] * 4 + [espec, ospec, espec, ospec, pl.BlockSpec((8, tn), lambda j, i: (0, j))],
        out_specs=[kspec] * 4,
        out_shape=[jax.ShapeDtypeStruct((half_len, width), F32)] * 4,
        compiler_params=_cparams(2, 4 * _nbytes((tk, half_len), BF16), 4 * _nbytes((half_len, tn), BF16),
                                 4 * _nbytes((tk, tn), F32), extra=4 * _nbytes((tk, tn), F32)),
        name="hy_spec",
    )(mats["ce"], mats["se"], mats["co"], mats["so"], hs, hs, hd, hd, hb0)


def _hy_fwd_kernel(ce, se, co, so, ue_ref, uo_ref, kar, kai, kbr, kbi, pr_ref, pi_ref, qr_ref, qi_ref, ub_ref):
    @pl.when(pl.program_id(2) == 0)
    def _():
        ub_ref[0] = ue_ref[...].astype(BF16)
        ub_ref[1] = uo_ref[...].astype(BF16)

    e_re, e_im = _dot(ce[...], ub_ref[0]), -_dot(se[...], ub_ref[0])
    o_re, o_im = _dot(co[...], ub_ref[1]), -_dot(so[...], ub_ref[1])
    ua_re, ua_im = e_re + o_re, e_im + o_im
    ub_re, ub_im = e_re - o_re, o_im - e_im
    ya_re = ua_re * kar[...] - ua_im * kai[...]
    ya_im = ua_re * kai[...] + ua_im * kar[...]
    yb_re = ub_re * kbr[...] - ub_im * kbi[...]
    yb_im = ub_re * kbi[...] + ub_im * kbr[...]
    pr_ref[...] = (ya_re + yb_re).astype(BF16)
    pi_ref[...] = (ya_im - yb_im).astype(BF16)
    qr_ref[...] = (ya_re - yb_re).astype(BF16)
    qi_ref[...] = (ya_im + yb_im).astype(BF16)


def _hy_inv_kernel(cet, set_, cot, sot, pr, pi_, qr, qi, ue_ref, uo_ref, ge_ref, go_ref, bias_ref, ye_ref, yo_ref,
                   *, inv_len):
    even = _dot(cet[...], pr[...]) - _dot(set_[...], pi_[...])
    odd = _dot(cot[...], qr[...]) - _dot(sot[...], qi[...])
    ye_ref[...] = (ge_ref[...] * (even * inv_len + ue_ref[...] * bias_ref[...])).astype(ye_ref.dtype)
    yo_ref[...] = (go_ref[...] * (odd * inv_len + uo_ref[...] * bias_ref[...])).astype(yo_ref.dtype)


def _hy_conv(length, nseq, mats, spec, order, data, gate, bias, l, out_dtype):
    half_len = length // 2
    tn = min(512, HY_WIDTH)
    tk = min(512, half_len)
    ncb = HY_WIDTH // tn
    ntk = half_len // tk
    fspec = pl.BlockSpec((tk, half_len), lambda s, j, i: (i, 0))
    kspec = pl.BlockSpec((tk, tn), lambda s, j, i: (i, order * ncb + j))
    tile = pl.BlockSpec((tk, tn), lambda s, j, i: (s * ntk + i, j))

    def full(loc):
        _, rb0, cb0 = loc
        return pl.BlockSpec((half_len, tn), lambda s, j, i: (rb0 + s, cb0 + j))

    def rows(loc):
        _, rb0, cb0 = loc
        return pl.BlockSpec((tk, tn), lambda s, j, i: ((rb0 + s) * ntk + i, cb0 + j))

    pq = pl.pallas_call(
        _hy_fwd_kernel,
        grid=(nseq, ncb, ntk),
        in_specs=[fspec] * 4 + [full(data[0]), full(data[1])] + [kspec] * 4,
        out_specs=[tile] * 4,
        out_shape=[jax.ShapeDtypeStruct((nseq * half_len, HY_WIDTH), BF16)] * 4,
        scratch_shapes=[pltpu.VMEM((2, half_len, tn), BF16)],
        compiler_params=_cparams(3, 4 * _nbytes((tk, half_len), BF16), 2 * _nbytes((half_len, tn), F32),
                                 4 * _nbytes((tk, tn), F32), 4 * _nbytes((tk, tn), BF16),
                                 extra=2 * _nbytes((half_len, tn), BF16) + 12 * _nbytes((tk, tn), F32)),
        name="hy_fwd",
    )(mats["ce"], mats["se"], mats["co"], mats["so"], data[0][0], data[1][0], *spec)

    resident = pl.BlockSpec((half_len, tn), lambda s, j, i: (s, j))
    return pl.pallas_call(
        functools.partial(_hy_inv_kernel, inv_len=1.0 / length),
        grid=(nseq, ncb, ntk),
        in_specs=[fspec] * 4 + [resident] * 4 + [rows(data[0]), rows(data[1]), rows(gate[0]), rows(gate[1]),
                                                 pl.BlockSpec((None, None, 1, tn), lambda s, j, i: (l, order, 0, j))],
        out_specs=[tile, tile],
        out_shape=[jax.ShapeDtypeStruct((nseq * half_len, HY_WIDTH), out_dtype)] * 2,
        compiler_params=_cparams(3, 4 * _nbytes((tk, half_len), BF16), 4 * _nbytes((half_len, tn), BF16),
                                 6 * _nbytes((tk, tn), F32), extra=6 * _nbytes((tk, tn), F32)),
        name="hy_inv",
    )(mats["cet"], mats["set"], mats["cot"], mats["sot"], *pq, data[0][0], data[1][0], gate[0][0], gate[1][0], bias)


def _hyena(cfg, consts, z, p, l):
    zce, zco = _hy_dwconv(cfg, z, p["hy_conv_w"], p["hy_conv_b"], l)
    bias = p["hy_bias"].reshape(DEPTH, 2, 1, HY_WIDTH)
    w1p = jnp.pad(p["hy_w1"], ((0, 0), (0, LANES - HY_FEAT), (0, 0)))
    tn = min(512, HY_WIDTH)
    outs = []
    for length, nseq, row0 in ((cfg.lc, cfg.nc, 0), (cfg.ll, cfg.nl, cfg.rc)):
        mats = consts["dft"][length]
        hs, hd, hb0 = _hy_filter(length, consts["feats"][length], w1p, p["hy_b1"].reshape(DEPTH, 1, HY_HID),
                                 p["hy_w2"], p["hy_b2"].reshape(DEPTH, 1, HY_HID), p["hy_w3"],
                                 p["hy_freq"].reshape(DEPTH, 1, HY_HID),
                                 p["hy_decay"].reshape(DEPTH, 1, 4 * HY_WIDTH), l)
        spec = _hy_spec(length, mats, hs, hd, hb0)
        rb0 = row0 // length

        def zc_cols(col0):
            return (zce, rb0, col0 // tn), (zco, rb0, col0 // tn)

        y1 = _hy_conv(length, nseq, mats, spec, 0, zc_cols(0), zc_cols(HY_WIDTH), bias, l, F32)
        y2 = _hy_conv(length, nseq, mats, spec, 1, ((y1[0], 0, 0), (y1[1], 0, 0)), zc_cols(2 * HY_WIDTH), bias, l,
                      BF16)
        outs.append(jnp.stack(y2, axis=1).reshape(nseq * length, HY_WIDTH))
    return tuple(outs)


def _softmax_sink_pv(parts, sink):
    m = sink
    for s, _ in parts:
        m = jnp.maximum(m, jnp.max(s, axis=-1, keepdims=True))
    den = jnp.exp(sink - m)
    acc = None
    for s, v in parts:
        e = jnp.exp(s - m)
        den = den + jnp.sum(e, axis=-1, keepdims=True)
        pv = _dot(e.astype(BF16), v)
        acc = pv if acc is None else acc + pv
    return acc / den


def _group_sink(sink_ref, h):
    return jnp.concatenate([jnp.broadcast_to(sink_ref[hq:hq + 1, 0:1], (BLOCK, 1))
                            for hq in range(h * GQA_GROUP, (h + 1) * GQA_GROUP)], axis=0)


def _store_group(o_ref, h, o):
    for g in range(GQA_GROUP):
        hq = h * GQA_GROUP + g
        o_ref[:, hq * HEAD_DIM:(hq + 1) * HEAD_DIM] = o[g * BLOCK:(g + 1) * BLOCK].astype(BF16)


def _attn_ctx_kernel(q_ref, k_ref, v_ref, sink_ref, o_ref):
    scale = HEAD_DIM ** -0.5
    for h in range(N_KV_HEADS):
        hs = slice(h * HEAD_DIM, (h + 1) * HEAD_DIM)
        k = k_ref[:, hs].astype(BF16)
        v = v_ref[:, hs].astype(BF16)
        q = jnp.concatenate([q_ref[:, hq * HEAD_DIM:(hq + 1) * HEAD_DIM]
                             for hq in range(h * GQA_GROUP, (h + 1) * GQA_GROUP)], axis=0).astype(BF16)
        s = _dot_nt(q, k) * scale
        _store_group(o_ref, h, _softmax_sink_pv([(s, v)], _group_sink(sink_ref, h)))


def _attn_ctx(cfg, z, sink):
    nb = cfg.lc // BLOCK
    qw = N_Q_HEADS * HEAD_DIM
    return pl.pallas_call(
        _attn_ctx_kernel,
        grid=(cfg.nc, nb),
        in_specs=[pl.BlockSpec((BLOCK, qw), lambda s, i: (s * nb + i, Q_COL0 // qw)),
                  pl.BlockSpec((cfg.lc, KV_COLS), lambda s, i: (s, K_COL0 // KV_COLS)),
                  pl.BlockSpec((cfg.lc, KV_COLS), lambda s, i: (s, V_COL0 // KV_COLS)),
                  pl.BlockSpec((N_Q_HEADS, LANES), lambda s, i: (0, 0))],
        out_specs=pl.BlockSpec((BLOCK, qw), lambda s, i: (s * nb + i, 0)),
        out_shape=jax.ShapeDtypeStruct((cfg.rc, qw), BF16),
        compiler_params=_cparams(2, _nbytes((BLOCK, qw), F32), 2 * _nbytes((cfg.lc, KV_COLS), F32)),
        name="attn_ctx",
    )(z, z, z, sink)


def _rope(x, c, s):
    return x * c + pltpu.roll(x, HEAD_DIM // 2, 1) * s


def _attn_lat_kernel(q_ref, kp_ref, kc_ref, kn_ref, vp_ref, vc_ref, vn_ref, ck_ref, cv_ref,
                     cq, sq, cp, sp, cn, sn, sink_ref, o_ref, *, nb):
    i = pl.program_id(1)
    scale = HEAD_DIM ** -0.5
    shape = (GQA_GROUP * BLOCK, 3 * BLOCK)
    qi = lax.broadcasted_iota(jnp.int32, shape, 0) & (BLOCK - 1)
    kj = lax.broadcasted_iota(jnp.int32, shape, 1)
    ok = jnp.abs(kj - BLOCK - qi) <= WINDOW
    ok = ok & ((kj >= BLOCK) | (i > 0)) & ((kj < 2 * BLOCK) | (i < nb - 1))
    for h in range(N_KV_HEADS):
        hs = slice(h * HEAD_DIM, (h + 1) * HEAD_DIM)
        kw = jnp.concatenate([_rope(kp_ref[:, hs], cp[...], sp[...]),
                              _rope(kc_ref[:, hs], cq[...], sq[...]),
                              _rope(kn_ref[:, hs], cn[...], sn[...])], axis=0).astype(BF16)
        vw = jnp.concatenate([vp_ref[:, hs], vc_ref[:, hs], vn_ref[:, hs]], axis=0).astype(BF16)
        kctx = ck_ref[:, hs].astype(BF16)
        vctx = cv_ref[:, hs].astype(BF16)
        q = jnp.concatenate([_rope(q_ref[:, hq * HEAD_DIM:(hq + 1) * HEAD_DIM], cq[...], sq[...])
                             for hq in range(h * GQA_GROUP, (h + 1) * GQA_GROUP)], axis=0).astype(BF16)
        s1 = _dot_nt(q, kctx) * scale
        s2 = jnp.where(ok, _dot_nt(q, kw) * scale, NEG_INF)
        _store_group(o_ref, h, _softmax_sink_pv([(s1, vctx), (s2, vw)], _group_sink(sink_ref, h)))


def _attn_lat(cfg, consts, z, ck, cv, sink):
    nb = cfg.ll // BLOCK
    rb0 = cfg.rc // BLOCK
    qw = N_Q_HEADS * HEAD_DIM
    past = ck.shape[1]
    cos2, sin2 = consts["rope"]

    def blk(delta):
        return lambda b, i: rb0 + b * nb + jnp.clip(i + delta, 0, nb - 1)

    def zspec(width, col0, delta):
        rowf = blk(delta)
        return pl.BlockSpec((BLOCK, width), lambda b, i: (rowf(b, i), col0 // width))

    def tspec(delta):
        return pl.BlockSpec((BLOCK, HEAD_DIM), lambda b, i: (jnp.clip(i + delta, 0, nb - 1), 0))

    cspec = pl.BlockSpec((None, past, KV_COLS), lambda b, i: (b, 0, 0))
    return pl.pallas_call(
        functools.partial(_attn_lat_kernel, nb=nb),
        grid=(cfg.nl, nb),
        in_specs=[zspec(qw, Q_COL0, 0),
                  zspec(KV_COLS, K_COL0, -1), zspec(KV_COLS, K_COL0, 0), zspec(KV_COLS, K_COL0, 1),
                  zspec(KV_COLS, V_COL0, -1), zspec(KV_COLS, V_COL0, 0), zspec(KV_COLS, V_COL0, 1),
                  cspec, cspec,
                  tspec(0), tspec(0), tspec(-1), tspec(-1), tspec(1), tspec(1),
                  pl.BlockSpec((N_Q_HEADS, LANES), lambda b, i: (0, 0))],
        out_specs=pl.BlockSpec((BLOCK, qw), lambda b, i: (b * nb + i, 0)),
        out_shape=jax.ShapeDtypeStruct((cfg.rl, qw), BF16),
        compiler_params=_cparams(2, _nbytes((BLOCK, qw), F32), 6 * _nbytes((BLOCK, KV_COLS), F32),
                                 2 * _nbytes((past, KV_COLS), F32)),
        name="attn_lat",
    )(z, z, z, z, z, z, z, ck, cv, cos2, sin2, cos2, sin2, cos2, sin2, sink)


def _rope_tables(length):
    rows = length // GRID_W
    row = jnp.repeat(jnp.arange(rows, dtype=F32), GRID_W)
    col = jnp.tile(jnp.arange(GRID_W, dtype=F32), rows)
    n_freq = HEAD_DIM // 4
    inv = ROPE_BASE ** (-jnp.arange(n_freq, dtype=F32) / n_freq)
    ang = jnp.concatenate([row[:, None] * inv, col[:, None] * inv], axis=-1)
    cos, sin = jnp.cos(ang), jnp.sin(ang)
    return jnp.concatenate([cos, cos], axis=-1), jnp.concatenate([-sin, sin], axis=-1)


def _s5_operands(p):
    t_len = S5_CHUNK
    lr, li = p["s5_lam_re"].astype(F32), p["s5_lam_im"].astype(F32)
    n_layers = lr.shape[0]
    dt = jnp.exp(p["s5_log_dt"].astype(F32))[..., None]
    zr, zi = lr * dt, li * dt
    mag = jnp.exp(zr)
    ar, ai = mag * jnp.cos(zi), mag * jnp.sin(zi)
    den = lr * lr + li * li
    cr = ((ar - 1.0) * lr + ai * li) / den
    ci = (ai * lr - (ar - 1.0) * li) / den
    k = jnp.arange(t_len + 1, dtype=F32)[:, None, None, None, None]
    pm = jnp.exp(k * zr)
    pr, pi = pm * jnp.cos(k * zi), pm * jnp.sin(k * zi)

    prs, pis = pr.transpose(1, 2, 0, 3, 4), pi.transpose(1, 2, 0, 3, 4)

    def per_direction(x, fwd, bwd):
        return jnp.concatenate([fwd(x[:, 0:1]), bwd(x[:, 1:2])], axis=1)

    bt_re = p["s5_b_re"].astype(F32).transpose(0, 1, 2, 4, 3)
    bt_im = p["s5_b_im"].astype(F32).transpose(0, 1, 2, 4, 3)
    bbr = (cr[:, :, :, None, :] * bt_re - ci[:, :, :, None, :] * bt_im)[:, :, None]
    bbi = (cr[:, :, :, None, :] * bt_im + ci[:, :, :, None, :] * bt_re)[:, :, None]
    kp = [per_direction(x, lambda y: jnp.flip(y[:, :, :t_len], axis=2), lambda y: y[:, :, :t_len])
          [:, :, :, :, None, :] for x in (prs, pis)]
    pf = jnp.concatenate([kp[0] * bbr - kp[1] * bbi, kp[0] * bbi + kp[1] * bbr], axis=-1)
    pf = pf.reshape(n_layers, 2, t_len, D_BRANCH, 2 * S5_STATE)

    c_re, c_im = p["s5_c_re"].astype(F32), p["s5_c_im"].astype(F32)
    cmat = jnp.concatenate([c_re, -c_im], axis=-1).reshape(n_layers, 2, D_BRANCH, 2 * S5_STATE)
    ct_re = c_re.reshape(n_layers, 2, D_BRANCH, S5_STATE).transpose(0, 1, 3, 2)[:, :, None]
    ct_im = c_im.reshape(n_layers, 2, D_BRANCH, S5_STATE).transpose(0, 1, 3, 2)[:, :, None]
    to_lanes = jnp.asarray(np.repeat(np.eye(S5_GROUPS, dtype=np.float32), S5_CH, axis=1))
    kq = [jnp.dot(per_direction(x, lambda y: y[:, :, 1:], lambda y: jnp.flip(y[:, :, 1:], axis=2))
                  .transpose(0, 1, 2, 4, 3).reshape(-1, S5_GROUPS), to_lanes, precision=lax.Precision.HIGHEST)
          .reshape(n_layers, 2, t_len, S5_STATE, D_BRANCH) for x in (prs, pis)]
    qf = jnp.concatenate([ct_re * kq[0] - ct_im * kq[1], -(ct_re * kq[1] + ct_im * kq[0])], axis=3)

    a_chunk = jnp.stack([jnp.concatenate([pr[t_len], pr[t_len]], axis=-1),
                         jnp.concatenate([-pi[t_len], pi[t_len]], axis=-1)], axis=2)
    return pf, qf, cmat, a_chunk


def _s5_ucat(z_ref):
    return jnp.concatenate([z_ref[:, t, :].astype(BF16) for t in range(S5_CHUNK)], axis=1)


def _octet_group(shape, axis):
    return lax.broadcasted_iota(jnp.int32, shape, axis) // S5_CH


def _s5_in_kernel(z_ref, pf_ref, x_ref, p_scr):
    gpo = S5_GROUPS // S5_OCTETS

    @pl.when(pl.program_id(1) == 0)
    def _():
        row_group = _octet_group((LANES, LANES), 0)
        for d in range(2):
            for t in range(S5_CHUNK):
                blk = pf_ref[d, t]
                for g in range(gpo):
                    p_scr[d, t * LANES:(t + 1) * LANES, g * LANES:(g + 1) * LANES] = (
                        jnp.where(row_group == g, blk, 0.0).astype(BF16))

    u = _s5_ucat(z_ref)
    for d in range(2):
        x_ref[d] = _dot(u, p_scr[d]).reshape(u.shape[0], gpo, 2 * S5_STATE)


def _s5_in(z3, pf, l, tr):
    r16 = z3.shape[0]
    gpo = S5_GROUPS // S5_OCTETS
    w = 2 * S5_STATE
    return pl.pallas_call(
        _s5_in_kernel,
        grid=(S5_OCTETS, r16 // tr),
        in_specs=[pl.BlockSpec((tr, S5_CHUNK, LANES), lambda o, i: (i, 0, S5_COL0 // LANES + o)),
                  pl.BlockSpec((None, 2, S5_CHUNK, LANES, w), lambda o, i: (l, 0, 0, o, 0))],
        out_specs=pl.BlockSpec((2, tr, gpo, w), lambda o, i: (0, i, o, 0)),
        out_shape=jax.ShapeDtypeStruct((2, r16, S5_GROUPS, w), F32),
        scratch_shapes=[pltpu.VMEM((2, S5_OCT_K, gpo * w), BF16)],
        compiler_params=_cparams(2, _nbytes((tr, S5_CHUNK, LANES), F32), _nbytes((2, S5_CHUNK, LANES, w), F32),
                                 _nbytes((2, tr, gpo, w), F32),
                                 extra=_nbytes((2, S5_OCT_K, gpo * w), BF16) + 4 * _nbytes((tr, S5_OCT_K), F32)),
        name="s5_in",
    )(z3, pf)


def _s5_scan_kernel(*refs, n_in, nsb, jb):
    a_ref, h0_ref = refs[0], refs[1]
    xf = refs[2:2 + n_in]
    xb = refs[2 + n_in:2 + 2 * n_in]
    sf = refs[2 + 2 * n_in:2 + 3 * n_in]
    sb = refs[2 + 3 * n_in:2 + 4 * n_in]
    fin_ref, st_ref = refs[2 + 4 * n_in], refs[3 + 4 * n_in]
    t = pl.program_id(0)

    @pl.when(t == 0)
    def _():
        st_ref[...] = h0_ref[...]

    half = S5_STATE

    def body(s, carry):
        jf = s
        jr = jb - 1 - s
        for r in range(n_in):
            for q in range(nsb):
                idx = r * nsb + q
                cur = st_ref[0, idx]
                sf[r][q * jb + jf] = cur
                st_ref[0, idx] = cur * a_ref[0, 0] + pltpu.roll(cur, half, 1) * a_ref[0, 1] + xf[r][q * jb + jf]
                cur = st_ref[1, idx]
                sb[r][q * jb + jr] = cur
                st_ref[1, idx] = cur * a_ref[1, 0] + pltpu.roll(cur, half, 1) * a_ref[1, 1] + xb[r][q * jb + jr]
        return carry

    lax.fori_loop(0, jb, body, 0)

    @pl.when(t == pl.num_programs(0) - 1)
    def _():
        fin_ref[...] = st_ref[...]


def _s5_scan(x, a_chunk, h0, l, row0, n_in, nsb, nj, jb):
    g = x.shape[2]
    w = x.shape[3]
    nblk = nj // jb
    nseq = n_in * nsb
    rows = nsb * jb
    base = row0 // rows

    def xspec(d, r):
        if d == 0:
            return pl.BlockSpec((None, rows, g, w), lambda t: (0, base + r * nblk + t, 0, 0))
        return pl.BlockSpec((None, rows, g, w), lambda t: (1, base + r * nblk + nblk - 1 - t, 0, 0))

    def sspec(d):
        if d == 0:
            return pl.BlockSpec((rows, g, w), lambda t: (t, 0, 0))
        return pl.BlockSpec((rows, g, w), lambda t: (nblk - 1 - t, 0, 0))

    st_spec = pl.BlockSpec((2, nseq, g, w), lambda t: (0, 0, 0, 0))
    res = pl.pallas_call(
        functools.partial(_s5_scan_kernel, n_in=n_in, nsb=nsb, jb=jb),
        grid=(nblk,),
        in_specs=[pl.BlockSpec((None, 2, 2, g, w), lambda t: (l, 0, 0, 0, 0)), st_spec]
                 + [xspec(0, r) for r in range(n_in)] + [xspec(1, r) for r in range(n_in)],
        out_specs=[sspec(0)] * n_in + [sspec(1)] * n_in + [st_spec],
        out_shape=[jax.ShapeDtypeStruct((nsb * nj, g, w), F32)] * (2 * n_in)
                  + [jax.ShapeDtypeStruct((2, nseq, g, w), F32)],
        scratch_shapes=[pltpu.VMEM((2, nseq, g, w), F32)],
        compiler_params=_cparams(1, 4 * n_in * _nbytes((rows, g, w), F32), 3 * _nbytes((2, nseq, g, w), F32)),
        name="s5_scan",
    )(a_chunk, h0, *([x] * (2 * n_in)))
    return res[:n_in], res[n_in:2 * n_in], res[2 * n_in]


def _s5_out_kernel(z_ref, pf_ref, cm_ref, qf_ref, d_ref, *rest, starts):
    n = len(starts)
    sf_refs, sb_refs = rest[:n], rest[n:2 * n]
    y_ref, m_scr, q_scr = rest[2 * n:]
    gpo = S5_GROUPS // S5_OCTETS

    @pl.when(pl.program_id(1) == 0)
    def _():
        same_group = _octet_group((LANES, LANES), 0) == _octet_group((LANES, LANES), 1)

        def lag_op(d, k):
            ab = pf_ref[d, S5_CHUNK - 1 - k] if d == 0 else pf_ref[d, k]
            ah, al = _split(ab)
            ch, cl = _split(cm_ref[d])
            return jnp.where(same_group, _dot_nt(ah, ch) + (_dot_nt(ah, cl) + _dot_nt(al, ch)), 0.0)

        fwd = [lag_op(0, k) for k in range(S5_CHUNK)]
        bwd = [lag_op(1, k) for k in range(S5_CHUNK)]
        blocks = {0: (fwd[0] + bwd[0]).astype(BF16)}
        for k in range(1, S5_CHUNK):
            blocks[k] = fwd[k].astype(BF16)
            blocks[-k] = bwd[k].astype(BF16)
        for t in range(S5_CHUNK):
            for t2 in range(S5_CHUNK):
                m_scr[t * LANES:(t + 1) * LANES, t2 * LANES:(t2 + 1) * LANES] = blocks[t2 - t]
        lane_group = _octet_group((LANES, LANES), 1)
        for d in range(2):
            for t in range(S5_CHUNK):
                blk = qf_ref[d, t]
                for g in range(gpo):
                    q_scr[d, g * LANES:(g + 1) * LANES, t * LANES:(t + 1) * LANES] = (
                        jnp.where(lane_group == g, blk, 0.0).astype(BF16))

    u = _s5_ucat(z_ref)
    rows = u.shape[0]
    width = gpo * 2 * S5_STATE

    def entering_state(refs):
        s = refs[0][...]
        for ref, start in zip(refs[1:], starts[1:]):
            s = jnp.where(pl.program_id(1) >= start, ref[...], s)
        return s.reshape(rows, width).astype(BF16)

    y = _dot(u, m_scr[...])
    y += _dot(entering_state(sf_refs), q_scr[0])
    y += _dot(entering_state(sb_refs), q_scr[1])
    for t in range(S5_CHUNK):
        y_ref[:, t, :] = y[:, t * LANES:(t + 1) * LANES] + z_ref[:, t, :] * d_ref[...]


def _s5_out(z3, pf, cmat, qf, d_oct, sin_f, sin_b, l, tr):
    r16 = z3.shape[0]
    gpo = S5_GROUPS // S5_OCTETS
    w = 2 * S5_STATE
    tiles = [s.shape[0] // tr for s in sin_f]
    starts = tuple(sum(tiles[:n]) for n in range(len(tiles)))
    sspecs = [pl.BlockSpec((tr, gpo, w), functools.partial(
        lambda o, i, start, count: (jnp.clip(i - start, 0, count - 1), o, 0), start=start, count=count))
        for start, count in zip(starts, tiles)]
    return pl.pallas_call(
        functools.partial(_s5_out_kernel, starts=starts),
        grid=(S5_OCTETS, r16 // tr),
        in_specs=[pl.BlockSpec((tr, S5_CHUNK, LANES), lambda o, i: (i, 0, S5_COL0 // LANES + o)),
                  pl.BlockSpec((None, 2, S5_CHUNK, LANES, w), lambda o, i: (l, 0, 0, o, 0)),
                  pl.BlockSpec((None, 2, LANES, w), lambda o, i: (l, 0, o, 0)),
                  pl.BlockSpec((None, 2, S5_CHUNK, w, LANES), lambda o, i: (l, 0, 0, 0, o)),
                  pl.BlockSpec((None, None, 1, LANES), lambda o, i: (l, o, 0, 0)),
                  *sspecs, *sspecs],
        out_specs=pl.BlockSpec((tr, S5_CHUNK, LANES), lambda o, i: (i, 0, o)),
        out_shape=jax.ShapeDtypeStruct((r16, S5_CHUNK, D_BRANCH), F32),
        scratch_shapes=[pltpu.VMEM((S5_OCT_K, S5_OCT_K), BF16), pltpu.VMEM((2, gpo * w, S5_OCT_K), BF16)],
        compiler_params=_cparams(2, 2 * _nbytes((tr, S5_CHUNK, LANES), F32), 2 * _nbytes((2, S5_CHUNK, LANES, w), F32),
                                 2 * len(tiles) * _nbytes((tr, gpo, w), F32),
                                 extra=_nbytes((S5_OCT_K, S5_OCT_K), BF16) + _nbytes((2, gpo * w, S5_OCT_K), BF16)
                                 + 4 * _nbytes((tr, S5_OCT_K), F32)),
        name="s5_out",
    )(z3, pf, cmat, qf, d_oct, *sin_f, *sin_b)


def _s5_glu_kernel(y_ref, w_ref, b_ref, o_ref, wb_ref):
    @pl.when(pl.program_id(0) == 0)
    def _():
        wb_ref[...] = w_ref[...].astype(BF16)

    y = _gelu(y_ref[...])
    o_ref[...] = (y * _sigmoid(_dot(y.astype(BF16), wb_ref[...]) + b_ref[...])).astype(BF16)


def _s5_glu(cfg, y, w, b, l):
    tm = min(512, cfg.ll)
    return pl.pallas_call(
        _s5_glu_kernel,
        grid=(cfg.r // tm,),
        in_specs=[pl.BlockSpec((tm, D_BRANCH), lambda i: (i, 0)),
                  pl.BlockSpec((None, D_BRANCH, D_BRANCH), lambda i: (l, 0, 0)),
                  pl.BlockSpec((None, 1, D_BRANCH), lambda i: (l, 0, 0))],
        out_specs=pl.BlockSpec((tm, D_BRANCH), lambda i: (i, 0)),
        out_shape=jax.ShapeDtypeStruct((cfg.r, D_BRANCH), BF16),
        scratch_shapes=[pltpu.VMEM((D_BRANCH, D_BRANCH), BF16)],
        compiler_params=_cparams(1, 2 * _nbytes((tm, D_BRANCH), F32), _nbytes((D_BRANCH, D_BRANCH), F32),
                                 extra=_nbytes((D_BRANCH, D_BRANCH), BF16) + 4 * _nbytes((tm, D_BRANCH), F32)),
        name="s5_glu",
    )(y, w, b.reshape(DEPTH, 1, D_BRANCH))


def _s5(cfg, s5m, z, p, h0_lat, l):
    pf, qf, cmat, a_chunk = s5m
    r16 = cfg.r // S5_CHUNK
    tr = min(256, cfg.ll // S5_CHUNK)
    z3 = z.reshape(r16, S5_CHUNK, N_IN)
    x = _s5_in(z3, pf, l, tr)
    njc, njl = cfg.lc // S5_CHUNK, cfg.ll // S5_CHUNK
    rc16 = cfg.rc // S5_CHUNK
    zero_h0 = jnp.zeros((2, cfg.nc, S5_GROUPS, 2 * S5_STATE), F32)
    sf_c, sb_c, fin_ctx = _s5_scan(x, a_chunk, zero_h0, l, 0, 1, cfg.nc, njc, njc)
    sf_l, sb_l, _ = _s5_scan(x, a_chunk, h0_lat, l, rc16, cfg.nl, 1, njl, min(32, njl))
    sin_f = list(sf_c) + list(sf_l)
    sin_b = list(sb_c) + list(sb_l)
    d_oct = p["s5_d"].reshape(DEPTH, S5_OCTETS, 1, LANES)
    y = _s5_out(z3, pf, cmat, qf, d_oct, sin_f, sin_b, l, tr).reshape(cfg.r, D_BRANCH)
    return _s5_glu(cfg, y, p["s5_glu_w"], p["s5_glu_b"], l), fin_ctx


def _forward(cfg, x_prompt, x_sample, c, cache_k, cache_v, state_ssm_re, state_ssm_im, c_ctx, p):
    d = D_MODEL
    x = (x_prompt.reshape(cfg.rc, d), x_sample.reshape(cfg.rl, d))
    cvec = jnp.concatenate([c_ctx[None], c, jnp.zeros((MOD_ROWS - cfg.nseg, d), F32)], axis=0)
    mod = _mod(cvec, p["w_mod"], p["b_mod"])[:, :cfg.nseg].reshape(DEPTH, cfg.nseg, 6, 1, d)
    mod = [[mod[l, :, i] for i in range(6)] for l in range(DEPTH)]
    norm_g = p["norm_g"]

    consts = {
        "dft": {},
        "feats": {n: _hy_feats(n) for n in {cfg.lc, cfg.ll}},
        "rope": _rope_tables(cfg.ll),
    }
    for n in {cfg.lc, cfg.ll}:
        consts["dft"][n] = _dft_tables(n)
    s5m = _s5_operands(p)

    kv_shape = (cfg.nc, cfg.lc, N_KV_HEADS, HEAD_DIM)
    ks, vs, srs, sis = [], [], [], []
    h = _resid_norm(cfg, x, gpre=norm_g[0, 0], sc=mod[0][1], sh=mod[0][0])
    for l in range(DEPTH):
        z = _mm(h, p["w_in"], l, tm=cfg.tm, tn=512)
        ks.append(z[:cfg.rc, K_COL0:K_COL0 + KV_COLS].reshape(kv_shape))
        vs.append(z[:cfg.rc, V_COL0:V_COL0 + KV_COLS].reshape(kv_shape))

        y_hy = _hyena(cfg, consts, z, p, l)
        sink = jnp.broadcast_to(p["attn_sink"][l][:, None], (N_Q_HEADS, LANES))
        past = cache_k.shape[2]
        y_at = (_attn_ctx(cfg, z, sink),
                _attn_lat(cfg, consts, z, cache_k[:, l].reshape(cfg.nl, past, KV_COLS),
                          cache_v[:, l].reshape(cfg.nl, past, KV_COLS), sink))
        h0 = jnp.concatenate([state_ssm_re[:, l], state_ssm_im[:, l]], axis=-1).transpose(1, 0, 2, 3)
        y_s5, fin = _s5(cfg, s5m, z, p, h0, l)
        srs.append(fin[..., :S5_STATE].transpose(1, 0, 2, 3))
        sis.append(fin[..., S5_STATE:].transpose(1, 0, 2, 3))

        merged = _merge(cfg, z, y_hy, y_at, y_s5,
                        (p["w_branch_hy"], p["w_branch_attn"], p["w_branch_s5"]), l)
        y = _mm(merged, p["w_out"], l, tm=cfg.tm, tn=512)
        x, h = _resid_norm(cfg, x, y, gate=mod[l][2], gpost=norm_g[l, 1],
                           gpre=norm_g[l, 2], sc=mod[l][4], sh=mod[l][3])
        u = _mm(h, p["ffn_w_up"], l, tm=cfg.tm, tn=512)
        act = _ffn_act(cfg, u, p["ffn_conv_w"], p["ffn_conv_b"], l)
        f = _mm(act, p["ffn_w_down"], l, tm=min(512, cfg.tm), tn=512, w_buffers=1)
        if l + 1 < DEPTH:
            x, h = _resid_norm(cfg, x, f, gate=mod[l][5], gpost=norm_g[l, 3],
                               gpre=norm_g[l + 1, 0], sc=mod[l + 1][1], sh=mod[l + 1][0])
        else:
            x = _resid_norm(cfg, x, f, gate=mod[l][5], gpost=norm_g[l, 3], split_out=True)

    return (x[0].reshape(cfg.nc, cfg.lc, d), x[1].reshape(cfg.nl, cfg.ll, d),
            jnp.stack(ks, axis=1), jnp.stack(vs, axis=1), jnp.stack(srs, axis=1), jnp.stack(sis, axis=1))


def kernel(x_prompt, x_sample, c, cache_k, cache_v, state_ssm_re, state_ssm_im, c_ctx, w_mod, b_mod, norm_g, w_in, hy_conv_w, hy_conv_b, hy_w1, hy_b1, hy_w2, hy_b2, hy_w3, hy_freq, hy_decay, hy_bias, attn_sink, s5_lam_re, s5_lam_im, s5_log_dt, s5_b_re, s5_b_im, s5_c_re, s5_c_im, s5_d, s5_glu_w, s5_glu_b, w_branch_hy, w_branch_attn, w_branch_s5, w_out, ffn_w_up, ffn_conv_w, ffn_conv_b, ffn_w_down):
    p = dict(w_mod=w_mod, b_mod=b_mod, norm_g=norm_g, w_in=w_in, hy_conv_w=hy_conv_w, hy_conv_b=hy_conv_b,
             hy_w1=hy_w1, hy_b1=hy_b1, hy_w2=hy_w2, hy_b2=hy_b2, hy_w3=hy_w3, hy_freq=hy_freq,
             hy_decay=hy_decay, hy_bias=hy_bias, attn_sink=attn_sink, s5_lam_re=s5_lam_re,
             s5_lam_im=s5_lam_im, s5_log_dt=s5_log_dt, s5_b_re=s5_b_re, s5_b_im=s5_b_im, s5_c_re=s5_c_re,
             s5_c_im=s5_c_im, s5_d=s5_d, s5_glu_w=s5_glu_w, s5_glu_b=s5_glu_b, w_branch_hy=w_branch_hy,
             w_branch_attn=w_branch_attn, w_branch_s5=w_branch_s5, w_out=w_out, ffn_w_up=ffn_w_up,
             ffn_conv_w=ffn_conv_w, ffn_conv_b=ffn_conv_b, ffn_w_down=ffn_w_down)
    cfg = _Cfg(x_prompt.shape[0], x_prompt.shape[1], x_sample.shape[0], x_sample.shape[1])
    return _forward(cfg, x_prompt, x_sample, c, cache_k, cache_v, state_ssm_re, state_ssm_im, c_ctx, p)
```
